```python
import jax, jax.numpy as jnp
from jax import lax
import numpy as np

D_MODEL = 1024
BATCH = 4
SEQ = 4096
DEPTH = 4

CONV_CH = 512
CONV_K = 31
ATT_HEADS = 8
HEAD_DIM = 64
ATT_WIDTH = ATT_HEADS * HEAD_DIM
Q_BLOCK = 128
SG_GROUPS = 8
SG_WIDTH = 512
SG_GROUP_DIM = SG_WIDTH // SG_GROUPS
SG_CHUNK = 128
N_BRANCH = 3
OFF_CONV = 0
OFF_Q = OFF_CONV + 2 * CONV_CH
OFF_K = OFF_Q + ATT_WIDTH
OFF_V = OFF_K + ATT_WIDTH
OFF_F = OFF_V + ATT_WIDTH
OFF_SG = OFF_F + ATT_HEADS
OFF_GATE = OFF_SG + 2 * SG_WIDTH
IN_COLS = OFF_GATE + N_BRANCH * D_MODEL
D_FF = 2816
N_EXPERTS = 8
TOP_K = 2
D_FF_EXPERT = 3584
MOE_BLOCK = 256
N_DENSE = (DEPTH + 1) // 2
N_MOE = DEPTH // 2
EPS = 1e-6

kernel_name = "hybrid_conv_fox_gmlp_moe_trunk"


def rms_norm(x, g):
    xf = x.astype(jnp.float32)
    y = xf * lax.rsqrt(jnp.mean(xf * xf, axis=-1, keepdims=True) + EPS)
    return (y * g.astype(jnp.float32)).astype(x.dtype)


def layer_norm(x, g, b):
    xf = x.astype(jnp.float32)
    mu = jnp.mean(xf, axis=-1, keepdims=True)
    var = jnp.mean(jnp.square(xf - mu), axis=-1, keepdims=True)
    y = (xf - mu) * lax.rsqrt(var + EPS)
    return (y * g.astype(jnp.float32) + b.astype(jnp.float32)).astype(x.dtype)


def conformer_conv_branch(a, conv_w, conv_b, ln_g, ln_b, w_out):
    h = a[..., :CONV_CH] * jax.nn.sigmoid(a[..., CONV_CH:])
    h = lax.conv_general_dilated(
        h, conv_w[:, None, :], window_strides=(1,), padding=[(CONV_K - 1, 0)],
        dimension_numbers=("NWC", "WIO", "NWC"), feature_group_count=CONV_CH) + conv_b
    h = jax.nn.silu(layer_norm(h, ln_g, ln_b))
    return h @ w_out


def forgetting_attention(q, k, v, f_logit, w_o):
    B, S, _ = q.shape
    q = q.reshape(B, S, ATT_HEADS, HEAD_DIM)
    k = k.reshape(B, S, ATT_HEADS, HEAD_DIM)
    v = v.reshape(B, S, ATT_HEADS, HEAD_DIM)
    c = jnp.cumsum(jax.nn.log_sigmoid(f_logit.astype(jnp.float32)), axis=1)
    c_k = c.transpose(0, 2, 1)
    n_blk = S // Q_BLOCK
    q_blocks = q.reshape(B, n_blk, Q_BLOCK, ATT_HEADS, HEAD_DIM).transpose(1, 0, 2, 3, 4)
    c_blocks = c.reshape(B, n_blk, Q_BLOCK, ATT_HEADS).transpose(1, 0, 3, 2)
    k_pos = jnp.arange(S)
    scale = HEAD_DIM ** -0.5

    def query_block(args):
        i, q_i, c_i = args
        s = jnp.einsum("bthd,bshd->bhts", q_i, k).astype(jnp.float32) * scale
        s = s + c_i[..., :, None] - c_k[..., None, :]
        q_pos = i * Q_BLOCK + jnp.arange(Q_BLOCK)
        s = jnp.where(k_pos[None, :] <= q_pos[:, None], s, -jnp.inf)
        p = jax.nn.softmax(s, axis=-1).astype(v.dtype)
        return jnp.einsum("bhts,bshd->bthd", p, v)

    o = lax.map(query_block, (jnp.arange(n_blk), q_blocks, c_blocks))
    o = o.transpose(1, 0, 2, 3, 4).reshape(B, S, ATT_WIDTH)
    return o @ w_o


def spatial_gating_branch(z, ln_g, ln_b, w_s, b_s, w_out):
    z = jax.nn.gelu(z, approximate=False)
    u, v = z[..., :SG_WIDTH], z[..., SG_WIDTH:]
    v = layer_norm(v, ln_g, ln_b)
    B, S, _ = v.shape
    n_chunk = S // SG_CHUNK
    v = v.reshape(B, n_chunk, SG_CHUNK, SG_GROUPS, SG_GROUP_DIM)
    causal = jnp.tril(jnp.ones((SG_CHUNK, SG_CHUNK), dtype=bool))
    w = jnp.where(causal[None], w_s, 0)
    mixed = jnp.einsum("gts,bnsgc->bntgc", w, v) + b_s.T[None, None, :, :, None]
    out = u * mixed.reshape(B, S, SG_WIDTH)
    return out @ w_out


def swiglu(h, w1, w3, w2):
    return (jax.nn.silu(h @ w1) * (h @ w3)) @ w2


def moe_swiglu(h, w_router, w1, w3, w2):
    B, S, D = h.shape
    N = B * S
    hf = h.reshape(N, D)
    logits = (hf @ w_router).astype(jnp.float32)
    top_v, top_i = lax.top_k(logits, TOP_K)
    gates = jax.nn.softmax(top_v, axis=-1).astype(h.dtype)
    flat_e = top_i.reshape(-1)
    flat_g = gates.reshape(-1)
    flat_t = jnp.arange(N * TOP_K, dtype=jnp.int32) // TOP_K
    order = jnp.argsort(flat_e)
    e_s, t_s, g_s = flat_e[order], flat_t[order], flat_g[order]
    counts = jnp.bincount(flat_e, length=N_EXPERTS).astype(jnp.int32)
    padded = (counts + MOE_BLOCK - 1) // MOE_BLOCK * MOE_BLOCK
    start = jnp.cumsum(counts) - counts
    pad_end = jnp.cumsum(padded)
    pad_start = pad_end - padded
    dest = pad_start[e_s] + jnp.arange(N * TOP_K, dtype=jnp.int32) - start[e_s]
    n_blk = (N * TOP_K + MOE_BLOCK - 1) // MOE_BLOCK + N_EXPERTS
    P = n_blk * MOE_BLOCK
    buf_t = jnp.zeros((P,), jnp.int32).at[dest].set(t_s)
    buf_g = jnp.zeros((P,), h.dtype).at[dest].set(g_s)
    blk_start = jnp.arange(n_blk, dtype=jnp.int32) * MOE_BLOCK
    blk_e = jnp.minimum(jnp.searchsorted(pad_end, blk_start, side="right"), N_EXPERTS - 1)
    xs = hf[buf_t].reshape(n_blk, MOE_BLOCK, D)

    def expert_block(args):
        x_b, e = args
        return (jax.nn.silu(x_b @ w1[e]) * (x_b @ w3[e])) @ w2[e]

    y = lax.map(expert_block, (xs, blk_e)).reshape(P, D) * buf_g[:, None]
    return jnp.zeros((N, D), h.dtype).at[buf_t].add(y).reshape(B, S, D)


def setup_inputs(seed: int = 0) -> dict:
    key = jax.random.key(seed)
    ks = jax.random.split(key, 32)
    f32 = jnp.float32

    def nrm(k, shape, scale):
        return jax.random.normal(k, shape, f32) * scale

    def gain(k, shape):
        return 1.0 + 0.02 * jax.random.normal(k, shape, f32)

    L = DEPTH
    return {
        "x": nrm(ks[0], (BATCH, SEQ, D_MODEL), 1.0),
        "mix_norm_g": gain(ks[1], (L, D_MODEL)),
        "w_in": nrm(ks[2], (L, D_MODEL, IN_COLS), D_MODEL ** -0.5),
        "b_forget": jax.random.uniform(ks[3], (L, ATT_HEADS), f32, 2.0, 5.0),
        "b_gate": nrm(ks[4], (L, N_BRANCH * D_MODEL), 0.02),
        "conv_w": nrm(ks[5], (L, CONV_K, CONV_CH), CONV_K ** -0.5),
        "conv_b": nrm(ks[6], (L, CONV_CH), 0.02),
        "conv_ln_g": gain(ks[7], (L, CONV_CH)),
        "conv_ln_b": nrm(ks[8], (L, CONV_CH), 0.02),
        "w_conv_out": nrm(ks[9], (L, CONV_CH, D_MODEL), CONV_CH ** -0.5),
        "w_att_out": nrm(ks[10], (L, ATT_WIDTH, D_MODEL), ATT_WIDTH ** -0.5),
        "sg_ln_g": gain(ks[11], (L, SG_WIDTH)),
        "sg_ln_b": nrm(ks[12], (L, SG_WIDTH), 0.02),
        "w_spatial": nrm(ks[13], (L, SG_GROUPS, SG_CHUNK, SG_CHUNK), SG_CHUNK ** -0.5),
        "b_spatial": gain(ks[14], (L, SG_GROUPS, SG_CHUNK)),
        "w_sg_out": nrm(ks[15], (L, SG_WIDTH, D_MODEL), SG_WIDTH ** -0.5),
        "w_mix_out": nrm(ks[16], (L, D_MODEL, D_MODEL), D_MODEL ** -0.5),
        "ffn_norm_g": gain(ks[17], (L, D_MODEL)),
        "ffn_w1": nrm(ks[18], (N_DENSE, D_MODEL, D_FF), D_MODEL ** -0.5),
        "ffn_w3": nrm(ks[19], (N_DENSE, D_MODEL, D_FF), D_MODEL ** -0.5),
        "ffn_w2": nrm(ks[20], (N_DENSE, D_FF, D_MODEL), D_FF ** -0.5),
        "router_w": nrm(ks[21], (N_MOE, D_MODEL, N_EXPERTS), D_MODEL ** -0.5),
        "moe_w1": nrm(ks[22], (N_MOE, N_EXPERTS, D_MODEL, D_FF_EXPERT), D_MODEL ** -0.5),
        "moe_w3": nrm(ks[23], (N_MOE, N_EXPERTS, D_MODEL, D_FF_EXPERT), D_MODEL ** -0.5),
        "moe_w2": nrm(ks[24], (N_MOE, N_EXPERTS, D_FF_EXPERT, D_MODEL), D_FF_EXPERT ** -0.5),
        "final_norm_g": gain(ks[25], (D_MODEL,)),
    }


def reference(x, mix_norm_g, w_in, b_forget, b_gate, conv_w, conv_b, conv_ln_g, conv_ln_b,
              w_conv_out, w_att_out, sg_ln_g, sg_ln_b, w_spatial, b_spatial, w_sg_out,
              w_mix_out, ffn_norm_g, ffn_w1, ffn_w3, ffn_w2, router_w, moe_w1, moe_w3, moe_w2,
              final_norm_g):
    B, S, D = x.shape
    for layer in range(DEPTH):
        h = rms_norm(x, mix_norm_g[layer])
        p = h @ w_in[layer]
        y_conv = conformer_conv_branch(p[..., OFF_CONV:OFF_Q], conv_w[layer], conv_b[layer],
                                       conv_ln_g[layer], conv_ln_b[layer], w_conv_out[layer])
        y_att = forgetting_attention(p[..., OFF_Q:OFF_K], p[..., OFF_K:OFF_V], p[..., OFF_V:OFF_F],
                                     p[..., OFF_F:OFF_SG] + b_forget[layer], w_att_out[layer])
        y_sg = spatial_gating_branch(p[..., OFF_SG:OFF_GATE], sg_ln_g[layer], sg_ln_b[layer],
                                     w_spatial[layer], b_spatial[layer], w_sg_out[layer])
        g = jax.nn.sigmoid(p[..., OFF_GATE:] + b_gate[layer]).reshape(B, S, N_BRANCH, D)
        merged = g[..., 0, :] * y_conv + g[..., 1, :] * y_att + g[..., 2, :] * y_sg
        x = x + merged @ w_mix_out[layer]
        h = rms_norm(x, ffn_norm_g[layer])
        if layer % 2 == 0:
            i = layer // 2
            x = x + swiglu(h, ffn_w1[i], ffn_w3[i], ffn_w2[i])
        else:
            i = layer // 2
            x = x + moe_swiglu(h, router_w[i], moe_w1[i], moe_w3[i], moe_w2[i])
    return rms_norm(x, final_norm_g)
```

```python
import functools

import jax
import jax.numpy as jnp
from jax import lax
from jax.experimental import pallas as pl
from jax.experimental.pallas import tpu as pltpu

F32 = jnp.float32
BF16 = jnp.bfloat16

D_MODEL = 1024
CONV_CH = 512
CONV_K = 31
ATT_HEADS = 8
HEAD_DIM = 64
ATT_WIDTH = ATT_HEADS * HEAD_DIM
SG_GROUPS = 8
SG_WIDTH = 512
SG_CHUNK = 128
N_BRANCH = 3
OFF_CONV = 0
OFF_Q = OFF_CONV + 2 * CONV_CH
OFF_K = OFF_Q + ATT_WIDTH
OFF_V = OFF_K + ATT_WIDTH
OFF_F = OFF_V + ATT_WIDTH
OFF_SG = OFF_F + ATT_HEADS
OFF_GATE = OFF_SG + 2 * SG_WIDTH
D_FF = 2816
N_EXPERTS = 8
D_FF_EXPERT = 3584
EPS = 1e-6

LANES = 128
MIB = 1024 * 1024

P_CONV = 0
P_SG = P_CONV + 2 * CONV_CH
P_GATE = P_SG + 2 * SG_WIDTH
P_Q = P_GATE + N_BRANCH * D_MODEL
P_K = P_Q + ATT_WIDTH
P_V = P_K + ATT_WIDTH
P_COLS = P_V + ATT_WIDTH

TM_PROJ = 1024
TN_PROJ = 512
TM_BRANCH = 512
CONV_HIST = 32
CONV_RB = 64
TQ = 256
TK = 512
TM_FFN = 1024
TF_FFN = 256
MOE_BLK = 512
TF_MOE = 512
TM_ROW = 256
F_PAD = 16


def _params(vmem_mib, n_axes):
    return pltpu.CompilerParams(dimension_semantics=("arbitrary",) * n_axes,
                                vmem_limit_bytes=vmem_mib * MIB)


def _rms(x, g):
    return x * lax.rsqrt(jnp.mean(x * x, axis=-1, keepdims=True) + EPS) * g


def _layer_norm(x, g, b):
    mu = jnp.mean(x, axis=-1, keepdims=True)
    d = x - mu
    var = jnp.mean(d * d, axis=-1, keepdims=True)
    return d * lax.rsqrt(var + EPS) * g + b


def _sigmoid(x):
    return 1.0 / (1.0 + jnp.exp(-x))


def _split3(x):
    hi = x.astype(BF16)
    r1 = x - hi.astype(F32)
    mid = r1.astype(BF16)
    lo = (r1 - mid.astype(F32)).astype(BF16)
    return hi, mid, lo


def _inproj_kernel(x_ref, g_ref, w_ref, wft_ref, bf_ref, tri_ref, p_ref, c_ref, xn_ref, carry_ref,
                   *, tiles_per_batch):
    i = pl.program_id(0)
    j = pl.program_id(1)
    tm = x_ref.shape[0]

    @pl.when(j == 0)
    def _():
        xn = _rms(x_ref[...], g_ref[...]).astype(BF16)
        xn_ref[...] = xn
        f = lax.dot_general(wft_ref[...], xn, (((1,), (1,)), ((), ())),
                            preferred_element_type=F32) + bf_ref[...]
        ls = jnp.minimum(f, 0.0) - jnp.log1p(jnp.exp(-jnp.abs(f)))
        tri = tri_ref[...]
        hi, mid, lo = _split3(ls)
        cum = (jnp.dot(hi, tri, preferred_element_type=F32)
               + jnp.dot(mid, tri, preferred_element_type=F32)
               + jnp.dot(lo, tri, preferred_element_type=F32))

        @pl.when(i % tiles_per_batch == 0)
        def _():
            carry_ref[...] = jnp.zeros_like(carry_ref)

        c = cum + carry_ref[...]
        c_ref[...] = c[:ATT_HEADS]
        carry_ref[...] = c[:, tm - 1:tm]

    p_ref[...] = jnp.dot(xn_ref[...], w_ref[...], preferred_element_type=F32).astype(BF16)


def _inproj(x, g, w, wft, bf, tri, seq):
    n = x.shape[0]
    tm, tn = TM_PROJ, TN_PROJ
    kern = functools.partial(_inproj_kernel, tiles_per_batch=seq // tm)
    return pl.pallas_call(
        kern,
        grid=(n // tm, P_COLS // tn),
        in_specs=[
            pl.BlockSpec((tm, D_MODEL), lambda i, j: (i, 0)),
            pl.BlockSpec((1, D_MODEL), lambda i, j: (0, 0)),
            pl.BlockSpec((D_MODEL, tn), lambda i, j: (0, j)),
            pl.BlockSpec((F_PAD, D_MODEL), lambda i, j: (0, 0)),
            pl.BlockSpec((F_PAD, 1), lambda i, j: (0, 0)),
            pl.BlockSpec((tm, tm), lambda i, j: (0, 0)),
        ],
        out_specs=[
            pl.BlockSpec((tm, tn), lambda i, j: (i, j)),
            pl.BlockSpec((ATT_HEADS, tm), lambda i, j: (0, i)),
        ],
        out_shape=[
            jax.ShapeDtypeStruct((n, P_COLS), BF16),
            jax.ShapeDtypeStruct((ATT_HEADS, n), F32),
        ],
        scratch_shapes=[pltpu.VMEM((tm, D_MODEL), BF16), pltpu.VMEM((F_PAD, 1), F32)],
        compiler_params=_params(40, 2),
        name="inproj",
    )(x, g, w, wft, bf, tri)


def _conv_kernel(a1_ref, a2_ref, w_ref, cb_ref, g_ref, b_ref, o_ref, hext_ref, *, tiles_per_batch):
    i = pl.program_id(0)
    tm = a1_ref.shape[0]

    @pl.when(i % tiles_per_batch == 0)
    def _():
        hext_ref[0:CONV_HIST, :] = jnp.zeros((CONV_HIST, CONV_CH), F32)

    @pl.when(i % tiles_per_batch != 0)
    def _():
        hext_ref[0:CONV_HIST, :] = hext_ref[tm:tm + CONV_HIST, :]

    hext_ref[CONV_HIST:CONV_HIST + tm, :] = (
        a1_ref[...].astype(F32) * _sigmoid(a2_ref[...].astype(F32)))

    g = g_ref[...]
    b = b_ref[...]
    base = CONV_HIST - (CONV_K - 1)
    for r in range(0, tm, CONV_RB):
        acc = jnp.broadcast_to(cb_ref[...], (CONV_RB, CONV_CH))
        for j in range(CONV_K):
            acc = acc + w_ref[j:j + 1, :] * hext_ref[r + base + j:r + base + j + CONV_RB, :]
        y = _layer_norm(acc, g, b)
        o_ref[r:r + CONV_RB, :] = (y * _sigmoid(y)).astype(BF16)


def _conv_branch(p, w, cb, g, b, seq):
    n = p.shape[0]
    tm = TM_BRANCH
    kern = functools.partial(_conv_kernel, tiles_per_batch=seq // tm)
    c0 = P_CONV // CONV_CH
    vec = pl.BlockSpec((1, CONV_CH), lambda i: (0, 0))
    return pl.pallas_call(
        kern,
        grid=(n // tm,),
        in_specs=[
            pl.BlockSpec((tm, CONV_CH), lambda i: (i, c0)),
            pl.BlockSpec((tm, CONV_CH), lambda i: (i, c0 + 1)),
            pl.BlockSpec((CONV_K + 1, CONV_CH), lambda i: (0, 0)),
            vec, vec, vec,
        ],
        out_specs=pl.BlockSpec((tm, CONV_CH), lambda i: (i, 0)),
        out_shape=jax.ShapeDtypeStruct((n, CONV_CH), BF16),
        scratch_shapes=[pltpu.VMEM((tm + CONV_HIST, CONV_CH), F32)],
        compiler_params=_params(32, 1),
        name="conv_branch",
    )(p, p, w, cb, g, b)


def _sg_kernel(u_ref, v_ref, g_ref, b_ref, w_ref, bias_ref, o_ref):
    tm = u_ref.shape[0]

    def gelu(z):
        return 0.5 * z * (1.0 + lax.erf(z * 0.7071067811865476))

    zu = gelu(u_ref[...].astype(F32))
    vn = _layer_norm(gelu(v_ref[...].astype(F32)), g_ref[...], b_ref[...]).astype(BF16)
    lane = lax.broadcasted_iota(jnp.int32, (SG_CHUNK, LANES), 1)
    first_group = lane < (SG_WIDTH // SG_GROUPS)
    for c in range(tm // SG_CHUNK):
        rows = slice(c * SG_CHUNK, (c + 1) * SG_CHUNK)
        for pr in range(SG_WIDTH // LANES):
            cols = slice(pr * LANES, (pr + 1) * LANES)
            vp = vn[rows, cols]
            m0 = jnp.dot(w_ref[2 * pr], vp, preferred_element_type=F32)
            m1 = jnp.dot(w_ref[2 * pr + 1], vp, preferred_element_type=F32)
            mixed = jnp.where(first_group, m0, m1) + bias_ref[:, cols]
            o_ref[rows, cols] = (zu[rows, cols] * mixed).astype(BF16)


def _sg_branch(p, g, b, w_tril, bias_full):
    n = p.shape[0]
    tm = TM_BRANCH
    c0 = P_SG // SG_WIDTH
    vec = pl.BlockSpec((1, SG_WIDTH), lambda i: (0, 0))
    return pl.pallas_call(
        _sg_kernel,
        grid=(n // tm,),
        in_specs=[
            pl.BlockSpec((tm, SG_WIDTH), lambda i: (i, c0)),
            pl.BlockSpec((tm, SG_WIDTH), lambda i: (i, c0 + 1)),
            vec, vec,
            pl.BlockSpec((SG_GROUPS, SG_CHUNK, SG_CHUNK), lambda i: (0, 0, 0)),
            pl.BlockSpec((SG_CHUNK, SG_WIDTH), lambda i: (0, 0)),
        ],
        out_specs=pl.BlockSpec((tm, SG_WIDTH), lambda i: (i, 0)),
        out_shape=jax.ShapeDtypeStruct((n, SG_WIDTH), BF16),
        compiler_params=_params(32, 1),
        name="sg_branch",
    )(p, p, g, b, w_tril, bias_full)


def _attn_kernel(q_ref, k_ref, v_ref, c_ref, o_ref):
    i = pl.program_id(2)
    tq = q_ref.shape[0]
    tk = c_ref.shape[-1]
    q2 = q_ref[...]
    lane = lax.broadcasted_iota(jnp.int32, (tq, LANES), 1)
    row = i * tq + lax.broadcasted_iota(jnp.int32, (tq, tk), 0)
    col = lax.broadcasted_iota(jnp.int32, (tq, tk), 1)
    n_full = (i * tq) // tk
    scale = HEAD_DIM ** -0.5

    outs = []
    for hh in range(2):
        sel = (lane < HEAD_DIM) if hh == 0 else (lane >= HEAD_DIM)
        qh = jnp.where(sel, q2, jnp.zeros_like(q2)) * scale

        def step(j, carry, masked):
            m, l, acc = carry
            start = pl.multiple_of(j * tk, tk)
            ks = k_ref[pl.ds(start, tk), :]
            vs = v_ref[pl.ds(start, tk), :]
            s = lax.dot_general(qh, ks, (((1,), (1,)), ((), ())), preferred_element_type=F32)
            s = s - c_ref[hh, j]
            if masked:
                s = jnp.where(j * tk + col <= row, s, -jnp.inf)
            m_new = jnp.maximum(m, jnp.max(s, axis=1, keepdims=True))
            alpha = jnp.exp(m - m_new)
            pexp = jnp.exp(s - m_new)
            l = alpha * l + jnp.sum(pexp, axis=1, keepdims=True)
            acc = alpha * acc + jnp.dot(pexp.astype(BF16), vs, preferred_element_type=F32)
            return m_new, l, acc

        carry = (jnp.full((tq, 1), -jnp.inf, F32), jnp.zeros((tq, 1), F32),
                 jnp.zeros((tq, LANES), F32))
        carry = lax.fori_loop(0, n_full, functools.partial(step, masked=False), carry)
        _, l, acc = step(n_full, carry, True)
        outs.append(acc / l)
    o_ref[...] = jnp.where(lane < HEAD_DIM, outs[0], outs[1]).astype(BF16)


def _attention(p, c4, batch, seq):
    n = p.shape[0]
    pairs = ATT_WIDTH // LANES
    qt = seq // TQ
    return pl.pallas_call(
        _attn_kernel,
        grid=(batch, pairs, qt),
        in_specs=[
            pl.BlockSpec((TQ, LANES), lambda b, h, i: (b * qt + i, P_Q // LANES + h)),
            pl.BlockSpec((seq, LANES), lambda b, h, i: (b, P_K // LANES + h)),
            pl.BlockSpec((seq, LANES), lambda b, h, i: (b, P_V // LANES + h)),
            pl.BlockSpec((2, seq // TK, 1, TK), lambda b, h, i: (h, b, 0, 0)),
        ],
        out_specs=pl.BlockSpec((TQ, LANES), lambda b, h, i: (b * qt + i, h)),
        out_shape=jax.ShapeDtypeStruct((n, ATT_WIDTH), BF16),
        compiler_params=_params(32, 3),
        name="fox_attention",
    )(p, p, p, c4)


def _merge_kernel(x_ref, hc_ref, ha_ref, hs_ref, g0_ref, g1_ref, g2_ref, bg_ref,
                  wc_ref, wa_ref, ws_ref, wm_ref, o_ref):
    merged = None
    for k, (h_ref, w_ref, gl_ref) in enumerate(((hc_ref, wc_ref, g0_ref), (ha_ref, wa_ref, g1_ref),
                                                (hs_ref, ws_ref, g2_ref))):
        y = jnp.dot(h_ref[...], w_ref[...], preferred_element_type=F32)
        gate = _sigmoid(gl_ref[...].astype(F32) + bg_ref[:, k * D_MODEL:(k + 1) * D_MODEL])
        merged = gate * y if merged is None else merged + gate * y
    o_ref[...] = x_ref[...] + jnp.dot(merged.astype(BF16), wm_ref[...], preferred_element_type=F32)


def _merge(x, hc, ha, hs, p, bg, wc, wa, ws, wm):
    n = x.shape[0]
    tm = TM_BRANCH
    g0 = P_GATE // D_MODEL
    half = lambda: pl.BlockSpec((tm, CONV_CH), lambda i: (i, 0))
    wspec = lambda k: pl.BlockSpec((k, D_MODEL), lambda i: (0, 0))
    return pl.pallas_call(
        _merge_kernel,
        grid=(n // tm,),
        in_specs=[
            pl.BlockSpec((tm, D_MODEL), lambda i: (i, 0)),
            half(), half(), half(),
            pl.BlockSpec((tm, D_MODEL), lambda i: (i, g0)),
            pl.BlockSpec((tm, D_MODEL), lambda i: (i, g0 + 1)),
            pl.BlockSpec((tm, D_MODEL), lambda i: (i, g0 + 2)),
            pl.BlockSpec((1, N_BRANCH * D_MODEL), lambda i: (0, 0)),
            wspec(CONV_CH), wspec(ATT_WIDTH), wspec(SG_WIDTH), wspec(D_MODEL),
        ],
        out_specs=pl.BlockSpec((tm, D_MODEL), lambda i: (i, 0)),
        out_shape=jax.ShapeDtypeStruct((n, D_MODEL), F32),
        compiler_params=_params(48, 1),
        name="merge",
    )(x, hc, ha, hs, p, p, p, bg, wc, wa, ws, wm)


def _ffn_kernel(x_ref, g_ref, w1_ref, w3_ref, w2_ref, o_ref, hn_ref, acc_ref):
    f = pl.program_id(1)

    @pl.when(f == 0)
    def _():
        hn_ref[...] = _rms(x_ref[...], g_ref[...]).astype(BF16)
        acc_ref[...] = jnp.zeros_like(acc_ref)

    hn = hn_ref[...]
    a = jnp.dot(hn, w1_ref[...], preferred_element_type=F32)
    b = jnp.dot(hn, w3_ref[...], preferred_element_type=F32)
    t = (a * _sigmoid(a) * b).astype(BF16)
    acc_ref[...] += jnp.dot(t, w2_ref[...], preferred_element_type=F32)

    @pl.when(f == pl.num_programs(1) - 1)
    def _():
        o_ref[...] = x_ref[...] + acc_ref[...]


def _ffn(x, g, w1, w3, w2):
    n = x.shape[0]
    tm, tf = TM_FFN, TF_FFN
    return pl.pallas_call(
        _ffn_kernel,
        grid=(n // tm, D_FF // tf),
        in_specs=[
            pl.BlockSpec((tm, D_MODEL), lambda i, f: (i, 0)),
            pl.BlockSpec((1, D_MODEL), lambda i, f: (0, 0)),
            pl.BlockSpec((D_MODEL, tf), lambda i, f: (0, f)),
            pl.BlockSpec((D_MODEL, tf), lambda i, f: (0, f)),
            pl.BlockSpec((tf, D_MODEL), lambda i, f: (f, 0)),
        ],
        out_specs=pl.BlockSpec((tm, D_MODEL), lambda i, f: (i, 0)),
        out_shape=jax.ShapeDtypeStruct((n, D_MODEL), F32),
        scratch_shapes=[pltpu.VMEM((tm, D_MODEL), BF16), pltpu.VMEM((tm, D_MODEL), F32)],
        compiler_params=_params(48, 2),
        name="ffn_dense",
    )(x, g, w1, w3, w2)


def _router_kernel(x_ref, g_ref, wr_ref, lt_ref, h_ref, meta_ref, cnt_ref, carry_ref):
    i = pl.program_id(0)
    tm = x_ref.shape[0]

    @pl.when(i == 0)
    def _():
        carry_ref[...] = jnp.zeros_like(carry_ref)

    h = _rms(x_ref[...], g_ref[...])
    h_ref[...] = h
    logits = jnp.dot(h, wr_ref[...], preferred_element_type=F32, precision=lax.Precision.HIGHEST)
    lane = lax.broadcasted_iota(jnp.int32, (tm, LANES), 1)
    lanef = lane.astype(F32)
    lg = jnp.where(lane < N_EXPERTS, logits, -jnp.inf)
    v1 = jnp.max(lg, axis=1, keepdims=True)
    i1 = jnp.min(jnp.where(lg == v1, lanef, float(LANES)), axis=1, keepdims=True)
    lg2 = jnp.where(lanef == i1, -jnp.inf, lg)
    v2 = jnp.max(lg2, axis=1, keepdims=True)
    i2 = jnp.min(jnp.where(lg2 == v2, lanef, float(LANES)), axis=1, keepdims=True)
    e = jnp.exp(v2 - v1)
    g1 = 1.0 / (1.0 + e)
    g2 = e / (1.0 + e)
    oh1 = lanef == i1
    oh2 = lanef == i2
    cnt = (oh1.astype(F32) + oh2.astype(F32))
    before = jnp.dot(lt_ref[...], cnt.astype(BF16), preferred_element_type=F32) + carry_ref[...]
    r1 = jnp.sum(jnp.where(oh1, before, 0.0), axis=1, keepdims=True)
    r2 = jnp.sum(jnp.where(oh2, before, 0.0), axis=1, keepdims=True)
    total = carry_ref[...] + jnp.sum(cnt, axis=0, keepdims=True)
    carry_ref[...] = total
    cnt_ref[...] = jnp.broadcast_to(total, cnt_ref.shape)
    meta = jnp.zeros((tm, LANES), F32)
    for k, val in enumerate((i1, i2, g1, g2, r1, r2)):
        meta = jnp.where(lane == k, val, meta)
    meta_ref[...] = meta


def _router(x, g, wr, lt):
    n = x.shape[0]
    tm = TM_BRANCH
    return pl.pallas_call(
        _router_kernel,
        grid=(n // tm,),
        in_specs=[
            pl.BlockSpec((tm, D_MODEL), lambda i: (i, 0)),
            pl.BlockSpec((1, D_MODEL), lambda i: (0, 0)),
            pl.BlockSpec((D_MODEL, LANES), lambda i: (0, 0)),
            pl.BlockSpec((tm, tm), lambda i: (0, 0)),
        ],
        out_specs=[
            pl.BlockSpec((tm, D_MODEL), lambda i: (i, 0)),
            pl.BlockSpec((tm, LANES), lambda i: (i, 0)),
            pl.BlockSpec((8, LANES), lambda i: (0, 0)),
        ],
        out_shape=[
            jax.ShapeDtypeStruct((n, D_MODEL), F32),
            jax.ShapeDtypeStruct((n, LANES), F32),
            jax.ShapeDtypeStruct((8, LANES), F32),
        ],
        scratch_shapes=[pltpu.VMEM((1, LANES), F32)],
        compiler_params=_params(32, 1),
        name="router",
    )(x, g, wr, lt)


def _row_copy(src_ref, src_row, dst_ref, dst_row, sem):
    return pltpu.make_async_copy(src_ref.at[pl.ds(src_row, 1), :], dst_ref.at[pl.ds(dst_row, 1), :], sem)


def _dispatch_kernel(dest_ref, h_ref, xs_in_ref, xs_ref, sem):
    del xs_in_ref
    tm = h_ref.shape[0]

    def issue(r, carry):
        for k in range(2):
            _row_copy(h_ref, r, xs_ref, dest_ref[2 * r + k], sem).start()
        return carry

    lax.fori_loop(0, tm, issue, 0)

    def drain(r, carry):
        _row_copy(h_ref, 0, xs_ref, 0, sem).wait()
        return carry

    lax.fori_loop(0, 2 * tm, drain, 0)


def _dispatch(dest_flat, h, xs_zero):
    n = h.shape[0]
    tm = TM_ROW
    return pl.pallas_call(
        _dispatch_kernel,
        grid=(n // tm,),
        in_specs=[
            pl.BlockSpec((2 * tm,), lambda i: (i,), memory_space=pltpu.SMEM),
            pl.BlockSpec((tm, D_MODEL), lambda i: (i, 0)),
            pl.BlockSpec(memory_space=pl.ANY),
        ],
        out_specs=pl.BlockSpec(memory_space=pl.ANY),
        out_shape=jax.ShapeDtypeStruct(xs_zero.shape, xs_zero.dtype),
        scratch_shapes=[pltpu.SemaphoreType.DMA(())],
        input_output_aliases={2: 0},
        compiler_params=_params(32, 1),
        name="moe_dispatch",
    )(dest_flat, h, xs_zero)


def _moe_kernel(be_ref, nu_ref, x_ref, w1_ref, w3_ref, w2_ref, y_ref, xb_ref, acc_ref):
    del be_ref
    b = pl.program_id(0)
    f = pl.program_id(1)

    @pl.when(b < nu_ref[0])
    def _():
        @pl.when(f == 0)
        def _():
            xb_ref[...] = x_ref[...].astype(BF16)
            acc_ref[...] = jnp.zeros_like(acc_ref)

        xb = xb_ref[...]
        a = jnp.dot(xb, w1_ref[...], preferred_element_type=F32)
        c = jnp.dot(xb, w3_ref[...], preferred_element_type=F32)
        t = (a * _sigmoid(a) * c).astype(BF16)
        acc_ref[...] += jnp.dot(t, w2_ref[...], preferred_element_type=F32)

        @pl.when(f == pl.num_programs(1) - 1)
        def _():
            y_ref[...] = acc_ref[...]

    @pl.when(jnp.logical_and(b >= nu_ref[0], f == pl.num_programs(1) - 1))
    def _():
        y_ref[...] = jnp.zeros_like(y_ref)


def _moe_experts(blk_e, n_used, xs, w1, w3, w2):
    rows = xs.shape[0]
    n_blk = rows // MOE_BLK
    nf = D_FF_EXPERT // TF_MOE

    def row_idx(b, f, be, nu):
        return (jnp.minimum(b, nu[0] - 1), 0)

    def f_idx(b, f, nu):
        return jnp.where(b < nu[0], f, nf - 1)

    grid_spec = pltpu.PrefetchScalarGridSpec(
        num_scalar_prefetch=2,
        grid=(n_blk, nf),
        in_specs=[
            pl.BlockSpec((MOE_BLK, D_MODEL), row_idx),
            pl.BlockSpec((None, D_MODEL, TF_MOE), lambda b, f, be, nu: (be[b], 0, f_idx(b, f, nu))),
            pl.BlockSpec((None, D_MODEL, TF_MOE), lambda b, f, be, nu: (be[b], 0, f_idx(b, f, nu))),
            pl.BlockSpec((None, TF_MOE, D_MODEL), lambda b, f, be, nu: (be[b], f_idx(b, f, nu), 0)),
        ],
        out_specs=pl.BlockSpec((MOE_BLK, D_MODEL), lambda b, f, be, nu: (b, 0)),
        scratch_shapes=[pltpu.VMEM((MOE_BLK, D_MODEL), BF16), pltpu.VMEM((MOE_BLK, D_MODEL), F32)],
    )
    return pl.pallas_call(
        _moe_kernel,
        grid_spec=grid_spec,
        out_shape=jax.ShapeDtypeStruct((rows, D_MODEL), F32),
        compiler_params=_params(48, 2),
        name="moe_experts",
    )(blk_e, n_used, xs, w1, w3, w2)


def _combine_kernel(dest_ref, x_ref, meta_ref, gfin_ref, y_ref, o_ref, ybuf_ref, sem, *, final_norm):
    tm = x_ref.shape[0]

    def issue(r, carry):
        for k in range(2):
            _row_copy(y_ref, dest_ref[2 * r + k], ybuf_ref.at[k], r, sem).start()
        return carry

    lax.fori_loop(0, tm, issue, 0)

    def drain(r, carry):
        _row_copy(y_ref, 0, ybuf_ref.at[0], 0, sem).wait()
        return carry

    lax.fori_loop(0, 2 * tm, drain, 0)

    g1 = meta_ref[:, 2:3]
    g2 = meta_ref[:, 3:4]
    out = x_ref[...] + (g1 * ybuf_ref[0] + g2 * ybuf_ref[1])
    if final_norm:
        out = _rms(out, gfin_ref[...])
    o_ref[...] = out


def _combine(dest_flat, x, meta, gfin, y, final_norm):
    n = x.shape[0]
    tm = TM_ROW
    kern = functools.partial(_combine_kernel, final_norm=final_norm)
    return pl.pallas_call(
        kern,
        grid=(n // tm,),
        in_specs=[
            pl.BlockSpec((2 * tm,), lambda i: (i,), memory_space=pltpu.SMEM),
            pl.BlockSpec((tm, D_MODEL), lambda i: (i, 0)),
            pl.BlockSpec((tm, LANES), lambda i: (i, 0)),
            pl.BlockSpec((1, D_MODEL), lambda i: (0, 0)),
            pl.BlockSpec(memory_space=pl.ANY),
        ],
        out_specs=pl.BlockSpec((tm, D_MODEL), lambda i: (i, 0)),
        out_shape=jax.ShapeDtypeStruct((n, D_MODEL), F32),
        scratch_shapes=[pltpu.VMEM((2, tm, D_MODEL), F32), pltpu.SemaphoreType.DMA(())],
        compiler_params=_params(32, 1),
        name="moe_combine",
    )(dest_flat, x, meta, gfin, y)


def _moe_layer(x, g, wr, lt, w1, w3, w2, gfin, final_norm):
    n = x.shape[0]
    h, meta, cnt = _router(x, g, wr, lt)
    expert = meta[:, 0:2].astype(jnp.int32)
    rank = meta[:, 4:6].astype(jnp.int32)
    counts = cnt[0, :N_EXPERTS].astype(jnp.int32)
    padded = (counts + MOE_BLK - 1) // MOE_BLK * MOE_BLK
    pad_end = jnp.cumsum(padded)
    pad_start = pad_end - padded
    dest = (pad_start[expert] + rank).reshape(-1)
    n_blk = (2 * n) // MOE_BLK + N_EXPERTS
    blk_start = jnp.arange(n_blk, dtype=jnp.int32) * MOE_BLK
    blk_e = jnp.minimum(jnp.searchsorted(pad_end, blk_start, side="right"), N_EXPERTS - 1)
    n_used = (pad_end[-1:] // MOE_BLK).astype(jnp.int32)
    xs = _dispatch(dest, h, jnp.zeros((n_blk * MOE_BLK, D_MODEL), F32))
    y = _moe_experts(blk_e.astype(jnp.int32), n_used, xs, w1, w3, w2)
    return _combine(dest, x, meta, gfin, y, final_norm)


def kernel(x, mix_norm_g, w_in, b_forget, b_gate, conv_w, conv_b, conv_ln_g, conv_ln_b, w_conv_out,
           w_att_out, sg_ln_g, sg_ln_b, w_spatial, b_spatial, w_sg_out, w_mix_out, ffn_norm_g,
           ffn_w1, ffn_w3, ffn_w2, router_w, moe_w1, moe_w3, moe_w2, final_norm_g):
    batch, seq, d = x.shape
    depth = w_in.shape[0]
    n = batch * seq
    xf = x.reshape(n, d)

    w_main = jnp.concatenate(
        [w_in[:, :, OFF_CONV:OFF_Q], w_in[:, :, OFF_SG:OFF_GATE], w_in[:, :, OFF_GATE:],
         w_in[:, :, OFF_Q:OFF_F]], axis=2).astype(BF16)
    wft = jnp.pad(jnp.swapaxes(w_in[:, :, OFF_F:OFF_SG], 1, 2),
                  ((0, 0), (0, F_PAD - ATT_HEADS), (0, 0))).astype(BF16)
    bfg = jnp.pad(b_forget, ((0, 0), (0, F_PAD - ATT_HEADS)))[:, :, None]
    tri = jnp.triu(jnp.ones((TM_PROJ, TM_PROJ), F32)).astype(BF16)
    lt = jnp.tril(jnp.ones((TM_BRANCH, TM_BRANCH), F32), -1).astype(BF16)
    conv_w_p = jnp.pad(conv_w, ((0, 0), (0, 1), (0, 0)))
    causal = jnp.tril(jnp.ones((SG_CHUNK, SG_CHUNK), bool))
    w_sp = jnp.where(causal[None, None], w_spatial, 0).astype(BF16)
    sg_bias = jnp.repeat(jnp.swapaxes(b_spatial, 1, 2), SG_WIDTH // SG_GROUPS, axis=2)
    wc, wa, ws, wm = (w.astype(BF16) for w in (w_conv_out, w_att_out, w_sg_out, w_mix_out))
    f1, f3, f2 = (w.astype(BF16) for w in (ffn_w1, ffn_w3, ffn_w2))
    m1, m3, m2 = (w.astype(BF16) for w in (moe_w1, moe_w3, moe_w2))
    wr = jnp.pad(router_w, ((0, 0), (0, 0), (0, LANES - N_EXPERTS)))
    gfin = final_norm_g[None, :]

    for layer in range(depth):
        row = lambda v: v[layer][None, :]
        p, c = _inproj(xf, row(mix_norm_g), w_main[layer], wft[layer], bfg[layer], tri, seq)
        hc = _conv_branch(p, conv_w_p[layer], row(conv_b), row(conv_ln_g), row(conv_ln_b), seq)
        hs = _sg_branch(p, row(sg_ln_g), row(sg_ln_b), w_sp[layer], sg_bias[layer])
        c4 = c.reshape(ATT_HEADS, n // TK, 1, TK)
        ha = _attention(p, c4, batch, seq)
        xf = _merge(xf, hc, ha, hs, p, row(b_gate), wc[layer], wa[layer], ws[layer], wm[layer])
        i = layer // 2
        if layer % 2 == 0:
            xf = _ffn(xf, row(ffn_norm_g), f1[i], f3[i], f2[i])
        else:
            xf = _moe_layer(xf, row(ffn_norm_g), wr[i], lt, m1[i], m3[i], m2[i], gfin,
                            final_norm=(layer == depth - 1))
    if depth % 2 == 1:
        raise NotImplementedError("final norm is fused into the last (expert) layer")
    return xf.reshape(batch, seq, d)
```

```python
import functools

import jax
import jax.numpy as jnp
from jax import lax
from jax.experimental import pallas as pl
from jax.experimental.pallas import tpu as pltpu

F32 = jnp.float32
BF16 = jnp.bfloat16

D_MODEL = 1024
CONV_CH = 512
CONV_K = 31
ATT_HEADS = 8
HEAD_DIM = 64
ATT_WIDTH = ATT_HEADS * HEAD_DIM
SG_GROUPS = 8
SG_WIDTH = 512
SG_CHUNK = 128
N_BRANCH = 3
OFF_CONV = 0
OFF_Q = OFF_CONV + 2 * CONV_CH
OFF_K = OFF_Q + ATT_WIDTH
OFF_V = OFF_K + ATT_WIDTH
OFF_F = OFF_V + ATT_WIDTH
OFF_SG = OFF_F + ATT_HEADS
OFF_GATE = OFF_SG + 2 * SG_WIDTH
D_FF = 2816
N_EXPERTS = 8
D_FF_EXPERT = 3584
EPS = 1e-6

LANES = 128
MIB = 1024 * 1024

P_CONV = 0
P_SG = P_CONV + 2 * CONV_CH
P_GATE = P_SG + 2 * SG_WIDTH
HEAD_PAD = LANES
P_Q = P_GATE + N_BRANCH * D_MODEL
P_K = P_Q + ATT_HEADS * HEAD_PAD
P_COLS = P_K + ATT_HEADS * HEAD_PAD
N_SPLIT = 3
LOG2E = 1.4426950408889634

TM_PROJ = 1024
TN_PROJ = 512
TM_BRANCH = 512
CONV_HIST = 32
CONV_RB = 64
TQ = 512
TK = 512
TM_FFN = 1024
TF_FFN = 256
MOE_BLK = 512
TF_MOE = 512
TM_ROW = 256


def _params(vmem_mib, n_axes):
    return pltpu.CompilerParams(dimension_semantics=("arbitrary",) * n_axes,
                                vmem_limit_bytes=vmem_mib * MIB)


def _rms(x, g):
    return x * lax.rsqrt(jnp.mean(x * x, axis=-1, keepdims=True) + EPS) * g


def _layer_norm(x, g, b):
    mu = jnp.mean(x, axis=-1, keepdims=True)
    d = x - mu
    var = jnp.mean(d * d, axis=-1, keepdims=True)
    return d * lax.rsqrt(var + EPS) * g + b


def _sigmoid(x):
    return 1.0 / (1.0 + jnp.exp(-x))


def _split3(x):
    hi = x.astype(BF16)
    r1 = x - hi.astype(F32)
    mid = r1.astype(BF16)
    lo = (r1 - mid.astype(F32)).astype(BF16)
    return hi, mid, lo


def _inproj_kernel(x_ref, g_ref, w_ref, sc_ref, bi_ref, wf_ref, bf_ref, lt_ref, pl_ref, wvt_ref,
                   vb_ref, p_ref, vt_ref, xn_ref, c3_ref, carry_ref, *, tiles_per_batch, k_tile0):
    i = pl.program_id(0)
    j = pl.program_id(1)
    tm = x_ref.shape[0]
    tk = vt_ref.shape[-1]

    @pl.when(j == 0)
    def _():
        xn = _rms(x_ref[...], g_ref[...]).astype(BF16)
        xn_ref[...] = xn
        f = jnp.dot(xn, wf_ref[...], preferred_element_type=F32) + bf_ref[...]
        ls = jnp.minimum(f, 0.0) - jnp.log1p(jnp.exp(-jnp.abs(f)))
        lt = lt_ref[...]
        cum = sum(jnp.dot(lt, term, preferred_element_type=F32) for term in _split3(ls))

        @pl.when(i % tiles_per_batch == 0)
        def _():
            carry_ref[...] = jnp.zeros_like(carry_ref)

        c = cum + carry_ref[...]
        carry_ref[...] = c[tm - 1:tm, :]
        for r, term in enumerate(_split3(c * LOG2E)):
            c3_ref[r] = term
        vt = lax.dot_general(wvt_ref[...], xn, (((1,), (1,)), ((), ())),
                             preferred_element_type=F32) + vb_ref[...]
        for ch in range(tm // tk):
            vt_ref[ch] = vt[:, ch * tk:(ch + 1) * tk].astype(BF16)

    acc = jnp.dot(xn_ref[...], w_ref[...], preferred_element_type=F32) * sc_ref[...] + bi_ref[...]

    @pl.when(j < k_tile0)
    def _():
        p_ref[...] = acc.astype(BF16)

    @pl.when(j >= k_tile0)
    def _():
        decay = sum(jnp.dot(c3_ref[r], pl_ref[r], preferred_element_type=F32) for r in range(N_SPLIT))
        p_ref[...] = (acc + decay).astype(BF16)


def _inproj(x, g, w, col_scale, col_bias, wf, bf, lt, place, wvt, vbias, seq):
    n = x.shape[0]
    tm, tn = TM_PROJ, TN_PROJ
    k_tile0 = P_K // tn
    kern = functools.partial(_inproj_kernel, tiles_per_batch=seq // tm, k_tile0=k_tile0)
    vt_rows = ATT_HEADS * HEAD_PAD
    const = lambda shape: pl.BlockSpec(shape, lambda i, j: (0,) * len(shape))
    return pl.pallas_call(
        kern,
        grid=(n // tm, P_COLS // tn),
        in_specs=[
            pl.BlockSpec((tm, D_MODEL), lambda i, j: (i, 0)),
            const((1, D_MODEL)),
            pl.BlockSpec((D_MODEL, tn), lambda i, j: (0, j)),
            pl.BlockSpec((1, tn), lambda i, j: (0, j)),
            pl.BlockSpec((1, tn), lambda i, j: (0, j)),
            const((D_MODEL, LANES)),
            const((1, LANES)),
            const((tm, tm)),
            pl.BlockSpec((N_SPLIT, LANES, tn), lambda i, j: (0, 0, jnp.maximum(j - k_tile0, 0))),
            const((vt_rows, D_MODEL)),
            const((vt_rows, 1)),
        ],
        out_specs=[
            pl.BlockSpec((tm, tn), lambda i, j: (i, j)),
            pl.BlockSpec((tm // TK, vt_rows, TK), lambda i, j: (i, 0, 0)),
        ],
        out_shape=[
            jax.ShapeDtypeStruct((n, P_COLS), BF16),
            jax.ShapeDtypeStruct((n // TK, vt_rows, TK), BF16),
        ],
        scratch_shapes=[pltpu.VMEM((tm, D_MODEL), BF16), pltpu.VMEM((N_SPLIT, tm, LANES), BF16),
                        pltpu.VMEM((1, LANES), F32)],
        compiler_params=_params(48, 2),
        name="inproj",
    )(x, g, w, col_scale, col_bias, wf, bf, lt, place, wvt, vbias)


def _conv_kernel(a1_ref, a2_ref, w_ref, cb_ref, g_ref, b_ref, o_ref, hext_ref, *, tiles_per_batch):
    i = pl.program_id(0)
    tm = a1_ref.shape[0]

    @pl.when(i % tiles_per_batch == 0)
    def _():
        hext_ref[0:CONV_HIST, :] = jnp.zeros((CONV_HIST, CONV_CH), F32)

    @pl.when(i % tiles_per_batch != 0)
    def _():
        hext_ref[0:CONV_HIST, :] = hext_ref[tm:tm + CONV_HIST, :]

    hext_ref[CONV_HIST:CONV_HIST + tm, :] = (
        a1_ref[...].astype(F32) * _sigmoid(a2_ref[...].astype(F32)))

    g = g_ref[...]
    b = b_ref[...]
    base = CONV_HIST - (CONV_K - 1)
    for r in range(0, tm, CONV_RB):
        acc = jnp.broadcast_to(cb_ref[...], (CONV_RB, CONV_CH))
        for j in range(CONV_K):
            acc = acc + w_ref[j:j + 1, :] * hext_ref[r + base + j:r + base + j + CONV_RB, :]
        y = _layer_norm(acc, g, b)
        o_ref[r:r + CONV_RB, :] = (y * _sigmoid(y)).astype(BF16)


def _conv_branch(p, w, cb, g, b, seq):
    n = p.shape[0]
    tm = TM_BRANCH
    kern = functools.partial(_conv_kernel, tiles_per_batch=seq // tm)
    c0 = P_CONV // CONV_CH
    vec = pl.BlockSpec((1, CONV_CH), lambda i: (0, 0))
    return pl.pallas_call(
        kern,
        grid=(n // tm,),
        in_specs=[
            pl.BlockSpec((tm, CONV_CH), lambda i: (i, c0)),
            pl.BlockSpec((tm, CONV_CH), lambda i: (i, c0 + 1)),
            pl.BlockSpec((CONV_K + 1, CONV_CH), lambda i: (0, 0)),
            vec, vec, vec,
        ],
        out_specs=pl.BlockSpec((tm, CONV_CH), lambda i: (i, 0)),
        out_shape=jax.ShapeDtypeStruct((n, CONV_CH), BF16),
        scratch_shapes=[pltpu.VMEM((tm + CONV_HIST, CONV_CH), F32)],
        compiler_params=_params(32, 1),
        name="conv_branch",
    )(p, p, w, cb, g, b)


def _sg_kernel(u_ref, v_ref, g_ref, b_ref, w_ref, bias_ref, o_ref):
    tm = u_ref.shape[0]

    def gelu(z):
        return 0.5 * z * (1.0 + lax.erf(z * 0.7071067811865476))

    zu = gelu(u_ref[...].astype(F32))
    vn = _layer_norm(gelu(v_ref[...].astype(F32)), g_ref[...], b_ref[...]).astype(BF16)
    lane = lax.broadcasted_iota(jnp.int32, (SG_CHUNK, LANES), 1)
    first_group = lane < (SG_WIDTH // SG_GROUPS)
    for c in range(tm // SG_CHUNK):
        rows = slice(c * SG_CHUNK, (c + 1) * SG_CHUNK)
        for pr in range(SG_WIDTH // LANES):
            cols = slice(pr * LANES, (pr + 1) * LANES)
            vp = vn[rows, cols]
            m0 = jnp.dot(w_ref[2 * pr], vp, preferred_element_type=F32)
            m1 = jnp.dot(w_ref[2 * pr + 1], vp, preferred_element_type=F32)
            mixed = jnp.where(first_group, m0, m1) + bias_ref[:, cols]
            o_ref[rows, cols] = (zu[rows, cols] * mixed).astype(BF16)


def _sg_branch(p, g, b, w_tril, bias_full):
    n = p.shape[0]
    tm = TM_BRANCH
    c0 = P_SG // SG_WIDTH
    vec = pl.BlockSpec((1, SG_WIDTH), lambda i: (0, 0))
    return pl.pallas_call(
        _sg_kernel,
        grid=(n // tm,),
        in_specs=[
            pl.BlockSpec((tm, SG_WIDTH), lambda i: (i, c0)),
            pl.BlockSpec((tm, SG_WIDTH), lambda i: (i, c0 + 1)),
            vec, vec,
            pl.BlockSpec((SG_GROUPS, SG_CHUNK, SG_CHUNK), lambda i: (0, 0, 0)),
            pl.BlockSpec((SG_CHUNK, SG_WIDTH), lambda i: (0, 0)),
        ],
        out_specs=pl.BlockSpec((tm, SG_WIDTH), lambda i: (i, 0)),
        out_shape=jax.ShapeDtypeStruct((n, SG_WIDTH), BF16),
        compiler_params=_params(32, 1),
        name="sg_branch",
    )(p, p, g, b, w_tril, bias_full)


def _attn_kernel(q_ref, k_ref, vt_ref, o_ref):
    i = pl.program_id(2)
    tq = q_ref.shape[0]
    tk = vt_ref.shape[-1]
    key = lax.broadcasted_iota(jnp.int32, (tk, tq), 0)
    qry = i * tq + lax.broadcasted_iota(jnp.int32, (tk, tq), 1)
    lane = lax.broadcasted_iota(jnp.int32, (tq, LANES), 1)
    n_full = (i * tq) // tk

    head_cols = [slice(hh * HEAD_PAD, (hh + 1) * HEAD_PAD) for hh in range(2)]
    qps = [q_ref[:, cols] for cols in head_cols]

    def scores(j):
        start = pl.multiple_of(j * tk, tk)
        return tuple(lax.dot_general(k_ref[pl.ds(start, tk), cols], qps[hh], (((1,), (1,)), ((), ())),
                                     preferred_element_type=F32)
                     for hh, cols in enumerate(head_cols))

    def consume(j, state, s_pair, masked):
        probs = []
        for hh, s in enumerate(s_pair):
            m = state[hh][0]
            if masked:
                s = jnp.where(j * tk + key <= qry, s, -jnp.inf)
            m_new = jnp.maximum(m, jnp.max(s, axis=0, keepdims=True))
            probs.append((m_new, jnp.exp2(m - m_new), jnp.exp2(s - m_new).astype(BF16)))
        out = []
        for hh, (m_new, alpha, pexp) in enumerate(probs):
            pv = jnp.dot(vt_ref[j, head_cols[hh], :], pexp, preferred_element_type=F32)
            out.append((m_new, alpha * state[hh][1] + pv))
        return tuple(out)

    def step(j, carry):
        state, s_pair = carry
        s_next = scores(j + 1)
        return consume(j, state, s_pair, False), s_next

    init = (jnp.full((1, tq), -jnp.inf, F32), jnp.zeros((HEAD_PAD, tq), F32))
    state, s_last = lax.fori_loop(0, n_full, step, ((init, init), scores(0)))
    state = consume(n_full, state, s_last, True)
    outs = []
    for hh in range(2):
        acc_t = state[hh][1].T
        ones_lane = HEAD_DIM if hh == 0 else 0
        outs.append(acc_t / acc_t[:, ones_lane:ones_lane + 1])
    o_ref[...] = jnp.where(lane < HEAD_DIM, outs[0], outs[1]).astype(BF16)


def _attention(p, vt, batch, seq):
    n = p.shape[0]
    pairs = ATT_HEADS // 2
    qt = seq // TQ
    pw = 2 * HEAD_PAD
    return pl.pallas_call(
        _attn_kernel,
        grid=(batch, pairs, qt),
        in_specs=[
            pl.BlockSpec((TQ, pw), lambda b, h, i: (b * qt + i, P_Q // pw + h)),
            pl.BlockSpec((seq, pw), lambda b, h, i: (b, P_K // pw + h)),
            pl.BlockSpec((seq // TK, pw, TK), lambda b, h, i: (b, h, 0)),
        ],
        out_specs=pl.BlockSpec((TQ, LANES), lambda b, h, i: (b * qt + i, h)),
        out_shape=jax.ShapeDtypeStruct((n, ATT_WIDTH), BF16),
        compiler_params=_params(32, 3),
        name="fox_attention",
    )(p, p, vt)


def _merge_kernel(x_ref, hc_ref, ha_ref, hs_ref, g0_ref, g1_ref, g2_ref, bg_ref,
                  wc_ref, wa_ref, ws_ref, wm_ref, o_ref):
    merged = None
    for k, (h_ref, w_ref, gl_ref) in enumerate(((hc_ref, wc_ref, g0_ref), (ha_ref, wa_ref, g1_ref),
                                                (hs_ref, ws_ref, g2_ref))):
        y = jnp.dot(h_ref[...], w_ref[...], preferred_element_type=F32)
        gate = _sigmoid(gl_ref[...].astype(F32) + bg_ref[:, k * D_MODEL:(k + 1) * D_MODEL])
        merged = gate * y if merged is None else merged + gate * y
    o_ref[...] = x_ref[...] + jnp.dot(merged.astype(BF16), wm_ref[...], preferred_element_type=F32)


def _merge(x, hc, ha, hs, p, bg, wc, wa, ws, wm):
    n = x.shape[0]
    tm = TM_BRANCH
    g0 = P_GATE // D_MODEL
    half = lambda: pl.BlockSpec((tm, CONV_CH), lambda i: (i, 0))
    wspec = lambda k: pl.BlockSpec((k, D_MODEL), lambda i: (0, 0))
    return pl.pallas_call(
        _merge_kernel,
        grid=(n // tm,),
        in_specs=[
            pl.BlockSpec((tm, D_MODEL), lambda i: (i, 0)),
            half(), half(), half(),
            pl.BlockSpec((tm, D_MODEL), lambda i: (i, g0)),
            pl.BlockSpec((tm, D_MODEL), lambda i: (i, g0 + 1)),
            pl.BlockSpec((tm, D_MODEL), lambda i: (i, g0 + 2)),
            pl.BlockSpec((1, N_BRANCH * D_MODEL), lambda i: (0, 0)),
            wspec(CONV_CH), wspec(ATT_WIDTH), wspec(SG_WIDTH), wspec(D_MODEL),
        ],
        out_specs=pl.BlockSpec((tm, D_MODEL), lambda i: (i, 0)),
        out_shape=jax.ShapeDtypeStruct((n, D_MODEL), F32),
        compiler_params=_params(48, 1),
        name="merge",
    )(x, hc, ha, hs, p, p, p, bg, wc, wa, ws, wm)


def _ffn_kernel(x_ref, g_ref, w1_ref, w3_ref, w2_ref, o_ref, hn_ref, acc_ref):
    f = pl.program_id(1)

    @pl.when(f == 0)
    def _():
        hn_ref[...] = _rms(x_ref[...], g_ref[...]).astype(BF16)
        acc_ref[...] = jnp.zeros_like(acc_ref)

    hn = hn_ref[...]
    a = jnp.dot(hn, w1_ref[...], preferred_element_type=F32)
    b = jnp.dot(hn, w3_ref[...], preferred_element_type=F32)
    t = (a * _sigmoid(a) * b).astype(BF16)
    acc_ref[...] += jnp.dot(t, w2_ref[...], preferred_element_type=F32)

    @pl.when(f == pl.num_programs(1) - 1)
    def _():
        o_ref[...] = x_ref[...] + acc_ref[...]


def _ffn(x, g, w1, w3, w2):
    n = x.shape[0]
    tm, tf = TM_FFN, TF_FFN
    return pl.pallas_call(
        _ffn_kernel,
        grid=(n // tm, D_FF // tf),
        in_specs=[
            pl.BlockSpec((tm, D_MODEL), lambda i, f: (i, 0)),
            pl.BlockSpec((1, D_MODEL), lambda i, f: (0, 0)),
            pl.BlockSpec((D_MODEL, tf), lambda i, f: (0, f)),
            pl.BlockSpec((D_MODEL, tf), lambda i, f: (0, f)),
            pl.BlockSpec((tf, D_MODEL), lambda i, f: (f, 0)),
        ],
        out_specs=pl.BlockSpec((tm, D_MODEL), lambda i, f: (i, 0)),
        out_shape=jax.ShapeDtypeStruct((n, D_MODEL), F32),
        scratch_shapes=[pltpu.VMEM((tm, D_MODEL), BF16), pltpu.VMEM((tm, D_MODEL), F32)],
        compiler_params=_params(48, 2),
        name="ffn_dense",
    )(x, g, w1, w3, w2)


def _router_kernel(x_ref, g_ref, wr_ref, lt_ref, h_ref, meta_ref, cnt_ref, carry_ref):
    i = pl.program_id(0)
    tm = x_ref.shape[0]

    @pl.when(i == 0)
    def _():
        carry_ref[...] = jnp.zeros_like(carry_ref)

    h = _rms(x_ref[...], g_ref[...])
    h_ref[...] = h
    logits = jnp.dot(h, wr_ref[...], preferred_element_type=F32, precision=lax.Precision.HIGHEST)
    lane = lax.broadcasted_iota(jnp.int32, (tm, LANES), 1)
    lanef = lane.astype(F32)
    lg = jnp.where(lane < N_EXPERTS, logits, -jnp.inf)
    v1 = jnp.max(lg, axis=1, keepdims=True)
    i1 = jnp.min(jnp.where(lg == v1, lanef, float(LANES)), axis=1, keepdims=True)
    lg2 = jnp.where(lanef == i1, -jnp.inf, lg)
    v2 = jnp.max(lg2, axis=1, keepdims=True)
    i2 = jnp.min(jnp.where(lg2 == v2, lanef, float(LANES)), axis=1, keepdims=True)
    e = jnp.exp(v2 - v1)
    g1 = 1.0 / (1.0 + e)
    g2 = e / (1.0 + e)
    oh1 = lanef == i1
    oh2 = lanef == i2
    cnt = (oh1.astype(F32) + oh2.astype(F32))
    before = jnp.dot(lt_ref[...], cnt.astype(BF16), preferred_element_type=F32) + carry_ref[...]
    r1 = jnp.sum(jnp.where(oh1, before, 0.0), axis=1, keepdims=True)
    r2 = jnp.sum(jnp.where(oh2, before, 0.0), axis=1, keepdims=True)
    total = carry_ref[...] + jnp.sum(cnt, axis=0, keepdims=True)
    carry_ref[...] = total
    cnt_ref[...] = jnp.broadcast_to(total, cnt_ref.shape)
    meta = jnp.zeros((tm, LANES), F32)
    for k, val in enumerate((i1, i2, g1, g2, r1, r2)):
        meta = jnp.where(lane == k, val, meta)
    meta_ref[...] = meta


def _router(x, g, wr, lt):
    n = x.shape[0]
    tm = TM_BRANCH
    return pl.pallas_call(
        _router_kernel,
        grid=(n // tm,),
        in_specs=[
            pl.BlockSpec((tm, D_MODEL), lambda i: (i, 0)),
            pl.BlockSpec((1, D_MODEL), lambda i: (0, 0)),
            pl.BlockSpec((D_MODEL, LANES), lambda i: (0, 0)),
            pl.BlockSpec((tm, tm), lambda i: (0, 0)),
        ],
        out_specs=[
            pl.BlockSpec((tm, D_MODEL), lambda i: (i, 0)),
            pl.BlockSpec((tm, LANES), lambda i: (i, 0)),
            pl.BlockSpec((8, LANES), lambda i: (0, 0)),
        ],
        out_shape=[
            jax.ShapeDtypeStruct((n, D_MODEL), F32),
            jax.ShapeDtypeStruct((n, LANES), F32),
            jax.ShapeDtypeStruct((8, LANES), F32),
        ],
        scratch_shapes=[pltpu.VMEM((1, LANES), F32)],
        compiler_params=_params(32, 1),
        name="router",
    )(x, g, wr, lt)


def _row_copy(src_ref, src_row, dst_ref, dst_row, sem):
    return pltpu.make_async_copy(src_ref.at[pl.ds(src_row, 1), :], dst_ref.at[pl.ds(dst_row, 1), :], sem)


def _dispatch_kernel(dest_ref, h_ref, xs_in_ref, xs_ref, sem):
    del xs_in_ref
    tm = h_ref.shape[0]

    def issue(r, carry):
        for k in range(2):
            _row_copy(h_ref, r, xs_ref, dest_ref[2 * r + k], sem).start()
        return carry

    lax.fori_loop(0, tm, issue, 0)

    def drain(r, carry):
        _row_copy(h_ref, 0, xs_ref, 0, sem).wait()
        return carry

    lax.fori_loop(0, 2 * tm, drain, 0)


def _dispatch(dest_flat, h, xs_zero):
    n = h.shape[0]
    tm = TM_ROW
    return pl.pallas_call(
        _dispatch_kernel,
        grid=(n // tm,),
        in_specs=[
            pl.BlockSpec((2 * tm,), lambda i: (i,), memory_space=pltpu.SMEM),
            pl.BlockSpec((tm, D_MODEL), lambda i: (i, 0)),
            pl.BlockSpec(memory_space=pl.ANY),
        ],
        out_specs=pl.BlockSpec(memory_space=pl.ANY),
        out_shape=jax.ShapeDtypeStruct(xs_zero.shape, xs_zero.dtype),
        scratch_shapes=[pltpu.SemaphoreType.DMA(())],
        input_output_aliases={2: 0},
        compiler_params=_params(32, 1),
        name="moe_dispatch",
    )(dest_flat, h, xs_zero)


def _moe_kernel(be_ref, nu_ref, x_ref, w1_ref, w3_ref, w2_ref, y_ref, xb_ref, acc_ref):
    del be_ref
    b = pl.program_id(0)
    f = pl.program_id(1)

    @pl.when(b < nu_ref[0])
    def _():
        @pl.when(f == 0)
        def _():
            xb_ref[...] = x_ref[...].astype(BF16)
            acc_ref[...] = jnp.zeros_like(acc_ref)

        xb = xb_ref[...]
        a = jnp.dot(xb, w1_ref[...], preferred_element_type=F32)
        c = jnp.dot(xb, w3_ref[...], preferred_element_type=F32)
        t = (a * _sigmoid(a) * c).astype(BF16)
        acc_ref[...] += jnp.dot(t, w2_ref[...], preferred_element_type=F32)

        @pl.when(f == pl.num_programs(1) - 1)
        def _():
            y_ref[...] = acc_ref[...]

    @pl.when(jnp.logical_and(b >= nu_ref[0], f == pl.num_programs(1) - 1))
    def _():
        y_ref[...] = jnp.zeros_like(y_ref)


def _moe_experts(blk_e, n_used, xs, w1, w3, w2):
    rows = xs.shape[0]
    n_blk = rows // MOE_BLK
    nf = D_FF_EXPERT // TF_MOE

    def row_idx(b, f, be, nu):
        return (jnp.minimum(b, nu[0] - 1), 0)

    def f_idx(b, f, nu):
        return jnp.where(b < nu[0], f, nf - 1)

    grid_spec = pltpu.PrefetchScalarGridSpec(
        num_scalar_prefetch=2,
        grid=(n_blk, nf),
        in_specs=[
            pl.BlockSpec((MOE_BLK, D_MODEL), row_idx),
            pl.BlockSpec((None, D_MODEL, TF_MOE), lambda b, f, be, nu: (be[b], 0, f_idx(b, f, nu))),
            pl.BlockSpec((None, D_MODEL, TF_MOE), lambda b, f, be, nu: (be[b], 0, f_idx(b, f, nu))),
            pl.BlockSpec((None, TF_MOE, D_MODEL), lambda b, f, be, nu: (be[b], f_idx(b, f, nu), 0)),
        ],
        out_specs=pl.BlockSpec((MOE_BLK, D_MODEL), lambda b, f, be, nu: (b, 0)),
        scratch_shapes=[pltpu.VMEM((MOE_BLK, D_MODEL), BF16), pltpu.VMEM((MOE_BLK, D_MODEL), F32)],
    )
    return pl.pallas_call(
        _moe_kernel,
        grid_spec=grid_spec,
        out_shape=jax.ShapeDtypeStruct((rows, D_MODEL), F32),
        compiler_params=_params(48, 2),
        name="moe_experts",
    )(blk_e, n_used, xs, w1, w3, w2)


def _combine_kernel(dest_ref, x_ref, meta_ref, gfin_ref, y_ref, o_ref, ybuf_ref, sem, *, final_norm):
    tm = x_ref.shape[0]

    def issue(r, carry):
        for k in range(2):
            _row_copy(y_ref, dest_ref[2 * r + k], ybuf_ref.at[k], r, sem).start()
        return carry

    lax.fori_loop(0, tm, issue, 0)

    def drain(r, carry):
        _row_copy(y_ref, 0, ybuf_ref.at[0], 0, sem).wait()
        return carry

    lax.fori_loop(0, 2 * tm, drain, 0)

    g1 = meta_ref[:, 2:3]
    g2 = meta_ref[:, 3:4]
    out = x_ref[...] + (g1 * ybuf_ref[0] + g2 * ybuf_ref[1])
    if final_norm:
        out = _rms(out, gfin_ref[...])
    o_ref[...] = out


def _combine(dest_flat, x, meta, gfin, y, final_norm):
    n = x.shape[0]
    tm = TM_ROW
    kern = functools.partial(_combine_kernel, final_norm=final_norm)
    return pl.pallas_call(
        kern,
        grid=(n // tm,),
        in_specs=[
            pl.BlockSpec((2 * tm,), lambda i: (i,), memory_space=pltpu.SMEM),
            pl.BlockSpec((tm, D_MODEL), lambda i: (i, 0)),
            pl.BlockSpec((tm, LANES), lambda i: (i, 0)),
            pl.BlockSpec((1, D_MODEL), lambda i: (0, 0)),
            pl.BlockSpec(memory_space=pl.ANY),
        ],
        out_specs=pl.BlockSpec((tm, D_MODEL), lambda i: (i, 0)),
        out_shape=jax.ShapeDtypeStruct((n, D_MODEL), F32),
        scratch_shapes=[pltpu.VMEM((2, tm, D_MODEL), F32), pltpu.SemaphoreType.DMA(())],
        compiler_params=_params(32, 1),
        name="moe_combine",
    )(dest_flat, x, meta, gfin, y)


def _moe_layer(x, g, wr, lt, w1, w3, w2, gfin, final_norm):
    n = x.shape[0]
    h, meta, cnt = _router(x, g, wr, lt)
    expert = meta[:, 0:2].astype(jnp.int32)
    rank = meta[:, 4:6].astype(jnp.int32)
    counts = cnt[0, :N_EXPERTS].astype(jnp.int32)
    padded = (counts + MOE_BLK - 1) // MOE_BLK * MOE_BLK
    pad_end = jnp.cumsum(padded)
    pad_start = pad_end - padded
    dest = (pad_start[expert] + rank).reshape(-1)
    n_blk = (2 * n) // MOE_BLK + N_EXPERTS
    blk_start = jnp.arange(n_blk, dtype=jnp.int32) * MOE_BLK
    blk_e = jnp.minimum(jnp.searchsorted(pad_end, blk_start, side="right"), N_EXPERTS - 1)
    n_used = (pad_end[-1:] // MOE_BLK).astype(jnp.int32)
    xs = _dispatch(dest, h, jnp.zeros((n_blk * MOE_BLK, D_MODEL), F32))
    y = _moe_experts(blk_e.astype(jnp.int32), n_used, xs, w1, w3, w2)
    return _combine(dest, x, meta, gfin, y, final_norm)


def _prepare_mixer(w_in, b_forget, conv_w, w_spatial, b_spatial, w_conv_out, w_att_out, w_sg_out,
                   w_mix_out):
    depth, d, _ = w_in.shape

    def pad_heads(w):
        w = w.reshape(depth, d, ATT_HEADS, HEAD_DIM)
        w = jnp.pad(w, ((0, 0), (0, 0), (0, 0), (0, HEAD_PAD - HEAD_DIM)))
        return w.reshape(depth, d, ATT_HEADS * HEAD_PAD)

    prep = {}
    prep["w_main"] = jnp.concatenate(
        [w_in[:, :, OFF_CONV:OFF_Q], w_in[:, :, OFF_SG:OFF_GATE], w_in[:, :, OFF_GATE:],
         pad_heads(w_in[:, :, OFF_Q:OFF_K]), pad_heads(w_in[:, :, OFF_K:OFF_V])], axis=2).astype(BF16)
    col = jnp.arange(P_COLS)
    is_q = (col >= P_Q) & (col < P_K)
    spare = (col % HEAD_PAD >= HEAD_DIM) & (col % HEAD_PAD < HEAD_DIM + N_SPLIT)
    prep["col_scale"] = jnp.where(is_q, LOG2E * HEAD_DIM ** -0.5, 1.0).astype(F32)[None, :]
    prep["col_bias"] = jnp.where(is_q & spare, 1.0, 0.0).astype(F32)[None, :]
    kcol = jnp.arange(ATT_HEADS * HEAD_PAD)
    prep["place"] = -((kcol[None, None, :] // HEAD_PAD == jnp.arange(LANES)[None, :, None])
                      & (kcol[None, None, :] % HEAD_PAD == HEAD_DIM + jnp.arange(N_SPLIT)[:, None, None])
                      ).astype(BF16)
    prep["wf"] = jnp.pad(w_in[:, :, OFF_F:OFF_SG], ((0, 0), (0, 0), (0, LANES - ATT_HEADS))).astype(BF16)
    prep["bf"] = jnp.pad(b_forget, ((0, 0), (0, LANES - ATT_HEADS)))[:, None, :]
    wv_t = jnp.swapaxes(w_in[:, :, OFF_V:OFF_F], 1, 2).reshape(depth, ATT_HEADS // 2, 2, HEAD_DIM, d)
    zeros_h = jnp.zeros_like(wv_t[:, :, 0])
    wvt = jnp.concatenate([wv_t[:, :, 0], zeros_h, zeros_h, wv_t[:, :, 1]], axis=2)
    prep["wvt"] = wvt.reshape(depth, ATT_HEADS * HEAD_PAD, d).astype(BF16)
    vrow = jnp.arange(ATT_HEADS * HEAD_PAD) % (2 * HEAD_PAD)
    prep["vbias"] = ((vrow == HEAD_DIM) | (vrow == HEAD_PAD)).astype(F32)[:, None]
    prep["lt_incl"] = jnp.tril(jnp.ones((TM_PROJ, TM_PROJ), F32)).astype(BF16)
    prep["conv_w"] = jnp.pad(conv_w, ((0, 0), (0, 1), (0, 0)))
    causal = jnp.tril(jnp.ones((SG_CHUNK, SG_CHUNK), bool))
    prep["w_sp"] = jnp.where(causal[None, None], w_spatial, 0).astype(BF16)
    prep["sg_bias"] = jnp.repeat(jnp.swapaxes(b_spatial, 1, 2), SG_WIDTH // SG_GROUPS, axis=2)
    for name, w in (("wc", w_conv_out), ("wa", w_att_out), ("ws", w_sg_out), ("wm", w_mix_out)):
        prep[name] = w.astype(BF16)
    return prep


def _token_mixer(xf, prep, vecs, layer, batch, seq):
    row = lambda name: vecs[name][layer][None, :]
    p, vt = _inproj(xf, row("mix_norm_g"), prep["w_main"][layer], prep["col_scale"], prep["col_bias"],
                    prep["wf"][layer], prep["bf"][layer], prep["lt_incl"], prep["place"],
                    prep["wvt"][layer], prep["vbias"], seq)
    hc = _conv_branch(p, prep["conv_w"][layer], row("conv_b"), row("conv_ln_g"), row("conv_ln_b"), seq)
    hs = _sg_branch(p, row("sg_ln_g"), row("sg_ln_b"), prep["w_sp"][layer], prep["sg_bias"][layer])
    ha = _attention(p, vt, batch, seq)
    return _merge(xf, hc, ha, hs, p, row("b_gate"), prep["wc"][layer], prep["wa"][layer],
                  prep["ws"][layer], prep["wm"][layer])


def kernel(x, mix_norm_g, w_in, b_forget, b_gate, conv_w, conv_b, conv_ln_g, conv_ln_b, w_conv_out,
           w_att_out, sg_ln_g, sg_ln_b, w_spatial, b_spatial, w_sg_out, w_mix_out, ffn_norm_g,
           ffn_w1, ffn_w3, ffn_w2, router_w, moe_w1, moe_w3, moe_w2, final_norm_g):
    batch, seq, d = x.shape
    depth = w_in.shape[0]
    if depth % 2 == 1:
        raise NotImplementedError("the final norm is fused into the last (expert) layer")
    xf = x.reshape(batch * seq, d)

    prep = _prepare_mixer(w_in, b_forget, conv_w, w_spatial, b_spatial, w_conv_out, w_att_out,
                          w_sg_out, w_mix_out)
    vecs = dict(mix_norm_g=mix_norm_g, conv_b=conv_b, conv_ln_g=conv_ln_g, conv_ln_b=conv_ln_b,
                sg_ln_g=sg_ln_g, sg_ln_b=sg_ln_b, b_gate=b_gate)
    f1, f3, f2 = (w.astype(BF16) for w in (ffn_w1, ffn_w3, ffn_w2))
    m1, m3, m2 = (w.astype(BF16) for w in (moe_w1, moe_w3, moe_w2))
    wr = jnp.pad(router_w, ((0, 0), (0, 0), (0, LANES - N_EXPERTS)))
    lt = jnp.tril(jnp.ones((TM_BRANCH, TM_BRANCH), F32), -1).astype(BF16)
    gfin = final_norm_g[None, :]

    for layer in range(depth):
        xf = _token_mixer(xf, prep, vecs, layer, batch, seq)
        g_ffn = ffn_norm_g[layer][None, :]
        i = layer // 2
        if layer % 2 == 0:
            xf = _ffn(xf, g_ffn, f1[i], f3[i], f2[i])
        else:
            xf = _moe_layer(xf, g_ffn, wr[i], lt, m1[i], m3[i], m2[i], gfin,
                            final_norm=(layer == depth - 1))
    return xf.reshape(batch, seq, d)
```

```python
import functools

import jax
import jax.numpy as jnp
from jax import lax
from jax.experimental import pallas as pl
from jax.experimental.pallas import tpu as pltpu

F32 = jnp.float32
BF16 = jnp.bfloat16

D_MODEL = 1024
CONV_CH = 512
CONV_K = 31
ATT_HEADS = 8
HEAD_DIM = 64
ATT_WIDTH = ATT_HEADS * HEAD_DIM
SG_GROUPS = 8
SG_WIDTH = 512
SG_CHUNK = 128
N_BRANCH = 3
OFF_CONV = 0
OFF_Q = OFF_CONV + 2 * CONV_CH
OFF_K = OFF_Q + ATT_WIDTH
OFF_V = OFF_K + ATT_WIDTH
OFF_F = OFF_V + ATT_WIDTH
OFF_SG = OFF_F + ATT_HEADS
OFF_GATE = OFF_SG + 2 * SG_WIDTH
D_FF = 2816
N_EXPERTS = 8
D_FF_EXPERT = 3584
EPS = 1e-6

LANES = 128
SUBLANES = 8
MIB = 1024 * 1024

P_CONV = 0
P_SG = P_CONV + 2 * CONV_CH
P_GATE = P_SG + 2 * SG_WIDTH
HEAD_PAD = LANES
P_Q = P_GATE + N_BRANCH * D_MODEL
P_K = P_Q + ATT_HEADS * HEAD_PAD
P_COLS = P_K + ATT_HEADS * HEAD_PAD
N_SPLIT = 3
LOG2E = 1.4426950408889634

TM_PROJ = 1024
TN_PROJ = 1024
TM_BRANCH = 512
CONV_HIST = 32
CONV_RB = 64
TQ = 512
TK = 512
TM_FFN = 1024
TF_FFN = 256
MOE_BLK = 512
TF_MOE = 512
TM_ROW = 256


def _params(vmem_mib, n_axes):
    return pltpu.CompilerParams(dimension_semantics=("arbitrary",) * n_axes,
                                vmem_limit_bytes=vmem_mib * MIB)


def _rms(x, g):
    return x * lax.rsqrt(jnp.mean(x * x, axis=-1, keepdims=True) + EPS) * g


def _layer_norm(x, g, b):
    mu = jnp.mean(x, axis=-1, keepdims=True)
    d = x - mu
    var = jnp.mean(d * d, axis=-1, keepdims=True)
    return d * lax.rsqrt(var + EPS) * g + b


def _sigmoid(x):
    return 1.0 / (1.0 + jnp.exp(-x))


def _split3(x):
    hi = x.astype(BF16)
    r1 = x - hi.astype(F32)
    mid = r1.astype(BF16)
    lo = (r1 - mid.astype(F32)).astype(BF16)
    return hi, mid, lo


def _inproj_kernel(x_ref, g_ref, w_ref, sc_ref, bi_ref, wf_ref, bf_ref, lt_ref, pl_ref, wvt_ref,
                   vb_ref, p_ref, vt_ref, xn_ref, c3_ref, carry_ref, *, tiles_per_batch, k_tile0):
    i = pl.program_id(0)
    j = pl.program_id(1)
    tm = x_ref.shape[0]
    tk = vt_ref.shape[-1]

    @pl.when(j == 0)
    def _():
        xn = _rms(x_ref[...], g_ref[...]).astype(BF16)
        xn_ref[...] = xn
        f = jnp.dot(xn, wf_ref[...], preferred_element_type=F32) + bf_ref[...]
        ls = jnp.minimum(f, 0.0) - jnp.log1p(jnp.exp(-jnp.abs(f)))
        @pl.when(i % tiles_per_batch == 0)
        def _():
            carry_ref[...] = jnp.zeros_like(carry_ref)

        lt = lt_ref[...]
        terms = _split3(ls)
        offset = carry_ref[...]
        for blk in range(tm // LANES):
            rows = slice(blk * LANES, (blk + 1) * LANES)
            c = sum(jnp.dot(lt, term[rows], preferred_element_type=F32) for term in terms) + offset
            offset = c[LANES - 1:LANES, :]
            for r, term in enumerate(_split3(c * LOG2E)):
                c3_ref[r, rows, :] = term
        carry_ref[...] = offset
        vt = lax.dot_general(wvt_ref[...], xn, (((1,), (1,)), ((), ())),
                             preferred_element_type=F32) + vb_ref[...]
        for ch in range(tm // tk):
            vt_ref[ch] = vt[:, ch * tk:(ch + 1) * tk].astype(BF16)

    acc = jnp.dot(xn_ref[...], w_ref[...], preferred_element_type=F32) * sc_ref[...] + bi_ref[...]

    @pl.when(j < k_tile0)
    def _():
        p_ref[...] = acc.astype(BF16)

    @pl.when(j >= k_tile0)
    def _():
        decay = sum(jnp.dot(c3_ref[r], pl_ref[r], preferred_element_type=F32) for r in range(N_SPLIT))
        p_ref[...] = (acc + decay).astype(BF16)


def _inproj(x, g, w, col_scale, col_bias, wf, bf, lt, place, wvt, vbias, seq):
    n = x.shape[0]
    tm, tn = TM_PROJ, TN_PROJ
    k_tile0 = P_K // tn
    kern = functools.partial(_inproj_kernel, tiles_per_batch=seq // tm, k_tile0=k_tile0)
    vt_rows = ATT_HEADS * HEAD_PAD
    const = lambda shape: pl.BlockSpec(shape, lambda i, j: (0,) * len(shape))
    return pl.pallas_call(
        kern,
        grid=(n // tm, P_COLS // tn),
        in_specs=[
            pl.BlockSpec((tm, D_MODEL), lambda i, j: (i, 0)),
            const((1, D_MODEL)),
            pl.BlockSpec((D_MODEL, tn), lambda i, j: (0, j)),
            pl.BlockSpec((1, tn), lambda i, j: (0, j)),
            pl.BlockSpec((1, tn), lambda i, j: (0, j)),
            const((D_MODEL, LANES)),
            const((1, LANES)),
            const((LANES, LANES)),
            pl.BlockSpec((N_SPLIT, LANES, tn), lambda i, j: (0, 0, jnp.maximum(j - k_tile0, 0))),
            const((vt_rows, D_MODEL)),
            const((vt_rows, 1)),
        ],
        out_specs=[
            pl.BlockSpec((tm, tn), lambda i, j: (i, j)),
            pl.BlockSpec((tm // TK, vt_rows, TK), lambda i, j: (i, 0, 0)),
        ],
        out_shape=[
            jax.ShapeDtypeStruct((n, P_COLS), BF16),
            jax.ShapeDtypeStruct((n // TK, vt_rows, TK), BF16),
        ],
        scratch_shapes=[pltpu.VMEM((tm, D_MODEL), BF16), pltpu.VMEM((N_SPLIT, tm, LANES), BF16),
                        pltpu.VMEM((1, LANES), F32)],
        compiler_params=_params(48, 2),
        name="inproj",
    )(x, g, w, col_scale, col_bias, wf, bf, lt, place, wvt, vbias)


def _conv_kernel(a1_ref, a2_ref, w_ref, cb_ref, g_ref, b_ref, o_ref, hext_ref, hsh_ref,
                 *, tiles_per_batch):
    i = pl.program_id(0)
    tm = a1_ref.shape[0]
    span = hsh_ref.shape[1]

    @pl.when(i % tiles_per_batch == 0)
    def _():
        hext_ref[0:CONV_HIST, :] = jnp.zeros((CONV_HIST, CONV_CH), F32)

    @pl.when(i % tiles_per_batch != 0)
    def _():
        hext_ref[0:CONV_HIST, :] = hext_ref[tm:tm + CONV_HIST, :]

    hext_ref[CONV_HIST:CONV_HIST + tm, :] = (
        a1_ref[...].astype(F32) * _sigmoid(a2_ref[...].astype(F32)))
    for s in range(1, SUBLANES):
        hsh_ref[s] = hext_ref[s:s + span, :]

    g = g_ref[...]
    b = b_ref[...]
    base = CONV_HIST - (CONV_K - 1)
    for r in range(0, tm, CONV_RB):
        acc = jnp.broadcast_to(cb_ref[...], (CONV_RB, CONV_CH))
        for j in range(CONV_K):
            s = (base + j) % SUBLANES
            a = r + base + j - s
            rows = hext_ref[a:a + CONV_RB, :] if s == 0 else hsh_ref[s, a:a + CONV_RB, :]
            acc = acc + w_ref[j:j + 1, :] * rows
        y = _layer_norm(acc, g, b)
        o_ref[r:r + CONV_RB, :] = (y * _sigmoid(y)).astype(BF16)


def _conv_branch(p, w, cb, g, b, seq):
    n = p.shape[0]
    tm = TM_BRANCH
    kern = functools.partial(_conv_kernel, tiles_per_batch=seq // tm)
    c0 = P_CONV // CONV_CH
    vec = pl.BlockSpec((1, CONV_CH), lambda i: (0, 0))
    return pl.pallas_call(
        kern,
        grid=(n // tm,),
        in_specs=[
            pl.BlockSpec((tm, CONV_CH), lambda i: (i, c0)),
            pl.BlockSpec((tm, CONV_CH), lambda i: (i, c0 + 1)),
            pl.BlockSpec((CONV_K + 1, CONV_CH), lambda i: (0, 0)),
            vec, vec, vec,
        ],
        out_specs=pl.BlockSpec((tm, CONV_CH), lambda i: (i, 0)),
        out_shape=jax.ShapeDtypeStruct((n, CONV_CH), BF16),
        scratch_shapes=[pltpu.VMEM((tm + CONV_HIST, CONV_CH), F32),
                        pltpu.VMEM((SUBLANES, tm + CONV_HIST - SUBLANES, CONV_CH), F32)],
        compiler_params=_params(32, 1),
        name="conv_branch",
    )(p, p, w, cb, g, b)


def _sg_kernel(u_ref, v_ref, g_ref, b_ref, w_ref, bias_ref, o_ref):
    tm = u_ref.shape[0]

    def gelu(z):
        return 0.5 * z * (1.0 + lax.erf(z * 0.7071067811865476))

    zu = gelu(u_ref[...].astype(F32))
    vn = _layer_norm(gelu(v_ref[...].astype(F32)), g_ref[...], b_ref[...]).astype(BF16)
    lane = lax.broadcasted_iota(jnp.int32, (SG_CHUNK, LANES), 1)
    first_group = lane < (SG_WIDTH // SG_GROUPS)
    for c in range(tm // SG_CHUNK):
        rows = slice(c * SG_CHUNK, (c + 1) * SG_CHUNK)
        for pr in range(SG_WIDTH // LANES):
            cols = slice(pr * LANES, (pr + 1) * LANES)
            vp = vn[rows, cols]
            m0 = jnp.dot(w_ref[2 * pr], vp, preferred_element_type=F32)
            m1 = jnp.dot(w_ref[2 * pr + 1], vp, preferred_element_type=F32)
            mixed = jnp.where(first_group, m0, m1) + bias_ref[:, cols]
            o_ref[rows, cols] = (zu[rows, cols] * mixed).astype(BF16)


def _sg_branch(p, g, b, w_tril, bias_full):
    n = p.shape[0]
    tm = TM_BRANCH
    c0 = P_SG // SG_WIDTH
    vec = pl.BlockSpec((1, SG_WIDTH), lambda i: (0, 0))
    return pl.pallas_call(
        _sg_kernel,
        grid=(n // tm,),
        in_specs=[
            pl.BlockSpec((tm, SG_WIDTH), lambda i: (i, c0)),
            pl.BlockSpec((tm, SG_WIDTH), lambda i: (i, c0 + 1)),
            vec, vec,
            pl.BlockSpec((SG_GROUPS, SG_CHUNK, SG_CHUNK), lambda i: (0, 0, 0)),
            pl.BlockSpec((SG_CHUNK, SG_WIDTH), lambda i: (0, 0)),
        ],
        out_specs=pl.BlockSpec((tm, SG_WIDTH), lambda i: (i, 0)),
        out_shape=jax.ShapeDtypeStruct((n, SG_WIDTH), BF16),
        compiler_params=_params(32, 1),
        name="sg_branch",
    )(p, p, g, b, w_tril, bias_full)


def _attn_kernel(q_ref, k_ref, vt_ref, o_ref, sa_ref, sb_ref, m_ref, acc_ref):
    i = pl.program_id(2)
    tq = q_ref.shape[0]
    tk = vt_ref.shape[-1]
    n_full = (i * tq) // tk
    head_cols = [slice(hh * HEAD_PAD, (hh + 1) * HEAD_PAD) for hh in range(2)]

    def scores(j, dst_ref):
        start = pl.multiple_of(j * tk, tk)
        for hh, cols in enumerate(head_cols):
            dst_ref[hh] = lax.dot_general(k_ref[pl.ds(start, tk), cols], q_ref[:, cols],
                                          (((1,), (1,)), ((), ())), preferred_element_type=F32)

    def consume(j, src_ref, masked):
        probs = []
        for hh in range(2):
            s = src_ref[hh]
            if masked:
                key = j * tk + lax.broadcasted_iota(jnp.int32, (tk, tq), 0)
                qry = i * tq + lax.broadcasted_iota(jnp.int32, (tk, tq), 1)
                s = jnp.where(key <= qry, s, -jnp.inf)
            m = m_ref[hh]
            m_new = jnp.maximum(m, jnp.max(s, axis=0, keepdims=True))
            m_ref[hh] = m_new
            probs.append((jnp.exp2(m - m_new), jnp.exp2(s - m_new).astype(BF16)))
        for hh, (alpha, pexp) in enumerate(probs):
            pv = jnp.dot(vt_ref[j, head_cols[hh], :], pexp, preferred_element_type=F32)
            acc_ref[hh] = alpha * acc_ref[hh] + pv

    m_ref[...] = jnp.full(m_ref.shape, -jnp.inf, F32)
    acc_ref[...] = jnp.zeros(acc_ref.shape, F32)
    scores(0, sa_ref)

    def two_tiles(u, carry):
        j = 2 * u
        scores(j + 1, sb_ref)
        consume(j, sa_ref, False)
        scores(j + 2, sa_ref)
        consume(j + 1, sb_ref, False)
        return carry

    lax.fori_loop(0, n_full // 2, two_tiles, 0)

    @pl.when(n_full % 2 == 1)
    def _():
        scores(n_full, sb_ref)
        consume(n_full - 1, sa_ref, False)
        consume(n_full, sb_ref, True)

    @pl.when(n_full % 2 == 0)
    def _():
        consume(n_full, sa_ref, True)

    lane = lax.broadcasted_iota(jnp.int32, (tq, LANES), 1)
    outs = []
    for hh in range(2):
        acc_t = acc_ref[hh].T
        ones_lane = HEAD_DIM if hh == 0 else 0
        outs.append(acc_t / acc_t[:, ones_lane:ones_lane + 1])
    o_ref[...] = jnp.where(lane < HEAD_DIM, outs[0], outs[1]).astype(BF16)


def _attention(p, vt, batch, seq):
    n = p.shape[0]
    pairs = ATT_HEADS // 2
    qt = seq // TQ
    pw = 2 * HEAD_PAD
    return pl.pallas_call(
        _attn_kernel,
        grid=(batch, pairs, qt),
        in_specs=[
            pl.BlockSpec((TQ, pw), lambda b, h, i: (b * qt + i, P_Q // pw + h)),
            pl.BlockSpec((seq, pw), lambda b, h, i: (b, P_K // pw + h)),
            pl.BlockSpec((seq // TK, pw, TK), lambda b, h, i: (b, h, 0)),
        ],
        out_specs=pl.BlockSpec((TQ, LANES), lambda b, h, i: (b * qt + i, h)),
        out_shape=jax.ShapeDtypeStruct((n, ATT_WIDTH), BF16),
        scratch_shapes=[pltpu.VMEM((2, TK, TQ), F32), pltpu.VMEM((2, TK, TQ), F32),
                        pltpu.VMEM((2, 1, TQ), F32), pltpu.VMEM((2, HEAD_PAD, TQ), F32)],
        compiler_params=_params(32, 3),
        name="fox_attention",
    )(p, p, vt)


def _merge_kernel(x_ref, hc_ref, ha_ref, hs_ref, g0_ref, g1_ref, g2_ref, bg_ref,
                  wc_ref, wa_ref, ws_ref, wm_ref, o_ref):
    merged = None
    for k, (h_ref, w_ref, gl_ref) in enumerate(((hc_ref, wc_ref, g0_ref), (ha_ref, wa_ref, g1_ref),
                                                (hs_ref, ws_ref, g2_ref))):
        y = jnp.dot(h_ref[...], w_ref[...], preferred_element_type=F32)
        gate = _sigmoid(gl_ref[...].astype(F32) + bg_ref[:, k * D_MODEL:(k + 1) * D_MODEL])
        merged = gate * y if merged is None else merged + gate * y
    o_ref[...] = x_ref[...] + jnp.dot(merged.astype(BF16), wm_ref[...], preferred_element_type=F32)


def _merge(x, hc, ha, hs, p, bg, wc, wa, ws, wm):
    n = x.shape[0]
    tm = TM_BRANCH
    g0 = P_GATE // D_MODEL
    half = lambda: pl.BlockSpec((tm, CONV_CH), lambda i: (i, 0))
    wspec = lambda k: pl.BlockSpec((k, D_MODEL), lambda i: (0, 0))
    return pl.pallas_call(
        _merge_kernel,
        grid=(n // tm,),
        in_specs=[
            pl.BlockSpec((tm, D_MODEL), lambda i: (i, 0)),
            half(), half(), half(),
            pl.BlockSpec((tm, D_MODEL), lambda i: (i, g0)),
            pl.BlockSpec((tm, D_MODEL), lambda i: (i, g0 + 1)),
            pl.BlockSpec((tm, D_MODEL), lambda i: (i, g0 + 2)),
            pl.BlockSpec((1, N_BRANCH * D_MODEL), lambda i: (0, 0)),
            wspec(CONV_CH), wspec(ATT_WIDTH), wspec(SG_WIDTH), wspec(D_MODEL),
        ],
        out_specs=pl.BlockSpec((tm, D_MODEL), lambda i: (i, 0)),
        out_shape=jax.ShapeDtypeStruct((n, D_MODEL), F32),
        compiler_params=_params(48, 1),
        name="merge",
    )(x, hc, ha, hs, p, p, p, bg, wc, wa, ws, wm)


def _ffn_kernel(x_ref, g_ref, w1_ref, w3_ref, w2_ref, o_ref, hn_ref, acc_ref):
    f = pl.program_id(1)

    @pl.when(f == 0)
    def _():
        hn_ref[...] = _rms(x_ref[...], g_ref[...]).astype(BF16)
        acc_ref[...] = jnp.zeros_like(acc_ref)

    hn = hn_ref[...]
    a = jnp.dot(hn, w1_ref[...], preferred_element_type=F32)
    b = jnp.dot(hn, w3_ref[...], preferred_element_type=F32)
    t = (a * _sigmoid(a) * b).astype(BF16)
    acc_ref[...] += jnp.dot(t, w2_ref[...], preferred_element_type=F32)

    @pl.when(f == pl.num_programs(1) - 1)
    def _():
        o_ref[...] = x_ref[...] + acc_ref[...]


def _ffn(x, g, w1, w3, w2):
    n = x.shape[0]
    tm, tf = TM_FFN, TF_FFN
    return pl.pallas_call(
        _ffn_kernel,
        grid=(n // tm, D_FF // tf),
        in_specs=[
            pl.BlockSpec((tm, D_MODEL), lambda i, f: (i, 0)),
            pl.BlockSpec((1, D_MODEL), lambda i, f: (0, 0)),
            pl.BlockSpec((D_MODEL, tf), lambda i, f: (0, f)),
            pl.BlockSpec((D_MODEL, tf), lambda i, f: (0, f)),
            pl.BlockSpec((tf, D_MODEL), lambda i, f: (f, 0)),
        ],
        out_specs=pl.BlockSpec((tm, D_MODEL), lambda i, f: (i, 0)),
        out_shape=jax.ShapeDtypeStruct((n, D_MODEL), F32),
        scratch_shapes=[pltpu.VMEM((tm, D_MODEL), BF16), pltpu.VMEM((tm, D_MODEL), F32)],
        compiler_params=_params(48, 2),
        name="ffn_dense",
    )(x, g, w1, w3, w2)


def _router_kernel(x_ref, g_ref, wr_ref, lt_ref, h_ref, meta_ref, cnt_ref, carry_ref):
    i = pl.program_id(0)
    tm = x_ref.shape[0]

    @pl.when(i == 0)
    def _():
        carry_ref[...] = jnp.zeros_like(carry_ref)

    h = _rms(x_ref[...], g_ref[...])
    h_ref[...] = h
    logits = jnp.dot(h, wr_ref[...], preferred_element_type=F32, precision=lax.Precision.HIGHEST)
    lane = lax.broadcasted_iota(jnp.int32, (tm, LANES), 1)
    lanef = lane.astype(F32)
    lg = jnp.where(lane < N_EXPERTS, logits, -jnp.inf)
    v1 = jnp.max(lg, axis=1, keepdims=True)
    i1 = jnp.min(jnp.where(lg == v1, lanef, float(LANES)), axis=1, keepdims=True)
    lg2 = jnp.where(lanef == i1, -jnp.inf, lg)
    v2 = jnp.max(lg2, axis=1, keepdims=True)
    i2 = jnp.min(jnp.where(lg2 == v2, lanef, float(LANES)), axis=1, keepdims=True)
    e = jnp.exp(v2 - v1)
    g1 = 1.0 / (1.0 + e)
    g2 = e / (1.0 + e)
    oh1 = lanef == i1
    oh2 = lanef == i2
    cnt = (oh1.astype(F32) + oh2.astype(F32))
    before = jnp.dot(lt_ref[...], cnt.astype(BF16), preferred_element_type=F32) + carry_ref[...]
    r1 = jnp.sum(jnp.where(oh1, before, 0.0), axis=1, keepdims=True)
    r2 = jnp.sum(jnp.where(oh2, before, 0.0), axis=1, keepdims=True)
    total = carry_ref[...] + jnp.sum(cnt, axis=0, keepdims=True)
    carry_ref[...] = total
    cnt_ref[...] = jnp.broadcast_to(total, cnt_ref.shape)
    meta = jnp.zeros((tm, LANES), F32)
    for k, val in enumerate((i1, i2, g1, g2, r1, r2)):
        meta = jnp.where(lane == k, val, meta)
    meta_ref[...] = meta


def _router(x, g, wr, lt):
    n = x.shape[0]
    tm = TM_BRANCH
    return pl.pallas_call(
        _router_kernel,
        grid=(n // tm,),
        in_specs=[
            pl.BlockSpec((tm, D_MODEL), lambda i: (i, 0)),
            pl.BlockSpec((1, D_MODEL), lambda i: (0, 0)),
            pl.BlockSpec((D_MODEL, LANES), lambda i: (0, 0)),
            pl.BlockSpec((tm, tm), lambda i: (0, 0)),
        ],
        out_specs=[
            pl.BlockSpec((tm, D_MODEL), lambda i: (i, 0)),
            pl.BlockSpec((tm, LANES), lambda i: (i, 0)),
            pl.BlockSpec((8, LANES), lambda i: (0, 0)),
        ],
        out_shape=[
            jax.ShapeDtypeStruct((n, D_MODEL), F32),
            jax.ShapeDtypeStruct((n, LANES), F32),
            jax.ShapeDtypeStruct((8, LANES), F32),
        ],
        scratch_shapes=[pltpu.VMEM((1, LANES), F32)],
        compiler_params=_params(32, 1),
        name="router",
    )(x, g, wr, lt)


def _row_copy(src_ref, src_row, dst_ref, dst_row, sem):
    return pltpu.make_async_copy(src_ref.at[pl.ds(src_row, 1), :], dst_ref.at[pl.ds(dst_row, 1), :], sem)


def _dispatch_kernel(dest_ref, h_ref, xs_in_ref, xs_ref, sem):
    del xs_in_ref
    tm = h_ref.shape[0]

    def issue(r, carry):
        for k in range(2):
            _row_copy(h_ref, r, xs_ref, dest_ref[2 * r + k], sem).start()
        return carry

    lax.fori_loop(0, tm, issue, 0, unroll=8)
    for _ in range(2):
        pltpu.make_async_copy(h_ref, xs_ref.at[pl.ds(0, tm), :], sem).wait()


def _dispatch(dest_flat, h, xs_zero):
    n = h.shape[0]
    tm = TM_ROW
    return pl.pallas_call(
        _dispatch_kernel,
        grid=(n // tm,),
        in_specs=[
            pl.BlockSpec((2 * tm,), lambda i: (i,), memory_space=pltpu.SMEM),
            pl.BlockSpec((tm, D_MODEL), lambda i: (i, 0)),
            pl.BlockSpec(memory_space=pl.ANY),
        ],
        out_specs=pl.BlockSpec(memory_space=pl.ANY),
        out_shape=jax.ShapeDtypeStruct(xs_zero.shape, xs_zero.dtype),
        scratch_shapes=[pltpu.SemaphoreType.DMA(())],
        input_output_aliases={2: 0},
        compiler_params=_params(32, 1),
        name="moe_dispatch",
    )(dest_flat, h, xs_zero)


def _moe_kernel(be_ref, nu_ref, x_ref, w1_ref, w3_ref, w2_ref, y_ref, xb_ref, acc_ref):
    del be_ref
    b = pl.program_id(0)
    f = pl.program_id(1)

    @pl.when(b < nu_ref[0])
    def _():
        @pl.when(f == 0)
        def _():
            xb_ref[...] = x_ref[...].astype(BF16)
            acc_ref[...] = jnp.zeros_like(acc_ref)

        xb = xb_ref[...]
        a = jnp.dot(xb, w1_ref[...], preferred_element_type=F32)
        c = jnp.dot(xb, w3_ref[...], preferred_element_type=F32)
        t = (a * _sigmoid(a) * c).astype(BF16)
        acc_ref[...] += jnp.dot(t, w2_ref[...], preferred_element_type=F32)

        @pl.when(f == pl.num_programs(1) - 1)
        def _():
            y_ref[...] = acc_ref[...]

    @pl.when(jnp.logical_and(b >= nu_ref[0], f == pl.num_programs(1) - 1))
    def _():
        y_ref[...] = jnp.zeros_like(y_ref)


def _moe_experts(blk_e, n_used, xs, w1, w3, w2):
    rows = xs.shape[0]
    n_blk = rows // MOE_BLK
    nf = D_FF_EXPERT // TF_MOE

    def row_idx(b, f, be, nu):
        return (jnp.minimum(b, nu[0] - 1), 0)

    def f_idx(b, f, nu):
        return jnp.where(b < nu[0], f, nf - 1)

    grid_spec = pltpu.PrefetchScalarGridSpec(
        num_scalar_prefetch=2,
        grid=(n_blk, nf),
        in_specs=[
            pl.BlockSpec((MOE_BLK, D_MODEL), row_idx),
            pl.BlockSpec((None, D_MODEL, TF_MOE), lambda b, f, be, nu: (be[b], 0, f_idx(b, f, nu))),
            pl.BlockSpec((None, D_MODEL, TF_MOE), lambda b, f, be, nu: (be[b], 0, f_idx(b, f, nu))),
            pl.BlockSpec((None, TF_MOE, D_MODEL), lambda b, f, be, nu: (be[b], f_idx(b, f, nu), 0)),
        ],
        out_specs=pl.BlockSpec((MOE_BLK, D_MODEL), lambda b, f, be, nu: (b, 0)),
        scratch_shapes=[pltpu.VMEM((MOE_BLK, D_MODEL), BF16), pltpu.VMEM((MOE_BLK, D_MODEL), F32)],
    )
    return pl.pallas_call(
        _moe_kernel,
        grid_spec=grid_spec,
        out_shape=jax.ShapeDtypeStruct((rows, D_MODEL), F32),
        compiler_params=_params(48, 2),
        name="moe_experts",
    )(blk_e, n_used, xs, w1, w3, w2)


def _combine_kernel(dest_ref, x_ref, meta_ref, gfin_ref, y_ref, o_ref, ybuf_ref, sem, *, final_norm):
    tm = x_ref.shape[0]

    def issue(r, carry):
        for k in range(2):
            _row_copy(y_ref, dest_ref[2 * r + k], ybuf_ref.at[k], r, sem).start()
        return carry

    lax.fori_loop(0, tm, issue, 0, unroll=8)
    for k in range(2):
        pltpu.make_async_copy(y_ref.at[pl.ds(0, tm), :], ybuf_ref.at[k], sem).wait()

    g1 = meta_ref[:, 2:3]
    g2 = meta_ref[:, 3:4]
    out = x_ref[...] + (g1 * ybuf_ref[0] + g2 * ybuf_ref[1])
    if final_norm:
        out = _rms(out, gfin_ref[...])
    o_ref[...] = out


def _combine(dest_flat, x, meta, gfin, y, final_norm):
    n = x.shape[0]
    tm = TM_ROW
    kern = functools.partial(_combine_kernel, final_norm=final_norm)
    return pl.pallas_call(
        kern,
        grid=(n // tm,),
        in_specs=[
            pl.BlockSpec((2 * tm,), lambda i: (i,), memory_space=pltpu.SMEM),
            pl.BlockSpec((tm, D_MODEL), lambda i: (i, 0)),
            pl.BlockSpec((tm, LANES), lambda i: (i, 0)),
            pl.BlockSpec((1, D_MODEL), lambda i: (0, 0)),
            pl.BlockSpec(memory_space=pl.ANY),
        ],
        out_specs=pl.BlockSpec((tm, D_MODEL), lambda i: (i, 0)),
        out_shape=jax.ShapeDtypeStruct((n, D_MODEL), F32),
        scratch_shapes=[pltpu.VMEM((2, tm, D_MODEL), F32), pltpu.SemaphoreType.DMA(())],
        compiler_params=_params(32, 1),
        name="moe_combine",
    )(dest_flat, x, meta, gfin, y)


def _moe_layer(x, g, wr, lt, w1, w3, w2, gfin, final_norm):
    n = x.shape[0]
    h, meta, cnt = _router(x, g, wr, lt)
    expert = meta[:, 0:2].astype(jnp.int32)
    rank = meta[:, 4:6].astype(jnp.int32)
    counts = cnt[0, :N_EXPERTS].astype(jnp.int32)
    padded = (counts + MOE_BLK - 1) // MOE_BLK * MOE_BLK
    pad_end = jnp.cumsum(padded)
    pad_start = pad_end - padded
    dest = (pad_start[expert] + rank).reshape(-1)
    n_blk = (2 * n) // MOE_BLK + N_EXPERTS
    blk_start = jnp.arange(n_blk, dtype=jnp.int32) * MOE_BLK
    blk_e = jnp.minimum(jnp.searchsorted(pad_end, blk_start, side="right"), N_EXPERTS - 1)
    n_used = (pad_end[-1:] // MOE_BLK).astype(jnp.int32)
    xs = _dispatch(dest, h, jnp.zeros((n_blk * MOE_BLK, D_MODEL), F32))
    y = _moe_experts(blk_e.astype(jnp.int32), n_used, xs, w1, w3, w2)
    return _combine(dest, x, meta, gfin, y, final_norm)


def _prepare_mixer(w_in, b_forget, conv_w, w_spatial, b_spatial, w_conv_out, w_att_out, w_sg_out,
                   w_mix_out):
    depth, d, _ = w_in.shape

    def pad_heads(w):
        w = w.reshape(depth, d, ATT_HEADS, HEAD_DIM)
        w = jnp.pad(w, ((0, 0), (0, 0), (0, 0), (0, HEAD_PAD - HEAD_DIM)))
        return w.reshape(depth, d, ATT_HEADS * HEAD_PAD)

    prep = {}
    prep["w_main"] = jnp.concatenate(
        [w_in[:, :, OFF_CONV:OFF_Q], w_in[:, :, OFF_SG:OFF_GATE], w_in[:, :, OFF_GATE:],
         pad_heads(w_in[:, :, OFF_Q:OFF_K]), pad_heads(w_in[:, :, OFF_K:OFF_V])], axis=2).astype(BF16)
    col = jnp.arange(P_COLS)
    is_q = (col >= P_Q) & (col < P_K)
    spare = (col % HEAD_PAD >= HEAD_DIM) & (col % HEAD_PAD < HEAD_DIM + N_SPLIT)
    prep["col_scale"] = jnp.where(is_q, LOG2E * HEAD_DIM ** -0.5, 1.0).astype(F32)[None, :]
    prep["col_bias"] = jnp.where(is_q & spare, 1.0, 0.0).astype(F32)[None, :]
    kcol = jnp.arange(ATT_HEADS * HEAD_PAD)
    prep["place"] = -((kcol[None, None, :] // HEAD_PAD == jnp.arange(LANES)[None, :, None])
                      & (kcol[None, None, :] % HEAD_PAD == HEAD_DIM + jnp.arange(N_SPLIT)[:, None, None])
                      ).astype(BF16)
    prep["wf"] = jnp.pad(w_in[:, :, OFF_F:OFF_SG], ((0, 0), (0, 0), (0, LANES - ATT_HEADS))).astype(BF16)
    prep["bf"] = jnp.pad(b_forget, ((0, 0), (0, LANES - ATT_HEADS)))[:, None, :]
    wv_t = jnp.swapaxes(w_in[:, :, OFF_V:OFF_F], 1, 2).reshape(depth, ATT_HEADS // 2, 2, HEAD_DIM, d)
    zeros_h = jnp.zeros_like(wv_t[:, :, 0])
    wvt = jnp.concatenate([wv_t[:, :, 0], zeros_h, zeros_h, wv_t[:, :, 1]], axis=2)
    prep["wvt"] = wvt.reshape(depth, ATT_HEADS * HEAD_PAD, d).astype(BF16)
    vrow = jnp.arange(ATT_HEADS * HEAD_PAD) % (2 * HEAD_PAD)
    prep["vbias"] = ((vrow == HEAD_DIM) | (vrow == HEAD_PAD)).astype(F32)[:, None]
    prep["lt_incl"] = jnp.tril(jnp.ones((LANES, LANES), F32)).astype(BF16)
    prep["conv_w"] = jnp.pad(conv_w, ((0, 0), (0, 1), (0, 0)))
    causal = jnp.tril(jnp.ones((SG_CHUNK, SG_CHUNK), bool))
    prep["w_sp"] = jnp.where(causal[None, None], w_spatial, 0).astype(BF16)
    prep["sg_bias"] = jnp.repeat(jnp.swapaxes(b_spatial, 1, 2), SG_WIDTH // SG_GROUPS, axis=2)
    for name, w in (("wc", w_conv_out), ("wa", w_att_out), ("ws", w_sg_out), ("wm", w_mix_out)):
        prep[name] = w.astype(BF16)
    return prep


def _token_mixer(xf, prep, vecs, layer, batch, seq):
    row = lambda name: vecs[name][layer][None, :]
    p, vt = _inproj(xf, row("mix_norm_g"), prep["w_main"][layer], prep["col_scale"], prep["col_bias"],
                    prep["wf"][layer], prep["bf"][layer], prep["lt_incl"], prep["place"],
                    prep["wvt"][layer], prep["vbias"], seq)
    hc = _conv_branch(p, prep["conv_w"][layer], row("conv_b"), row("conv_ln_g"), row("conv_ln_b"), seq)
    hs = _sg_branch(p, row("sg_ln_g"), row("sg_ln_b"), prep["w_sp"][layer], prep["sg_bias"][layer])
    ha = _attention(p, vt, batch, seq)
    return _merge(xf, hc, ha, hs, p, row("b_gate"), prep["wc"][layer], prep["wa"][layer],
                  prep["ws"][layer], prep["wm"][layer])


def kernel(x, mix_norm_g, w_in, b_forget, b_gate, conv_w, conv_b, conv_ln_g, conv_ln_b, w_conv_out,
           w_att_out, sg_ln_g, sg_ln_b, w_spatial, b_spatial, w_sg_out, w_mix_out, ffn_norm_g,
           ffn_w1, ffn_w3, ffn_w2, router_w, moe_w1, moe_w3, moe_w2, final_norm_g):
    batch, seq, d = x.shape
    depth = w_in.shape[0]
    if depth % 2 == 1:
        raise NotImplementedError("the final norm is fused into the last (expert) layer")
    xf = x.reshape(batch * seq, d)

    prep = _prepare_mixer(w_in, b_forget, conv_w, w_spatial, b_spatial, w_conv_out, w_att_out,
                          w_sg_out, w_mix_out)
    vecs = dict(mix_norm_g=mix_norm_g, conv_b=conv_b, conv_ln_g=conv_ln_g, conv_ln_b=conv_ln_b,
                sg_ln_g=sg_ln_g, sg_ln_b=sg_ln_b, b_gate=b_gate)
    f1, f3, f2 = (w.astype(BF16) for w in (ffn_w1, ffn_w3, ffn_w2))
    m1, m3, m2 = (w.astype(BF16) for w in (moe_w1, moe_w3, moe_w2))
    wr = jnp.pad(router_w, ((0, 0), (0, 0), (0, LANES - N_EXPERTS)))
    lt = jnp.tril(jnp.ones((TM_BRANCH, TM_BRANCH), F32), -1).astype(BF16)
    gfin = final_norm_g[None, :]

    for layer in range(depth):
        xf = _token_mixer(xf, prep, vecs, layer, batch, seq)
        g_ffn = ffn_norm_g[layer][None, :]
        i = layer // 2
        if layer % 2 == 0:
            xf = _ffn(xf, g_ffn, f1[i], f3[i], f2[i])
        else:
            xf = _moe_layer(xf, g_ffn, wr[i], lt, m1[i], m3[i], m2[i], gfin,
                            final_norm=(layer == depth - 1))
    return xf.reshape(batch, seq, d)
```

```python
import functools

import jax
import jax.numpy as jnp
from jax import lax
from jax.experimental import pallas as pl
from jax.experimental.pallas import tpu as pltpu

F32 = jnp.float32
BF16 = jnp.bfloat16

D_MODEL = 1024
CONV_CH = 512
CONV_K = 31
ATT_HEADS = 8
HEAD_DIM = 64
ATT_WIDTH = ATT_HEADS * HEAD_DIM
SG_GROUPS = 8
SG_WIDTH = 512
SG_CHUNK = 128
N_BRANCH = 3
OFF_CONV = 0
OFF_Q = OFF_CONV + 2 * CONV_CH
OFF_K = OFF_Q + ATT_WIDTH
OFF_V = OFF_K + ATT_WIDTH
OFF_F = OFF_V + ATT_WIDTH
OFF_SG = OFF_F + ATT_HEADS
OFF_GATE = OFF_SG + 2 * SG_WIDTH
D_FF = 2816
N_EXPERTS = 8
D_FF_EXPERT = 3584
EPS = 1e-6

LANES = 128
SUBLANES = 8
MIB = 1024 * 1024

HEAD_PAD = LANES
P_K = 0
P_CONV = P_K + ATT_HEADS * HEAD_PAD
P_SG = P_CONV + 2 * CONV_CH
P_GATE = P_SG + 2 * SG_WIDTH
P_Q = P_GATE + N_BRANCH * D_MODEL
P_COLS = P_Q + ATT_HEADS * HEAD_PAD
N_SPLIT = 3
LOG2E = 1.4426950408889634

TM_PROJ = 1024
TN_PROJ = 1024
TM_BRANCH = 512
CONV_HIST = 32
CONV_RB = 64
TQ = 512
TK = 512
TM_FFN = 1024
TF_FFN = 256
MOE_BLK = 512
TF_MOE = 512
TM_ROW = 256


def _params(vmem_mib, n_axes):
    return pltpu.CompilerParams(dimension_semantics=("arbitrary",) * n_axes,
                                vmem_limit_bytes=vmem_mib * MIB)


def _rms(x, g):
    return x * lax.rsqrt(jnp.mean(x * x, axis=-1, keepdims=True) + EPS) * g


def _layer_norm(x, g, b):
    mu = jnp.mean(x, axis=-1, keepdims=True)
    d = x - mu
    var = jnp.mean(d * d, axis=-1, keepdims=True)
    return d * lax.rsqrt(var + EPS) * g + b


def _sigmoid(x):
    return 1.0 / (1.0 + jnp.exp(-x))


def _split3(x):
    hi = x.astype(BF16)
    r1 = x - hi.astype(F32)
    mid = r1.astype(BF16)
    lo = (r1 - mid.astype(F32)).astype(BF16)
    return hi, mid, lo


def _inproj_kernel(x_ref, g_ref, w_ref, sc_ref, bi_ref, wf_ref, bf_ref, lt_ref, pl_ref, wvt_ref,
                   vb_ref, p_ref, vt_ref, xn_ref, c3_ref, carry_ref, *, tiles_per_batch):
    i = pl.program_id(0)
    j = pl.program_id(1)
    tm = x_ref.shape[0]
    tk = vt_ref.shape[-1]

    @pl.when(j == 0)
    def _():
        xn = _rms(x_ref[...], g_ref[...]).astype(BF16)
        xn_ref[...] = xn
        f = jnp.dot(xn, wf_ref[...], preferred_element_type=F32) + bf_ref[...]
        ls = jnp.minimum(f, 0.0) - jnp.log1p(jnp.exp(-jnp.abs(f)))
        @pl.when(i % tiles_per_batch == 0)
        def _():
            carry_ref[...] = jnp.zeros_like(carry_ref)

        lt = lt_ref[...]
        terms = _split3(ls)
        offset = carry_ref[...]
        for blk in range(tm // LANES):
            rows = slice(blk * LANES, (blk + 1) * LANES)
            c = sum(jnp.dot(lt, term[rows], preferred_element_type=F32) for term in terms) + offset
            offset = c[LANES - 1:LANES, :]
            for r, term in enumerate(_split3(c * LOG2E)):
                c3_ref[r, rows, :] = term
        carry_ref[...] = offset
        vt = lax.dot_general(wvt_ref[...], xn, (((1,), (1,)), ((), ())),
                             preferred_element_type=F32) + vb_ref[...]
        for ch in range(tm // tk):
            vt_ref[ch] = vt[:, ch * tk:(ch + 1) * tk].astype(BF16)
        decay = sum(jnp.dot(c3_ref[r], pl_ref[r], preferred_element_type=F32) for r in range(N_SPLIT))
        p_ref[...] = (jnp.dot(xn, w_ref[...], preferred_element_type=F32) + decay).astype(BF16)

    @pl.when(j > 0)
    def _():
        acc = jnp.dot(xn_ref[...], w_ref[...], preferred_element_type=F32)
        p_ref[...] = (acc * sc_ref[...] + bi_ref[...]).astype(BF16)


def _inproj(x, g, w, col_scale, col_bias, wf, bf, lt, place, wvt, vbias, seq):
    n = x.shape[0]
    tm, tn = TM_PROJ, TN_PROJ
    assert P_K == 0 and tn == ATT_HEADS * HEAD_PAD
    kern = functools.partial(_inproj_kernel, tiles_per_batch=seq // tm)
    vt_rows = ATT_HEADS * HEAD_PAD
    const = lambda shape: pl.BlockSpec(shape, lambda i, j: (0,) * len(shape))
    return pl.pallas_call(
        kern,
        grid=(n // tm, P_COLS // tn),
        in_specs=[
            pl.BlockSpec((tm, D_MODEL), lambda i, j: (i, 0)),
            const((1, D_MODEL)),
            pl.BlockSpec((D_MODEL, tn), lambda i, j: (0, j)),
            pl.BlockSpec((1, tn), lambda i, j: (0, j)),
            pl.BlockSpec((1, tn), lambda i, j: (0, j)),
            const((D_MODEL, LANES)),
            const((1, LANES)),
            const((LANES, LANES)),
            const((N_SPLIT, LANES, tn)),
            const((vt_rows, D_MODEL)),
            const((vt_rows, 1)),
        ],
        out_specs=[
            pl.BlockSpec((tm, tn), lambda i, j: (i, j)),
            pl.BlockSpec((tm // TK, vt_rows, TK), lambda i, j: (i, 0, 0)),
        ],
        out_shape=[
            jax.ShapeDtypeStruct((n, P_COLS), BF16),
            jax.ShapeDtypeStruct((n // TK, vt_rows, TK), BF16),
        ],
        scratch_shapes=[pltpu.VMEM((tm, D_MODEL), BF16), pltpu.VMEM((N_SPLIT, tm, LANES), BF16),
                        pltpu.VMEM((1, LANES), F32)],
        compiler_params=_params(48, 2),
        name="inproj",
    )(x, g, w, col_scale, col_bias, wf, bf, lt, place, wvt, vbias)


def _conv_kernel(a1_ref, a2_ref, w_ref, cb_ref, g_ref, b_ref, o_ref, hext_ref, hsh_ref,
                 *, tiles_per_batch):
    i = pl.program_id(0)
    tm = a1_ref.shape[0]
    span = hsh_ref.shape[1]

    @pl.when(i % tiles_per_batch == 0)
    def _():
        hext_ref[0:CONV_HIST, :] = jnp.zeros((CONV_HIST, CONV_CH), F32)

    @pl.when(i % tiles_per_batch != 0)
    def _():
        hext_ref[0:CONV_HIST, :] = hext_ref[tm:tm + CONV_HIST, :]

    hext_ref[CONV_HIST:CONV_HIST + tm, :] = (
        a1_ref[...].astype(F32) * _sigmoid(a2_ref[...].astype(F32)))
    for s in range(1, SUBLANES):
        hsh_ref[s] = hext_ref[s:s + span, :]

    g = g_ref[...]
    b = b_ref[...]
    base = CONV_HIST - (CONV_K - 1)
    for r in range(0, tm, CONV_RB):
        acc = jnp.broadcast_to(cb_ref[...], (CONV_RB, CONV_CH))
        for j in range(CONV_K):
            s = (base + j) % SUBLANES
            a = r + base + j - s
            rows = hext_ref[a:a + CONV_RB, :] if s == 0 else hsh_ref[s, a:a + CONV_RB, :]
            acc = acc + w_ref[j:j + 1, :] * rows
        y = _layer_norm(acc, g, b)
        o_ref[r:r + CONV_RB, :] = (y * _sigmoid(y)).astype(BF16)


def _conv_branch(p, w, cb, g, b, seq):
    n = p.shape[0]
    tm = TM_BRANCH
    kern = functools.partial(_conv_kernel, tiles_per_batch=seq // tm)
    c0 = P_CONV // CONV_CH
    vec = pl.BlockSpec((1, CONV_CH), lambda i: (0, 0))
    return pl.pallas_call(
        kern,
        grid=(n // tm,),
        in_specs=[
            pl.BlockSpec((tm, CONV_CH), lambda i: (i, c0)),
            pl.BlockSpec((tm, CONV_CH), lambda i: (i, c0 + 1)),
            pl.BlockSpec((CONV_K + 1, CONV_CH), lambda i: (0, 0)),
            vec, vec, vec,
        ],
        out_specs=pl.BlockSpec((tm, CONV_CH), lambda i: (i, 0)),
        out_shape=jax.ShapeDtypeStruct((n, CONV_CH), BF16),
        scratch_shapes=[pltpu.VMEM((tm + CONV_HIST, CONV_CH), F32),
                        pltpu.VMEM((SUBLANES, tm + CONV_HIST - SUBLANES, CONV_CH), F32)],
        compiler_params=_params(32, 1),
        name="conv_branch",
    )(p, p, w, cb, g, b)


def _sg_kernel(u_ref, v_ref, g_ref, b_ref, w_ref, bias_ref, o_ref):
    tm = u_ref.shape[0]

    def gelu(z):
        return 0.5 * z * (1.0 + lax.erf(z * 0.7071067811865476))

    zu = gelu(u_ref[...].astype(F32))
    vn = _layer_norm(gelu(v_ref[...].astype(F32)), g_ref[...], b_ref[...]).astype(BF16)
    lane = lax.broadcasted_iota(jnp.int32, (SG_CHUNK, LANES), 1)
    first_group = lane < (SG_WIDTH // SG_GROUPS)
    for c in range(tm // SG_CHUNK):
        rows = slice(c * SG_CHUNK, (c + 1) * SG_CHUNK)
        for pr in range(SG_WIDTH // LANES):
            cols = slice(pr * LANES, (pr + 1) * LANES)
            vp = vn[rows, cols]
            m0 = jnp.dot(w_ref[2 * pr], vp, preferred_element_type=F32)
            m1 = jnp.dot(w_ref[2 * pr + 1], vp, preferred_element_type=F32)
            mixed = jnp.where(first_group, m0, m1) + bias_ref[:, cols]
            o_ref[rows, cols] = (zu[rows, cols] * mixed).astype(BF16)


def _sg_branch(p, g, b, w_tril, bias_full):
    n = p.shape[0]
    tm = TM_BRANCH
    c0 = P_SG // SG_WIDTH
    vec = pl.BlockSpec((1, SG_WIDTH), lambda i: (0, 0))
    return pl.pallas_call(
        _sg_kernel,
        grid=(n // tm,),
        in_specs=[
            pl.BlockSpec((tm, SG_WIDTH), lambda i: (i, c0)),
            pl.BlockSpec((tm, SG_WIDTH), lambda i: (i, c0 + 1)),
            vec, vec,
            pl.BlockSpec((SG_GROUPS, SG_CHUNK, SG_CHUNK), lambda i: (0, 0, 0)),
            pl.BlockSpec((SG_CHUNK, SG_WIDTH), lambda i: (0, 0)),
        ],
        out_specs=pl.BlockSpec((tm, SG_WIDTH), lambda i: (i, 0)),
        out_shape=jax.ShapeDtypeStruct((n, SG_WIDTH), BF16),
        compiler_params=_params(32, 1),
        name="sg_branch",
    )(p, p, g, b, w_tril, bias_full)


def _attn_kernel(q_ref, k_ref, vt_ref, o_ref, sa_ref, sb_ref, m_ref, acc_ref):
    i = pl.program_id(2)
    tq = q_ref.shape[0]
    tk = vt_ref.shape[-1]
    n_full = (i * tq) // tk
    head_cols = [slice(hh * HEAD_PAD, (hh + 1) * HEAD_PAD) for hh in range(2)]

    def scores(j, dst_ref):
        start = pl.multiple_of(j * tk, tk)
        for hh, cols in enumerate(head_cols):
            dst_ref[hh] = lax.dot_general(k_ref[pl.ds(start, tk), cols], q_ref[:, cols],
                                          (((1,), (1,)), ((), ())), preferred_element_type=F32)

    def consume(j, src_ref, masked):
        probs = []
        for hh in range(2):
            s = src_ref[hh]
            if masked:
                key = j * tk + lax.broadcasted_iota(jnp.int32, (tk, tq), 0)
                qry = i * tq + lax.broadcasted_iota(jnp.int32, (tk, tq), 1)
                s = jnp.where(key <= qry, s, -jnp.inf)
            m = m_ref[hh]
            m_new = jnp.maximum(m, jnp.max(s, axis=0, keepdims=True))
            m_ref[hh] = m_new
            probs.append((jnp.exp2(m - m_new), jnp.exp2(s - m_new).astype(BF16)))
        for hh, (alpha, pexp) in enumerate(probs):
            pv = jnp.dot(vt_ref[j, head_cols[hh], :], pexp, preferred_element_type=F32)
            acc_ref[hh] = alpha * acc_ref[hh] + pv

    m_ref[...] = jnp.full(m_ref.shape, -jnp.inf, F32)
    acc_ref[...] = jnp.zeros(acc_ref.shape, F32)
    scores(0, sa_ref)

    def two_tiles(u, carry):
        j = 2 * u
        scores(j + 1, sb_ref)
        consume(j, sa_ref, False)
        scores(j + 2, sa_ref)
        consume(j + 1, sb_ref, False)
        return carry

    lax.fori_loop(0, n_full // 2, two_tiles, 0)

    @pl.when(n_full % 2 == 1)
    def _():
        scores(n_full, sb_ref)
        consume(n_full - 1, sa_ref, False)
        consume(n_full, sb_ref, True)

    @pl.when(n_full % 2 == 0)
    def _():
        consume(n_full, sa_ref, True)

    lane = lax.broadcasted_iota(jnp.int32, (tq, LANES), 1)
    outs = []
    for hh in range(2):
        acc_t = acc_ref[hh].T
        ones_lane = HEAD_DIM if hh == 0 else 0
        outs.append(acc_t / acc_t[:, ones_lane:ones_lane + 1])
    o_ref[...] = jnp.where(lane < HEAD_DIM, outs[0], outs[1]).astype(BF16)


def _attention(p, vt, batch, seq):
    n = p.shape[0]
    pairs = ATT_HEADS // 2
    qt = seq // TQ
    pw = 2 * HEAD_PAD
    return pl.pallas_call(
        _attn_kernel,
        grid=(batch, pairs, qt),
        in_specs=[
            pl.BlockSpec((TQ, pw), lambda b, h, i: (b * qt + i, P_Q // pw + h)),
            pl.BlockSpec((seq, pw), lambda b, h, i: (b, P_K // pw + h)),
            pl.BlockSpec((seq // TK, pw, TK), lambda b, h, i: (b, h, 0)),
        ],
        out_specs=pl.BlockSpec((TQ, LANES), lambda b, h, i: (b * qt + i, h)),
        out_shape=jax.ShapeDtypeStruct((n, ATT_WIDTH), BF16),
        scratch_shapes=[pltpu.VMEM((2, TK, TQ), F32), pltpu.VMEM((2, TK, TQ), F32),
                        pltpu.VMEM((2, 1, TQ), F32), pltpu.VMEM((2, HEAD_PAD, TQ), F32)],
        compiler_params=_params(32, 3),
        name="fox_attention",
    )(p, p, vt)


def _merge_kernel(x_ref, hc_ref, ha_ref, hs_ref, g0_ref, g1_ref, g2_ref, bg_ref,
                  wc_ref, wa_ref, ws_ref, wm_ref, o_ref):
    merged = None
    for k, (h_ref, w_ref, gl_ref) in enumerate(((hc_ref, wc_ref, g0_ref), (ha_ref, wa_ref, g1_ref),
                                                (hs_ref, ws_ref, g2_ref))):
        y = jnp.dot(h_ref[...], w_ref[...], preferred_element_type=F32)
        gate = _sigmoid(gl_ref[...].astype(F32) + bg_ref[:, k * D_MODEL:(k + 1) * D_MODEL])
        merged = gate * y if merged is None else merged + gate * y
    o_ref[...] = x_ref[...] + jnp.dot(merged.astype(BF16), wm_ref[...], preferred_element_type=F32)


def _merge(x, hc, ha, hs, p, bg, wc, wa, ws, wm):
    n = x.shape[0]
    tm = TM_BRANCH
    g0 = P_GATE // D_MODEL
    half = lambda: pl.BlockSpec((tm, CONV_CH), lambda i: (i, 0))
    wspec = lambda k: pl.BlockSpec((k, D_MODEL), lambda i: (0, 0))
    return pl.pallas_call(
        _merge_kernel,
        grid=(n // tm,),
        in_specs=[
            pl.BlockSpec((tm, D_MODEL), lambda i: (i, 0)),
            half(), half(), half(),
            pl.BlockSpec((tm, D_MODEL), lambda i: (i, g0)),
            pl.BlockSpec((tm, D_MODEL), lambda i: (i, g0 + 1)),
            pl.BlockSpec((tm, D_MODEL), lambda i: (i, g0 + 2)),
            pl.BlockSpec((1, N_BRANCH * D_MODEL), lambda i: (0, 0)),
            wspec(CONV_CH), wspec(ATT_WIDTH), wspec(SG_WIDTH), wspec(D_MODEL),
        ],
        out_specs=pl.BlockSpec((tm, D_MODEL), lambda i: (i, 0)),
        out_shape=jax.ShapeDtypeStruct((n, D_MODEL), F32),
        compiler_params=_params(48, 1),
        name="merge",
    )(x, hc, ha, hs, p, p, p, bg, wc, wa, ws, wm)


def _ffn_kernel(x_ref, g_ref, w1_ref, w3_ref, w2_ref, o_ref, hn_ref, acc_ref):
    f = pl.program_id(1)

    @pl.when(f == 0)
    def _():
        hn_ref[...] = _rms(x_ref[...], g_ref[...]).astype(BF16)
        acc_ref[...] = jnp.zeros_like(acc_ref)

    hn = hn_ref[...]
    a = jnp.dot(hn, w1_ref[...], preferred_element_type=F32)
    b = jnp.dot(hn, w3_ref[...], preferred_element_type=F32)
    t = (a * _sigmoid(a) * b).astype(BF16)
    acc_ref[...] += jnp.dot(t, w2_ref[...], preferred_element_type=F32)

    @pl.when(f == pl.num_programs(1) - 1)
    def _():
        o_ref[...] = x_ref[...] + acc_ref[...]


def _ffn(x, g, w1, w3, w2):
    n = x.shape[0]
    tm, tf = TM_FFN, TF_FFN
    return pl.pallas_call(
        _ffn_kernel,
        grid=(n // tm, D_FF // tf),
        in_specs=[
            pl.BlockSpec((tm, D_MODEL), lambda i, f: (i, 0)),
            pl.BlockSpec((1, D_MODEL), lambda i, f: (0, 0)),
            pl.BlockSpec((D_MODEL, tf), lambda i, f: (0, f)),
            pl.BlockSpec((D_MODEL, tf), lambda i, f: (0, f)),
            pl.BlockSpec((tf, D_MODEL), lambda i, f: (f, 0)),
        ],
        out_specs=pl.BlockSpec((tm, D_MODEL), lambda i, f: (i, 0)),
        out_shape=jax.ShapeDtypeStruct((n, D_MODEL), F32),
        scratch_shapes=[pltpu.VMEM((tm, D_MODEL), BF16), pltpu.VMEM((tm, D_MODEL), F32)],
        compiler_params=_params(48, 2),
        name="ffn_dense",
    )(x, g, w1, w3, w2)


def _router_kernel(x_ref, g_ref, wr_ref, lt_ref, h_ref, meta_ref, cnt_ref, carry_ref):
    i = pl.program_id(0)
    tm = x_ref.shape[0]

    @pl.when(i == 0)
    def _():
        carry_ref[...] = jnp.zeros_like(carry_ref)

    h = _rms(x_ref[...], g_ref[...])
    h_ref[...] = h
    logits = jnp.dot(h, wr_ref[...], preferred_element_type=F32, precision=lax.Precision.HIGHEST)
    lane = lax.broadcasted_iota(jnp.int32, (tm, LANES), 1)
    lanef = lane.astype(F32)
    lg = jnp.where(lane < N_EXPERTS, logits, -jnp.inf)
    v1 = jnp.max(lg, axis=1, keepdims=True)
    i1 = jnp.min(jnp.where(lg == v1, lanef, float(LANES)), axis=1, keepdims=True)
    lg2 = jnp.where(lanef == i1, -jnp.inf, lg)
    v2 = jnp.max(lg2, axis=1, keepdims=True)
    i2 = jnp.min(jnp.where(lg2 == v2, lanef, float(LANES)), axis=1, keepdims=True)
    e = jnp.exp(v2 - v1)
    g1 = 1.0 / (1.0 + e)
    g2 = e / (1.0 + e)
    oh1 = lanef == i1
    oh2 = lanef == i2
    cnt = (oh1.astype(F32) + oh2.astype(F32))
    before = jnp.dot(lt_ref[...], cnt.astype(BF16), preferred_element_type=F32) + carry_ref[...]
    r1 = jnp.sum(jnp.where(oh1, before, 0.0), axis=1, keepdims=True)
    r2 = jnp.sum(jnp.where(oh2, before, 0.0), axis=1, keepdims=True)
    total = carry_ref[...] + jnp.sum(cnt, axis=0, keepdims=True)
    carry_ref[...] = total
    cnt_ref[...] = jnp.broadcast_to(total, cnt_ref.shape)
    meta = jnp.zeros((tm, LANES), F32)
    for k, val in enumerate((i1, i2, g1, g2, r1, r2)):
        meta = jnp.where(lane == k, val, meta)
    meta_ref[...] = meta


def _router(x, g, wr, lt):
    n = x.shape[0]
    tm = TM_BRANCH
    return pl.pallas_call(
        _router_kernel,
        grid=(n // tm,),
        in_specs=[
            pl.BlockSpec((tm, D_MODEL), lambda i: (i, 0)),
            pl.BlockSpec((1, D_MODEL), lambda i: (0, 0)),
            pl.BlockSpec((D_MODEL, LANES), lambda i: (0, 0)),
            pl.BlockSpec((tm, tm), lambda i: (0, 0)),
        ],
        out_specs=[
            pl.BlockSpec((tm, D_MODEL), lambda i: (i, 0)),
            pl.BlockSpec((tm, LANES), lambda i: (i, 0)),
            pl.BlockSpec((8, LANES), lambda i: (0, 0)),
        ],
        out_shape=[
            jax.ShapeDtypeStruct((n, D_MODEL), F32),
            jax.ShapeDtypeStruct((n, LANES), F32),
            jax.ShapeDtypeStruct((8, LANES), F32),
        ],
        scratch_shapes=[pltpu.VMEM((1, LANES), F32)],
        compiler_params=_params(32, 1),
        name="router",
    )(x, g, wr, lt)


def _row_copy(src_ref, src_row, dst_ref, dst_row, sem):
    return pltpu.make_async_copy(src_ref.at[pl.ds(src_row, 1), :], dst_ref.at[pl.ds(dst_row, 1), :], sem)


def _dispatch_kernel(bounds_ref, dest_ref, h_ref, xs_ref, zero_ref, sem, zero_sem):
    tm = h_ref.shape[0]
    n_blk = xs_ref.shape[0] // MOE_BLK

    @pl.when(pl.program_id(0) == 0)
    def _():
        zero_ref[...] = jnp.zeros_like(zero_ref)

        def zero_block(blk):
            start = pl.multiple_of(blk * MOE_BLK, MOE_BLK)
            copy = pltpu.make_async_copy(zero_ref, xs_ref.at[pl.ds(start, MOE_BLK), :], zero_sem)
            copy.start()
            copy.wait()

        for e in range(N_EXPERTS):
            @pl.when(bounds_ref[e + 1] > bounds_ref[e])
            def _():
                zero_block(bounds_ref[e + 1] // MOE_BLK - 1)

        def unused(blk, carry):
            zero_block(blk)
            return carry

        lax.fori_loop(bounds_ref[N_EXPERTS] // MOE_BLK, n_blk, unused, 0)

    def issue(r, carry):
        for k in range(2):
            _row_copy(h_ref, r, xs_ref, dest_ref[2 * r + k], sem).start()
        return carry

    lax.fori_loop(0, tm, issue, 0, unroll=8)
    for _ in range(2):
        pltpu.make_async_copy(h_ref, xs_ref.at[pl.ds(0, tm), :], sem).wait()


def _dispatch(bounds, dest_flat, h, n_rows):
    n = h.shape[0]
    tm = TM_ROW
    return pl.pallas_call(
        _dispatch_kernel,
        grid=(n // tm,),
        in_specs=[
            pl.BlockSpec(memory_space=pltpu.SMEM),
            pl.BlockSpec((2 * tm,), lambda i: (i,), memory_space=pltpu.SMEM),
            pl.BlockSpec((tm, D_MODEL), lambda i: (i, 0)),
        ],
        out_specs=pl.BlockSpec(memory_space=pl.ANY),
        out_shape=jax.ShapeDtypeStruct((n_rows, D_MODEL), F32),
        scratch_shapes=[pltpu.VMEM((MOE_BLK, D_MODEL), F32), pltpu.SemaphoreType.DMA(()),
                        pltpu.SemaphoreType.DMA(())],
        compiler_params=_params(32, 1),
        name="moe_dispatch",
    )(bounds, dest_flat, h)


def _first_block_of_expert(b, be_ref):
    return jnp.logical_or(b == 0, be_ref[b] != be_ref[jnp.maximum(b - 1, 0)])


def _moe_kernel(be_ref, nu_ref, x_ref, w1_ref, w3_ref, w2_ref, y_ref, xb_ref, acc_ref,
                wb1_ref, wb3_ref, wb2_ref):
    b = pl.program_id(0)
    f = pl.program_id(1)

    @pl.when(b < nu_ref[0])
    def _():
        @pl.when(_first_block_of_expert(b, be_ref))
        def _():
            wb1_ref[f] = w1_ref[...].astype(BF16)
            wb3_ref[f] = w3_ref[...].astype(BF16)
            wb2_ref[f] = w2_ref[...].astype(BF16)

        @pl.when(f == 0)
        def _():
            xb_ref[...] = x_ref[...].astype(BF16)
            acc_ref[...] = jnp.zeros_like(acc_ref)

        xb = xb_ref[...]
        a = jnp.dot(xb, wb1_ref[f], preferred_element_type=F32)
        c = jnp.dot(xb, wb3_ref[f], preferred_element_type=F32)
        t = (a * _sigmoid(a) * c).astype(BF16)
        acc_ref[...] += jnp.dot(t, wb2_ref[f], preferred_element_type=F32)

        @pl.when(f == pl.num_programs(1) - 1)
        def _():
            y_ref[...] = acc_ref[...]

    @pl.when(jnp.logical_and(b >= nu_ref[0], f == pl.num_programs(1) - 1))
    def _():
        y_ref[...] = jnp.zeros_like(y_ref)


def _moe_experts(blk_e, n_used, xs, w1, w3, w2):
    rows = xs.shape[0]
    n_blk = rows // MOE_BLK
    nf = D_FF_EXPERT // TF_MOE

    def row_idx(b, f, be, nu):
        return (jnp.minimum(b, nu[0] - 1), 0)

    def f_idx(b, f, be, nu):
        fetch = jnp.logical_and(b < nu[0], _first_block_of_expert(b, be))
        return jnp.where(fetch, f, nf - 1)

    grid_spec = pltpu.PrefetchScalarGridSpec(
        num_scalar_prefetch=2,
        grid=(n_blk, nf),
        in_specs=[
            pl.BlockSpec((MOE_BLK, D_MODEL), row_idx),
            pl.BlockSpec((None, D_MODEL, TF_MOE), lambda b, f, be, nu: (be[b], 0, f_idx(b, f, be, nu))),
            pl.BlockSpec((None, D_MODEL, TF_MOE), lambda b, f, be, nu: (be[b], 0, f_idx(b, f, be, nu))),
            pl.BlockSpec((None, TF_MOE, D_MODEL), lambda b, f, be, nu: (be[b], f_idx(b, f, be, nu), 0)),
        ],
        out_specs=pl.BlockSpec((MOE_BLK, D_MODEL), lambda b, f, be, nu: (b, 0)),
        scratch_shapes=[pltpu.VMEM((MOE_BLK, D_MODEL), BF16), pltpu.VMEM((MOE_BLK, D_MODEL), F32),
                        pltpu.VMEM((nf, D_MODEL, TF_MOE), BF16), pltpu.VMEM((nf, D_MODEL, TF_MOE), BF16),
                        pltpu.VMEM((nf, TF_MOE, D_MODEL), BF16)],
    )
    return pl.pallas_call(
        _moe_kernel,
        grid_spec=grid_spec,
        out_shape=jax.ShapeDtypeStruct((rows, D_MODEL), F32),
        compiler_params=_params(56, 2),
        name="moe_experts",
    )(blk_e, n_used, xs, w1, w3, w2)


def _combine_kernel(dest_ref, x_ref, meta_ref, gfin_ref, y_ref, o_ref, ybuf_ref, sem, *, final_norm):
    tm = x_ref.shape[0]

    def issue(r, carry):
        for k in range(2):
            _row_copy(y_ref, dest_ref[2 * r + k], ybuf_ref.at[k], r, sem).start()
        return carry

    lax.fori_loop(0, tm, issue, 0, unroll=8)
    for k in range(2):
        pltpu.make_async_copy(y_ref.at[pl.ds(0, tm), :], ybuf_ref.at[k], sem).wait()

    g1 = meta_ref[:, 2:3]
    g2 = meta_ref[:, 3:4]
    out = x_ref[...] + (g1 * ybuf_ref[0] + g2 * ybuf_ref[1])
    if final_norm:
        out = _rms(out, gfin_ref[...])
    o_ref[...] = out


def _combine(dest_flat, x, meta, gfin, y, final_norm):
    n = x.shape[0]
    tm = TM_ROW
    kern = functools.partial(_combine_kernel, final_norm=final_norm)
    return pl.pallas_call(
        kern,
        grid=(n // tm,),
        in_specs=[
            pl.BlockSpec((2 * tm,), lambda i: (i,), memory_space=pltpu.SMEM),
            pl.BlockSpec((tm, D_MODEL), lambda i: (i, 0)),
            pl.BlockSpec((tm, LANES), lambda i: (i, 0)),
            pl.BlockSpec((1, D_MODEL), lambda i: (0, 0)),
            pl.BlockSpec(memory_space=pl.ANY),
        ],
        out_specs=pl.BlockSpec((tm, D_MODEL), lambda i: (i, 0)),
        out_shape=jax.ShapeDtypeStruct((n, D_MODEL), F32),
        scratch_shapes=[pltpu.VMEM((2, tm, D_MODEL), F32), pltpu.SemaphoreType.DMA(())],
        compiler_params=_params(32, 1),
        name="moe_combine",
    )(dest_flat, x, meta, gfin, y)


def _moe_layer(x, g, wr, lt, w1, w3, w2, gfin, final_norm):
    n = x.shape[0]
    h, meta, cnt = _router(x, g, wr, lt)
    expert = meta[:, 0:2].astype(jnp.int32)
    rank = meta[:, 4:6].astype(jnp.int32)
    counts = cnt[0, :N_EXPERTS].astype(jnp.int32)
    padded = (counts + MOE_BLK - 1) // MOE_BLK * MOE_BLK
    pad_end = jnp.cumsum(padded)
    pad_start = pad_end - padded
    dest = (pad_start[expert] + rank).reshape(-1)
    n_blk = (2 * n) // MOE_BLK + N_EXPERTS
    blk_start = jnp.arange(n_blk, dtype=jnp.int32) * MOE_BLK
    blk_e = jnp.minimum(jnp.sum(blk_start[:, None] >= pad_end[None, :], axis=1), N_EXPERTS - 1)
    n_used = (pad_end[-1:] // MOE_BLK).astype(jnp.int32)
    bounds = jnp.concatenate([jnp.zeros((1,), jnp.int32), pad_end.astype(jnp.int32)])
    xs = _dispatch(bounds, dest, h, n_blk * MOE_BLK)
    y = _moe_experts(blk_e.astype(jnp.int32), n_used, xs, w1, w3, w2)
    return _combine(dest, x, meta, gfin, y, final_norm)


def _prepare_mixer(w_in, b_forget, conv_w, w_spatial, b_spatial, w_conv_out, w_att_out, w_sg_out,
                   w_mix_out):
    depth, d, _ = w_in.shape

    def pad_heads(w):
        w = w.reshape(depth, d, ATT_HEADS, HEAD_DIM)
        w = jnp.pad(w, ((0, 0), (0, 0), (0, 0), (0, HEAD_PAD - HEAD_DIM)))
        return w.reshape(depth, d, ATT_HEADS * HEAD_PAD)

    prep = {}
    prep["w_main"] = jnp.concatenate(
        [pad_heads(w_in[:, :, OFF_K:OFF_V]), w_in[:, :, OFF_CONV:OFF_Q], w_in[:, :, OFF_SG:OFF_GATE],
         w_in[:, :, OFF_GATE:], pad_heads(w_in[:, :, OFF_Q:OFF_K])], axis=2).astype(BF16)
    col = jnp.arange(P_COLS)
    is_q = col >= P_Q
    spare = (col % HEAD_PAD >= HEAD_DIM) & (col % HEAD_PAD < HEAD_DIM + N_SPLIT)
    prep["col_scale"] = jnp.where(is_q, LOG2E * HEAD_DIM ** -0.5, 1.0).astype(F32)[None, :]
    prep["col_bias"] = jnp.where(is_q & spare, 1.0, 0.0).astype(F32)[None, :]
    kcol = jnp.arange(ATT_HEADS * HEAD_PAD)
    prep["place"] = -((kcol[None, None, :] // HEAD_PAD == jnp.arange(LANES)[None, :, None])
                      & (kcol[None, None, :] % HEAD_PAD == HEAD_DIM + jnp.arange(N_SPLIT)[:, None, None])
                      ).astype(BF16)
    prep["wf"] = jnp.pad(w_in[:, :, OFF_F:OFF_SG], ((0, 0), (0, 0), (0, LANES - ATT_HEADS))).astype(BF16)
    prep["bf"] = jnp.pad(b_forget, ((0, 0), (0, LANES - ATT_HEADS)))[:, None, :]
    wv_t = jnp.swapaxes(w_in[:, :, OFF_V:OFF_F], 1, 2).reshape(depth, ATT_HEADS // 2, 2, HEAD_DIM, d)
    zeros_h = jnp.zeros_like(wv_t[:, :, 0])
    wvt = jnp.concatenate([wv_t[:, :, 0], zeros_h, zeros_h, wv_t[:, :, 1]], axis=2)
    prep["wvt"] = wvt.reshape(depth, ATT_HEADS * HEAD_PAD, d).astype(BF16)
    vrow = jnp.arange(ATT_HEADS * HEAD_PAD) % (2 * HEAD_PAD)
    prep["vbias"] = ((vrow == HEAD_DIM) | (vrow == HEAD_PAD)).astype(F32)[:, None]
    prep["lt_incl"] = jnp.tril(jnp.ones((LANES, LANES), F32)).astype(BF16)
    prep["conv_w"] = jnp.pad(conv_w, ((0, 0), (0, 1), (0, 0)))
    causal = jnp.tril(jnp.ones((SG_CHUNK, SG_CHUNK), bool))
    prep["w_sp"] = jnp.where(causal[None, None], w_spatial, 0).astype(BF16)
    prep["sg_bias"] = jnp.repeat(jnp.swapaxes(b_spatial, 1, 2), SG_WIDTH // SG_GROUPS, axis=2)
    for name, w in (("wc", w_conv_out), ("wa", w_att_out), ("ws", w_sg_out), ("wm", w_mix_out)):
        prep[name] = w.astype(BF16)
    return prep


def _token_mixer(xf, prep, vecs, layer, batch, seq):
    row = lambda name: vecs[name][layer][None, :]
    p, vt = _inproj(xf, row("mix_norm_g"), prep["w_main"][layer], prep["col_scale"], prep["col_bias"],
                    prep["wf"][layer], prep["bf"][layer], prep["lt_incl"], prep["place"],
                    prep["wvt"][layer], prep["vbias"], seq)
    hc = _conv_branch(p, prep["conv_w"][layer], row("conv_b"), row("conv_ln_g"), row("conv_ln_b"), seq)
    hs = _sg_branch(p, row("sg_ln_g"), row("sg_ln_b"), prep["w_sp"][layer], prep["sg_bias"][layer])
    ha = _attention(p, vt, batch, seq)
    return _merge(xf, hc, ha, hs, p, row("b_gate"), prep["wc"][layer], prep["wa"][layer],
                  prep["ws"][layer], prep["wm"][layer])


def kernel(x, mix_norm_g, w_in, b_forget, b_gate, conv_w, conv_b, conv_ln_g, conv_ln_b, w_conv_out,
           w_att_out, sg_ln_g, sg_ln_b, w_spatial, b_spatial, w_sg_out, w_mix_out, ffn_norm_g,
           ffn_w1, ffn_w3, ffn_w2, router_w, moe_w1, moe_w3, moe_w2, final_norm_g):
    batch, seq, d = x.shape
    depth = w_in.shape[0]
    if depth % 2 == 1:
        raise NotImplementedError("the final norm is fused into the last (expert) layer")
    xf = x.reshape(batch * seq, d)

    prep = _prepare_mixer(w_in, b_forget, conv_w, w_spatial, b_spatial, w_conv_out, w_att_out,
                          w_sg_out, w_mix_out)
    vecs = dict(mix_norm_g=mix_norm_g, conv_b=conv_b, conv_ln_g=conv_ln_g, conv_ln_b=conv_ln_b,
                sg_ln_g=sg_ln_g, sg_ln_b=sg_ln_b, b_gate=b_gate)
    f1, f3, f2 = (w.astype(BF16) for w in (ffn_w1, ffn_w3, ffn_w2))
    m1, m3, m2 = moe_w1, moe_w3, moe_w2
    wr = jnp.pad(router_w, ((0, 0), (0, 0), (0, LANES - N_EXPERTS)))
    lt = jnp.tril(jnp.ones((TM_BRANCH, TM_BRANCH), F32), -1).astype(BF16)
    gfin = final_norm_g[None, :]

    for layer in range(depth):
        xf = _token_mixer(xf, prep, vecs, layer, batch, seq)
        g_ffn = ffn_norm_g[layer][None, :]
        i = layer // 2
        if layer % 2 == 0:
            xf = _ffn(xf, g_ffn, f1[i], f3[i], f2[i])
        else:
            xf = _moe_layer(xf, g_ffn, wr[i], lt, m1[i], m3[i], m2[i], gfin,
                            final_norm=(layer == depth - 1))
    return xf.reshape(batch, seq, d)
```

```python
import functools

import jax
import jax.numpy as jnp
from jax import lax
from jax.experimental import pallas as pl
from jax.experimental.pallas import tpu as pltpu

F32 = jnp.float32
BF16 = jnp.bfloat16

D_MODEL = 1024
CONV_CH = 512
CONV_K = 31
ATT_HEADS = 8
HEAD_DIM = 64
ATT_WIDTH = ATT_HEADS * HEAD_DIM
SG_GROUPS = 8
SG_WIDTH = 512
SG_CHUNK = 128
N_BRANCH = 3
OFF_CONV = 0
OFF_Q = OFF_CONV + 2 * CONV_CH
OFF_K = OFF_Q + ATT_WIDTH
OFF_V = OFF_K + ATT_WIDTH
OFF_F = OFF_V + ATT_WIDTH
OFF_SG = OFF_F + ATT_HEADS
OFF_GATE = OFF_SG + 2 * SG_WIDTH
D_FF = 2816
N_EXPERTS = 8
D_FF_EXPERT = 3584
EPS = 1e-6

LANES = 128
SUBLANES = 8
MIB = 1024 * 1024

HEAD_PAD = LANES
P_K = 0
P_CONV = P_K + ATT_HEADS * HEAD_PAD
P_SG = P_CONV + 2 * CONV_CH
P_GATE = P_SG + 2 * SG_WIDTH
P_Q = P_GATE + N_BRANCH * D_MODEL
P_COLS = P_Q + ATT_HEADS * HEAD_PAD
N_SPLIT = 3
LOG2E = 1.4426950408889634

TM_PROJ = 1024
TN_PROJ = 1024
TM_BRANCH = 512
CONV_HIST = 32
CONV_RB = 64
TQ = 512
TK = 512
TM_FFN = 1024
TF_FFN = 256
MOE_BLK = 512
TF_MOE = 512
TM_ROW = 256


def _params(vmem_mib, n_axes):
    return pltpu.CompilerParams(dimension_semantics=("arbitrary",) * n_axes,
                                vmem_limit_bytes=vmem_mib * MIB)


def _rms(x, g):
    return x * lax.rsqrt(jnp.mean(x * x, axis=-1, keepdims=True) + EPS) * g


def _layer_norm(x, g, b):
    mu = jnp.mean(x, axis=-1, keepdims=True)
    d = x - mu
    var = jnp.mean(d * d, axis=-1, keepdims=True)
    return d * lax.rsqrt(var + EPS) * g + b


def _sigmoid(x):
    return 1.0 / (1.0 + jnp.exp(-x))


def _split3(x):
    hi = x.astype(BF16)
    r1 = x - hi.astype(F32)
    mid = r1.astype(BF16)
    lo = (r1 - mid.astype(F32)).astype(BF16)
    return hi, mid, lo


def _inproj_kernel(x_ref, g_ref, w_ref, sc_ref, bi_ref, wf_ref, bf_ref, lt_ref, pl_ref, wvt_ref,
                   vb_ref, p_ref, vt_ref, xn_ref, c3_ref, carry_ref, *, tiles_per_batch):
    i = pl.program_id(0)
    j = pl.program_id(1)
    tm = x_ref.shape[0]
    tk = vt_ref.shape[-1]

    @pl.when(j == 0)
    def _():
        xn = _rms(x_ref[...], g_ref[...]).astype(BF16)
        xn_ref[...] = xn
        f = jnp.dot(xn, wf_ref[...], preferred_element_type=F32) + bf_ref[...]
        ls = jnp.minimum(f, 0.0) - jnp.log1p(jnp.exp(-jnp.abs(f)))
        @pl.when(i % tiles_per_batch == 0)
        def _():
            carry_ref[...] = jnp.zeros_like(carry_ref)

        lt = lt_ref[...]
        terms = _split3(ls)
        offset = carry_ref[...]
        for blk in range(tm // LANES):
            rows = slice(blk * LANES, (blk + 1) * LANES)
            c = sum(jnp.dot(lt, term[rows], preferred_element_type=F32) for term in terms) + offset
            offset = c[LANES - 1:LANES, :]
            for r, term in enumerate(_split3(c * LOG2E)):
                c3_ref[r, rows, :] = term
        carry_ref[...] = offset
        vt = lax.dot_general(wvt_ref[...], xn, (((1,), (1,)), ((), ())),
                             preferred_element_type=F32) + vb_ref[...]
        for ch in range(tm // tk):
            vt_ref[ch] = vt[:, ch * tk:(ch + 1) * tk].astype(BF16)
        decay = sum(jnp.dot(c3_ref[r], pl_ref[r], preferred_element_type=F32) for r in range(N_SPLIT))
        p_ref[...] = (jnp.dot(xn, w_ref[...], preferred_element_type=F32) + decay).astype(BF16)

    @pl.when(j > 0)
    def _():
        acc = jnp.dot(xn_ref[...], w_ref[...], preferred_element_type=F32)
        p_ref[...] = (acc * sc_ref[...] + bi_ref[...]).astype(BF16)


def _inproj(x, g, w, col_scale, col_bias, wf, bf, lt, place, wvt, vbias, seq, layer):
    n = x.shape[0]
    tm, tn = TM_PROJ, TN_PROJ
    assert P_K == 0 and tn == ATT_HEADS * HEAD_PAD
    kern = functools.partial(_inproj_kernel, tiles_per_batch=seq // tm)
    vt_rows = ATT_HEADS * HEAD_PAD
    const = lambda shape: pl.BlockSpec(shape, lambda i, j: (0,) * len(shape))
    return pl.pallas_call(
        kern,
        grid=(n // tm, P_COLS // tn),
        in_specs=[
            pl.BlockSpec((tm, D_MODEL), lambda i, j: (i, 0)),
            const((1, D_MODEL)),
            pl.BlockSpec((None, D_MODEL, tn), lambda i, j: (layer, 0, j)),
            pl.BlockSpec((1, tn), lambda i, j: (0, j)),
            pl.BlockSpec((1, tn), lambda i, j: (0, j)),
            pl.BlockSpec((None, D_MODEL, LANES), lambda i, j: (layer, 0, 0)),
            const((1, LANES)),
            const((LANES, LANES)),
            const((N_SPLIT, LANES, tn)),
            pl.BlockSpec((None, vt_rows, D_MODEL), lambda i, j: (layer, 0, 0)),
            const((vt_rows, 1)),
        ],
        out_specs=[
            pl.BlockSpec((tm, tn), lambda i, j: (i, j)),
            pl.BlockSpec((tm // TK, vt_rows, TK), lambda i, j: (i, 0, 0)),
        ],
        out_shape=[
            jax.ShapeDtypeStruct((n, P_COLS), BF16),
            jax.ShapeDtypeStruct((n // TK, vt_rows, TK), BF16),
        ],
        scratch_shapes=[pltpu.VMEM((tm, D_MODEL), BF16), pltpu.VMEM((N_SPLIT, tm, LANES), BF16),
                        pltpu.VMEM((1, LANES), F32)],
        compiler_params=_params(48, 2),
        name="inproj",
    )(x, g, w, col_scale, col_bias, wf, bf, lt, place, wvt, vbias)


def _conv_kernel(a1_ref, a2_ref, w_ref, cb_ref, g_ref, b_ref, o_ref, hext_ref, hsh_ref,
                 *, tiles_per_batch):
    i = pl.program_id(0)
    tm = a1_ref.shape[0]
    span = hsh_ref.shape[1]

    @pl.when(i % tiles_per_batch == 0)
    def _():
        hext_ref[0:CONV_HIST, :] = jnp.zeros((CONV_HIST, CONV_CH), F32)

    @pl.when(i % tiles_per_batch != 0)
    def _():
        hext_ref[0:CONV_HIST, :] = hext_ref[tm:tm + CONV_HIST, :]

    hext_ref[CONV_HIST:CONV_HIST + tm, :] = (
        a1_ref[...].astype(F32) * _sigmoid(a2_ref[...].astype(F32)))
    for s in range(1, SUBLANES):
        hsh_ref[s] = hext_ref[s:s + span, :]

    g = g_ref[...]
    b = b_ref[...]
    base = CONV_HIST - (CONV_K - 1)
    for r in range(0, tm, CONV_RB):
        acc = jnp.broadcast_to(cb_ref[...], (CONV_RB, CONV_CH))
        for j in range(CONV_K):
            s = (base + j) % SUBLANES
            a = r + base + j - s
            rows = hext_ref[a:a + CONV_RB, :] if s == 0 else hsh_ref[s, a:a + CONV_RB, :]
            acc = acc + w_ref[j:j + 1, :] * rows
        y = _layer_norm(acc, g, b)
        o_ref[r:r + CONV_RB, :] = (y * _sigmoid(y)).astype(BF16)


def _conv_branch(p, w, cb, g, b, seq):
    n = p.shape[0]
    tm = TM_BRANCH
    kern = functools.partial(_conv_kernel, tiles_per_batch=seq // tm)
    c0 = P_CONV // CONV_CH
    vec = pl.BlockSpec((1, CONV_CH), lambda i: (0, 0))
    return pl.pallas_call(
        kern,
        grid=(n // tm,),
        in_specs=[
            pl.BlockSpec((tm, CONV_CH), lambda i: (i, c0)),
            pl.BlockSpec((tm, CONV_CH), lambda i: (i, c0 + 1)),
            pl.BlockSpec((CONV_K + 1, CONV_CH), lambda i: (0, 0)),
            vec, vec, vec,
        ],
        out_specs=pl.BlockSpec((tm, CONV_CH), lambda i: (i, 0)),
        out_shape=jax.ShapeDtypeStruct((n, CONV_CH), BF16),
        scratch_shapes=[pltpu.VMEM((tm + CONV_HIST, CONV_CH), F32),
                        pltpu.VMEM((SUBLANES, tm + CONV_HIST - SUBLANES, CONV_CH), F32)],
        compiler_params=_params(32, 1),
        name="conv_branch",
    )(p, p, w, cb, g, b)


def _sg_kernel(u_ref, v_ref, g_ref, b_ref, w_ref, bias_ref, o_ref):
    tm = u_ref.shape[0]

    def gelu(z):
        return 0.5 * z * (1.0 + lax.erf(z * 0.7071067811865476))

    zu = gelu(u_ref[...].astype(F32))
    vn = _layer_norm(gelu(v_ref[...].astype(F32)), g_ref[...], b_ref[...]).astype(BF16)
    lane = lax.broadcasted_iota(jnp.int32, (SG_CHUNK, LANES), 1)
    first_group = lane < (SG_WIDTH // SG_GROUPS)
    for c in range(tm // SG_CHUNK):
        rows = slice(c * SG_CHUNK, (c + 1) * SG_CHUNK)
        for pr in range(SG_WIDTH // LANES):
            cols = slice(pr * LANES, (pr + 1) * LANES)
            vp = vn[rows, cols]
            m0 = jnp.dot(w_ref[2 * pr], vp, preferred_element_type=F32)
            m1 = jnp.dot(w_ref[2 * pr + 1], vp, preferred_element_type=F32)
            mixed = jnp.where(first_group, m0, m1) + bias_ref[:, cols]
            o_ref[rows, cols] = (zu[rows, cols] * mixed).astype(BF16)


def _sg_branch(p, g, b, w_tril, bias_full):
    n = p.shape[0]
    tm = TM_BRANCH
    c0 = P_SG // SG_WIDTH
    vec = pl.BlockSpec((1, SG_WIDTH), lambda i: (0, 0))
    return pl.pallas_call(
        _sg_kernel,
        grid=(n // tm,),
        in_specs=[
            pl.BlockSpec((tm, SG_WIDTH), lambda i: (i, c0)),
            pl.BlockSpec((tm, SG_WIDTH), lambda i: (i, c0 + 1)),
            vec, vec,
            pl.BlockSpec((SG_GROUPS, SG_CHUNK, SG_CHUNK), lambda i: (0, 0, 0)),
            pl.BlockSpec((SG_CHUNK, SG_WIDTH), lambda i: (0, 0)),
        ],
        out_specs=pl.BlockSpec((tm, SG_WIDTH), lambda i: (i, 0)),
        out_shape=jax.ShapeDtypeStruct((n, SG_WIDTH), BF16),
        compiler_params=_params(32, 1),
        name="sg_branch",
    )(p, p, g, b, w_tril, bias_full)


def _attn_kernel(q_ref, k_ref, vt_ref, o_ref, sa_ref, sb_ref, m_ref, acc_ref):
    i = pl.program_id(2)
    tq = q_ref.shape[0]
    tk = vt_ref.shape[-1]
    n_full = (i * tq) // tk
    head_cols = [slice(hh * HEAD_PAD, (hh + 1) * HEAD_PAD) for hh in range(2)]

    def scores(j, dst_ref):
        start = pl.multiple_of(j * tk, tk)
        for hh, cols in enumerate(head_cols):
            dst_ref[hh] = lax.dot_general(k_ref[pl.ds(start, tk), cols], q_ref[:, cols],
                                          (((1,), (1,)), ((), ())), preferred_element_type=F32)

    def consume(j, src_ref, masked):
        probs = []
        for hh in range(2):
            s = src_ref[hh]
            if masked:
                key = j * tk + lax.broadcasted_iota(jnp.int32, (tk, tq), 0)
                qry = i * tq + lax.broadcasted_iota(jnp.int32, (tk, tq), 1)
                s = jnp.where(key <= qry, s, -jnp.inf)
            m = m_ref[hh]
            m_new = jnp.maximum(m, jnp.max(s, axis=0, keepdims=True))
            m_ref[hh] = m_new
            probs.append((jnp.exp2(m - m_new), jnp.exp2(s - m_new).astype(BF16)))
        for hh, (alpha, pexp) in enumerate(probs):
            pv = jnp.dot(vt_ref[j, head_cols[hh], :], pexp, preferred_element_type=F32)
            acc_ref[hh] = alpha * acc_ref[hh] + pv

    m_ref[...] = jnp.full(m_ref.shape, -jnp.inf, F32)
    acc_ref[...] = jnp.zeros(acc_ref.shape, F32)
    scores(0, sa_ref)

    def two_tiles(u, carry):
        j = 2 * u
        scores(j + 1, sb_ref)
        consume(j, sa_ref, False)
        scores(j + 2, sa_ref)
        consume(j + 1, sb_ref, False)
        return carry

    lax.fori_loop(0, n_full // 2, two_tiles, 0)

    @pl.when(n_full % 2 == 1)
    def _():
        scores(n_full, sb_ref)
        consume(n_full - 1, sa_ref, False)
        consume(n_full, sb_ref, True)

    @pl.when(n_full % 2 == 0)
    def _():
        consume(n_full, sa_ref, True)

    lane = lax.broadcasted_iota(jnp.int32, (tq, LANES), 1)
    outs = []
    for hh in range(2):
        acc_t = acc_ref[hh].T
        ones_lane = HEAD_DIM if hh == 0 else 0
        outs.append(acc_t / acc_t[:, ones_lane:ones_lane + 1])
    o_ref[...] = jnp.where(lane < HEAD_DIM, outs[0], outs[1]).astype(BF16)


def _attention(p, vt, batch, seq):
    n = p.shape[0]
    pairs = ATT_HEADS // 2
    qt = seq // TQ
    pw = 2 * HEAD_PAD
    return pl.pallas_call(
        _attn_kernel,
        grid=(batch, pairs, qt),
        in_specs=[
            pl.BlockSpec((TQ, pw), lambda b, h, i: (b * qt + i, P_Q // pw + h)),
            pl.BlockSpec((seq, pw), lambda b, h, i: (b, P_K // pw + h)),
            pl.BlockSpec((seq // TK, pw, TK), lambda b, h, i: (b, h, 0)),
        ],
        out_specs=pl.BlockSpec((TQ, LANES), lambda b, h, i: (b * qt + i, h)),
        out_shape=jax.ShapeDtypeStruct((n, ATT_WIDTH), BF16),
        scratch_shapes=[pltpu.VMEM((2, TK, TQ), F32), pltpu.VMEM((2, TK, TQ), F32),
                        pltpu.VMEM((2, 1, TQ), F32), pltpu.VMEM((2, HEAD_PAD, TQ), F32)],
        compiler_params=_params(32, 3),
        name="fox_attention",
    )(p, p, vt)


def _merge_kernel(x_ref, hc_ref, ha_ref, hs_ref, g0_ref, g1_ref, g2_ref, bg_ref,
                  wc_ref, wa_ref, ws_ref, wm_ref, o_ref):
    merged = None
    for k, (h_ref, w_ref, gl_ref) in enumerate(((hc_ref, wc_ref, g0_ref), (ha_ref, wa_ref, g1_ref),
                                                (hs_ref, ws_ref, g2_ref))):
        y = jnp.dot(h_ref[...], w_ref[...], preferred_element_type=F32)
        gate = _sigmoid(gl_ref[...].astype(F32) + bg_ref[:, k * D_MODEL:(k + 1) * D_MODEL])
        merged = gate * y if merged is None else merged + gate * y
    o_ref[...] = x_ref[...] + jnp.dot(merged.astype(BF16), wm_ref[...], preferred_element_type=F32)


def _merge(x, hc, ha, hs, p, bg, wc, wa, ws, wm, layer):
    n = x.shape[0]
    tm = TM_BRANCH
    g0 = P_GATE // D_MODEL
    half = lambda: pl.BlockSpec((tm, CONV_CH), lambda i: (i, 0))
    wspec = lambda k: pl.BlockSpec((None, k, D_MODEL), lambda i: (layer, 0, 0))
    return pl.pallas_call(
        _merge_kernel,
        grid=(n // tm,),
        in_specs=[
            pl.BlockSpec((tm, D_MODEL), lambda i: (i, 0)),
            half(), half(), half(),
            pl.BlockSpec((tm, D_MODEL), lambda i: (i, g0)),
            pl.BlockSpec((tm, D_MODEL), lambda i: (i, g0 + 1)),
            pl.BlockSpec((tm, D_MODEL), lambda i: (i, g0 + 2)),
            pl.BlockSpec((1, N_BRANCH * D_MODEL), lambda i: (0, 0)),
            wspec(CONV_CH), wspec(ATT_WIDTH), wspec(SG_WIDTH), wspec(D_MODEL),
        ],
        out_specs=pl.BlockSpec((tm, D_MODEL), lambda i: (i, 0)),
        out_shape=jax.ShapeDtypeStruct((n, D_MODEL), F32),
        compiler_params=_params(48, 1),
        name="merge",
    )(x, hc, ha, hs, p, p, p, bg, wc, wa, ws, wm)


def _ffn_kernel(x_ref, g_ref, w1_ref, w3_ref, w2_ref, o_ref, hn_ref, acc_ref):
    f = pl.program_id(1)

    @pl.when(f == 0)
    def _():
        hn_ref[...] = _rms(x_ref[...], g_ref[...]).astype(BF16)
        acc_ref[...] = jnp.zeros_like(acc_ref)

    hn = hn_ref[...]
    a = jnp.dot(hn, w1_ref[...], preferred_element_type=F32)
    b = jnp.dot(hn, w3_ref[...], preferred_element_type=F32)
    t = (a * _sigmoid(a) * b).astype(BF16)
    acc_ref[...] += jnp.dot(t, w2_ref[...], preferred_element_type=F32)

    @pl.when(f == pl.num_programs(1) - 1)
    def _():
        o_ref[...] = x_ref[...] + acc_ref[...]


def _ffn(x, g, w1, w3, w2, layer):
    n = x.shape[0]
    tm, tf = TM_FFN, TF_FFN
    return pl.pallas_call(
        _ffn_kernel,
        grid=(n // tm, D_FF // tf),
        in_specs=[
            pl.BlockSpec((tm, D_MODEL), lambda i, f: (i, 0)),
            pl.BlockSpec((1, D_MODEL), lambda i, f: (0, 0)),
            pl.BlockSpec((None, D_MODEL, tf), lambda i, f: (layer, 0, f)),
            pl.BlockSpec((None, D_MODEL, tf), lambda i, f: (layer, 0, f)),
            pl.BlockSpec((None, tf, D_MODEL), lambda i, f: (layer, f, 0)),
        ],
        out_specs=pl.BlockSpec((tm, D_MODEL), lambda i, f: (i, 0)),
        out_shape=jax.ShapeDtypeStruct((n, D_MODEL), F32),
        scratch_shapes=[pltpu.VMEM((tm, D_MODEL), BF16), pltpu.VMEM((tm, D_MODEL), F32)],
        compiler_params=_params(48, 2),
        name="ffn_dense",
    )(x, g, w1, w3, w2)


def _router_kernel(x_ref, g_ref, wr_ref, lt_ref, h_ref, meta_ref, cnt_ref, carry_ref):
    i = pl.program_id(0)
    tm = x_ref.shape[0]

    @pl.when(i == 0)
    def _():
        carry_ref[...] = jnp.zeros_like(carry_ref)

    h = _rms(x_ref[...], g_ref[...])
    h_ref[...] = h
    logits = jnp.dot(h, wr_ref[...], preferred_element_type=F32, precision=lax.Precision.HIGHEST)
    lane = lax.broadcasted_iota(jnp.int32, (tm, LANES), 1)
    lanef = lane.astype(F32)
    lg = jnp.where(lane < N_EXPERTS, logits, -jnp.inf)
    v1 = jnp.max(lg, axis=1, keepdims=True)
    i1 = jnp.min(jnp.where(lg == v1, lanef, float(LANES)), axis=1, keepdims=True)
    lg2 = jnp.where(lanef == i1, -jnp.inf, lg)
    v2 = jnp.max(lg2, axis=1, keepdims=True)
    i2 = jnp.min(jnp.where(lg2 == v2, lanef, float(LANES)), axis=1, keepdims=True)
    e = jnp.exp(v2 - v1)
    g1 = 1.0 / (1.0 + e)
    g2 = e / (1.0 + e)
    oh1 = lanef == i1
    oh2 = lanef == i2
    cnt = (oh1.astype(F32) + oh2.astype(F32))
    before = jnp.dot(lt_ref[...], cnt.astype(BF16), preferred_element_type=F32) + carry_ref[...]
    r1 = jnp.sum(jnp.where(oh1, before, 0.0), axis=1, keepdims=True)
    r2 = jnp.sum(jnp.where(oh2, before, 0.0), axis=1, keepdims=True)
    total = carry_ref[...] + jnp.sum(cnt, axis=0, keepdims=True)
    carry_ref[...] = total
    cnt_ref[...] = jnp.broadcast_to(total, cnt_ref.shape)
    meta = jnp.zeros((tm, LANES), F32)
    for k, val in enumerate((i1, i2, g1, g2, r1, r2)):
        meta = jnp.where(lane == k, val, meta)
    meta_ref[...] = meta


def _router(x, g, wr, lt):
    n = x.shape[0]
    tm = TM_BRANCH
    return pl.pallas_call(
        _router_kernel,
        grid=(n // tm,),
        in_specs=[
            pl.BlockSpec((tm, D_MODEL), lambda i: (i, 0)),
            pl.BlockSpec((1, D_MODEL), lambda i: (0, 0)),
            pl.BlockSpec((D_MODEL, LANES), lambda i: (0, 0)),
            pl.BlockSpec((tm, tm), lambda i: (0, 0)),
        ],
        out_specs=[
            pl.BlockSpec((tm, D_MODEL), lambda i: (i, 0)),
            pl.BlockSpec((tm, LANES), lambda i: (i, 0)),
            pl.BlockSpec((8, LANES), lambda i: (0, 0)),
        ],
        out_shape=[
            jax.ShapeDtypeStruct((n, D_MODEL), F32),
            jax.ShapeDtypeStruct((n, LANES), F32),
            jax.ShapeDtypeStruct((8, LANES), F32),
        ],
        scratch_shapes=[pltpu.VMEM((1, LANES), F32)],
        compiler_params=_params(32, 1),
        name="router",
    )(x, g, wr, lt)


def _row_copy(src_ref, src_row, dst_ref, dst_row, sem):
    return pltpu.make_async_copy(src_ref.at[pl.ds(src_row, 1), :], dst_ref.at[pl.ds(dst_row, 1), :], sem)


def _dispatch_kernel(bounds_ref, dest_ref, h_ref, xs_ref, zero_ref, sem, zero_sem):
    tm = h_ref.shape[0]
    n_blk = xs_ref.shape[0] // MOE_BLK

    @pl.when(pl.program_id(0) == 0)
    def _():
        zero_ref[...] = jnp.zeros_like(zero_ref)

        def zero_block(blk):
            start = pl.multiple_of(blk * MOE_BLK, MOE_BLK)
            copy = pltpu.make_async_copy(zero_ref, xs_ref.at[pl.ds(start, MOE_BLK), :], zero_sem)
            copy.start()
            copy.wait()

        for e in range(N_EXPERTS):
            @pl.when(bounds_ref[e + 1] > bounds_ref[e])
            def _():
                zero_block(bounds_ref[e + 1] // MOE_BLK - 1)

        def unused(blk, carry):
            zero_block(blk)
            return carry

        lax.fori_loop(bounds_ref[N_EXPERTS] // MOE_BLK, n_blk, unused, 0)

    def issue(r, carry):
        for k in range(2):
            _row_copy(h_ref, r, xs_ref, dest_ref[2 * r + k], sem).start()
        return carry

    lax.fori_loop(0, tm, issue, 0, unroll=8)
    for _ in range(2):
        pltpu.make_async_copy(h_ref, xs_ref.at[pl.ds(0, tm), :], sem).wait()


def _dispatch(bounds, dest_flat, h, n_rows):
    n = h.shape[0]
    tm = TM_ROW
    return pl.pallas_call(
        _dispatch_kernel,
        grid=(n // tm,),
        in_specs=[
            pl.BlockSpec(memory_space=pltpu.SMEM),
            pl.BlockSpec((2 * tm,), lambda i: (i,), memory_space=pltpu.SMEM),
            pl.BlockSpec((tm, D_MODEL), lambda i: (i, 0)),
        ],
        out_specs=pl.BlockSpec(memory_space=pl.ANY),
        out_shape=jax.ShapeDtypeStruct((n_rows, D_MODEL), F32),
        scratch_shapes=[pltpu.VMEM((MOE_BLK, D_MODEL), F32), pltpu.SemaphoreType.DMA(()),
                        pltpu.SemaphoreType.DMA(())],
        compiler_params=_params(32, 1),
        name="moe_dispatch",
    )(bounds, dest_flat, h)


def _first_block_of_expert(b, be_ref):
    return jnp.logical_or(b == 0, be_ref[b] != be_ref[jnp.maximum(b - 1, 0)])


def _moe_kernel(be_ref, nu_ref, x_ref, w1_ref, w3_ref, w2_ref, y_ref, xb_ref, acc_ref,
                wb1_ref, wb3_ref, wb2_ref):
    b = pl.program_id(0)
    f = pl.program_id(1)

    @pl.when(b < nu_ref[0])
    def _():
        @pl.when(_first_block_of_expert(b, be_ref))
        def _():
            wb1_ref[f] = w1_ref[...].astype(BF16)
            wb3_ref[f] = w3_ref[...].astype(BF16)
            wb2_ref[f] = w2_ref[...].astype(BF16)

        @pl.when(f == 0)
        def _():
            xb_ref[...] = x_ref[...].astype(BF16)
            acc_ref[...] = jnp.zeros_like(acc_ref)

        xb = xb_ref[...]
        a = jnp.dot(xb, wb1_ref[f], preferred_element_type=F32)
        c = jnp.dot(xb, wb3_ref[f], preferred_element_type=F32)
        t = (a * _sigmoid(a) * c).astype(BF16)
        acc_ref[...] += jnp.dot(t, wb2_ref[f], preferred_element_type=F32)

        @pl.when(f == pl.num_programs(1) - 1)
        def _():
            y_ref[...] = acc_ref[...]

    @pl.when(jnp.logical_and(b >= nu_ref[0], f == pl.num_programs(1) - 1))
    def _():
        y_ref[...] = jnp.zeros_like(y_ref)


def _moe_experts(blk_e, n_used, xs, w1, w3, w2, layer):
    rows = xs.shape[0]
    n_blk = rows // MOE_BLK
    nf = D_FF_EXPERT // TF_MOE

    def row_idx(b, f, be, nu):
        return (jnp.minimum(b, nu[0] - 1), 0)

    def f_idx(b, f, be, nu):
        fetch = jnp.logical_and(b < nu[0], _first_block_of_expert(b, be))
        return jnp.where(fetch, f, nf - 1)

    grid_spec = pltpu.PrefetchScalarGridSpec(
        num_scalar_prefetch=2,
        grid=(n_blk, nf),
        in_specs=[
            pl.BlockSpec((MOE_BLK, D_MODEL), row_idx),
            pl.BlockSpec((None, None, D_MODEL, TF_MOE),
                         lambda b, f, be, nu: (layer, be[b], 0, f_idx(b, f, be, nu))),
            pl.BlockSpec((None, None, D_MODEL, TF_MOE),
                         lambda b, f, be, nu: (layer, be[b], 0, f_idx(b, f, be, nu))),
            pl.BlockSpec((None, None, TF_MOE, D_MODEL),
                         lambda b, f, be, nu: (layer, be[b], f_idx(b, f, be, nu), 0)),
        ],
        out_specs=pl.BlockSpec((MOE_BLK, D_MODEL), lambda b, f, be, nu: (b, 0)),
        scratch_shapes=[pltpu.VMEM((MOE_BLK, D_MODEL), BF16), pltpu.VMEM((MOE_BLK, D_MODEL), F32),
                        pltpu.VMEM((nf, D_MODEL, TF_MOE), BF16), pltpu.VMEM((nf, D_MODEL, TF_MOE), BF16),
                        pltpu.VMEM((nf, TF_MOE, D_MODEL), BF16)],
    )
    return pl.pallas_call(
        _moe_kernel,
        grid_spec=grid_spec,
        out_shape=jax.ShapeDtypeStruct((rows, D_MODEL), F32),
        compiler_params=_params(56, 2),
        name="moe_experts",
    )(blk_e, n_used, xs, w1, w3, w2)


def _combine_kernel(dest_ref, x_ref, meta_ref, gfin_ref, y_ref, o_ref, ybuf_ref, sem, *, final_norm):
    tm = x_ref.shape[0]

    def issue(r, carry):
        for k in range(2):
            _row_copy(y_ref, dest_ref[2 * r + k], ybuf_ref.at[k], r, sem).start()
        return carry

    lax.fori_loop(0, tm, issue, 0, unroll=8)
    for k in range(2):
        pltpu.make_async_copy(y_ref.at[pl.ds(0, tm), :], ybuf_ref.at[k], sem).wait()

    g1 = meta_ref[:, 2:3]
    g2 = meta_ref[:, 3:4]
    out = x_ref[...] + (g1 * ybuf_ref[0] + g2 * ybuf_ref[1])
    if final_norm:
        out = _rms(out, gfin_ref[...])
    o_ref[...] = out


def _combine(dest_flat, x, meta, gfin, y, final_norm):
    n = x.shape[0]
    tm = TM_ROW
    kern = functools.partial(_combine_kernel, final_norm=final_norm)
    return pl.pallas_call(
        kern,
        grid=(n // tm,),
        in_specs=[
            pl.BlockSpec((2 * tm,), lambda i: (i,), memory_space=pltpu.SMEM),
            pl.BlockSpec((tm, D_MODEL), lambda i: (i, 0)),
            pl.BlockSpec((tm, LANES), lambda i: (i, 0)),
            pl.BlockSpec((1, D_MODEL), lambda i: (0, 0)),
            pl.BlockSpec(memory_space=pl.ANY),
        ],
        out_specs=pl.BlockSpec((tm, D_MODEL), lambda i: (i, 0)),
        out_shape=jax.ShapeDtypeStruct((n, D_MODEL), F32),
        scratch_shapes=[pltpu.VMEM((2, tm, D_MODEL), F32), pltpu.SemaphoreType.DMA(())],
        compiler_params=_params(32, 1),
        name="moe_combine",
    )(dest_flat, x, meta, gfin, y)


def _moe_layer(x, g, wr, lt, w1, w3, w2, layer, gfin, final_norm):
    n = x.shape[0]
    h, meta, cnt = _router(x, g, wr, lt)
    expert = meta[:, 0:2].astype(jnp.int32)
    rank = meta[:, 4:6].astype(jnp.int32)
    counts = cnt[0, :N_EXPERTS].astype(jnp.int32)
    padded = (counts + MOE_BLK - 1) // MOE_BLK * MOE_BLK
    pad_end = jnp.cumsum(padded)
    pad_start = pad_end - padded
    dest = (pad_start[expert] + rank).reshape(-1)
    n_blk = (2 * n) // MOE_BLK + N_EXPERTS
    blk_start = jnp.arange(n_blk, dtype=jnp.int32) * MOE_BLK
    blk_e = jnp.minimum(jnp.sum(blk_start[:, None] >= pad_end[None, :], axis=1), N_EXPERTS - 1)
    n_used = (pad_end[-1:] // MOE_BLK).astype(jnp.int32)
    bounds = jnp.concatenate([jnp.zeros((1,), jnp.int32), pad_end.astype(jnp.int32)])
    xs = _dispatch(bounds, dest, h, n_blk * MOE_BLK)
    y = _moe_experts(blk_e.astype(jnp.int32), n_used, xs, w1, w3, w2, layer)
    return _combine(dest, x, meta, gfin, y, final_norm)


def _prepare_mixer(w_in, b_forget, conv_w, w_spatial, b_spatial, w_conv_out, w_att_out, w_sg_out,
                   w_mix_out):
    depth, d, _ = w_in.shape

    def pad_heads(w):
        w = w.reshape(depth, d, ATT_HEADS, HEAD_DIM)
        w = jnp.pad(w, ((0, 0), (0, 0), (0, 0), (0, HEAD_PAD - HEAD_DIM)))
        return w.reshape(depth, d, ATT_HEADS * HEAD_PAD)

    prep = {}
    prep["w_main"] = jnp.concatenate(
        [pad_heads(w_in[:, :, OFF_K:OFF_V]), w_in[:, :, OFF_CONV:OFF_Q], w_in[:, :, OFF_SG:OFF_GATE],
         w_in[:, :, OFF_GATE:], pad_heads(w_in[:, :, OFF_Q:OFF_K])], axis=2).astype(BF16)
    col = jnp.arange(P_COLS)
    is_q = col >= P_Q
    spare = (col % HEAD_PAD >= HEAD_DIM) & (col % HEAD_PAD < HEAD_DIM + N_SPLIT)
    prep["col_scale"] = jnp.where(is_q, LOG2E * HEAD_DIM ** -0.5, 1.0).astype(F32)[None, :]
    prep["col_bias"] = jnp.where(is_q & spare, 1.0, 0.0).astype(F32)[None, :]
    kcol = jnp.arange(ATT_HEADS * HEAD_PAD)
    prep["place"] = -((kcol[None, None, :] // HEAD_PAD == jnp.arange(LANES)[None, :, None])
                      & (kcol[None, None, :] % HEAD_PAD == HEAD_DIM + jnp.arange(N_SPLIT)[:, None, None])
                      ).astype(BF16)
    prep["wf"] = jnp.pad(w_in[:, :, OFF_F:OFF_SG], ((0, 0), (0, 0), (0, LANES - ATT_HEADS))).astype(BF16)
    prep["bf"] = jnp.pad(b_forget, ((0, 0), (0, LANES - ATT_HEADS)))[:, None, :]
    wv_t = jnp.swapaxes(w_in[:, :, OFF_V:OFF_F], 1, 2).reshape(depth, ATT_HEADS // 2, 2, HEAD_DIM, d)
    zeros_h = jnp.zeros_like(wv_t[:, :, 0])
    wvt = jnp.concatenate([wv_t[:, :, 0], zeros_h, zeros_h, wv_t[:, :, 1]], axis=2)
    prep["wvt"] = wvt.reshape(depth, ATT_HEADS * HEAD_PAD, d).astype(BF16)
    vrow = jnp.arange(ATT_HEADS * HEAD_PAD) % (2 * HEAD_PAD)
    prep["vbias"] = ((vrow == HEAD_DIM) | (vrow == HEAD_PAD)).astype(F32)[:, None]
    prep["lt_incl"] = jnp.tril(jnp.ones((LANES, LANES), F32)).astype(BF16)
    prep["conv_w"] = jnp.pad(conv_w, ((0, 0), (0, 1), (0, 0)))
    causal = jnp.tril(jnp.ones((SG_CHUNK, SG_CHUNK), bool))
    prep["w_sp"] = jnp.where(causal[None, None], w_spatial, 0).astype(BF16)
    prep["sg_bias"] = jnp.repeat(jnp.swapaxes(b_spatial, 1, 2), SG_WIDTH // SG_GROUPS, axis=2)
    for name, w in (("wc", w_conv_out), ("wa", w_att_out), ("ws", w_sg_out), ("wm", w_mix_out)):
        prep[name] = w.astype(BF16)
    return prep


def _token_mixer(xf, prep, vecs, layer, batch, seq):
    row = lambda name: vecs[name][layer][None, :]
    p, vt = _inproj(xf, row("mix_norm_g"), prep["w_main"], prep["col_scale"], prep["col_bias"],
                    prep["wf"], prep["bf"][layer], prep["lt_incl"], prep["place"],
                    prep["wvt"], prep["vbias"], seq, layer)
    hc = _conv_branch(p, prep["conv_w"][layer], row("conv_b"), row("conv_ln_g"), row("conv_ln_b"), seq)
    hs = _sg_branch(p, row("sg_ln_g"), row("sg_ln_b"), prep["w_sp"][layer], prep["sg_bias"][layer])
    ha = _attention(p, vt, batch, seq)
    return _merge(xf, hc, ha, hs, p, row("b_gate"), prep["wc"], prep["wa"], prep["ws"], prep["wm"], layer)


def kernel(x, mix_norm_g, w_in, b_forget, b_gate, conv_w, conv_b, conv_ln_g, conv_ln_b, w_conv_out,
           w_att_out, sg_ln_g, sg_ln_b, w_spatial, b_spatial, w_sg_out, w_mix_out, ffn_norm_g,
           ffn_w1, ffn_w3, ffn_w2, router_w, moe_w1, moe_w3, moe_w2, final_norm_g):
    batch, seq, d = x.shape
    depth = w_in.shape[0]
    if depth % 2 == 1:
        raise NotImplementedError("the final norm is fused into the last (expert) layer")
    xf = x.reshape(batch * seq, d)

    prep = _prepare_mixer(w_in, b_forget, conv_w, w_spatial, b_spatial, w_conv_out, w_att_out,
                          w_sg_out, w_mix_out)
    vecs = dict(mix_norm_g=mix_norm_g, conv_b=conv_b, conv_ln_g=conv_ln_g, conv_ln_b=conv_ln_b,
                sg_ln_g=sg_ln_g, sg_ln_b=sg_ln_b, b_gate=b_gate)
    f1, f3, f2 = (w.astype(BF16) for w in (ffn_w1, ffn_w3, ffn_w2))
    m1, m3, m2 = moe_w1, moe_w3, moe_w2
    wr = jnp.pad(router_w, ((0, 0), (0, 0), (0, LANES - N_EXPERTS)))
    lt = jnp.tril(jnp.ones((TM_BRANCH, TM_BRANCH), F32), -1).astype(BF16)
    gfin = final_norm_g[None, :]

    for layer in range(depth):
        xf = _token_mixer(xf, prep, vecs, layer, batch, seq)
        g_ffn = ffn_norm_g[layer][None, :]
        i = layer // 2
        if layer % 2 == 0:
            xf = _ffn(xf, g_ffn, f1, f3, f2, i)
        else:
            xf = _moe_layer(xf, g_ffn, wr[i], lt, m1, m3, m2, i, gfin,
                            final_norm=(layer == depth - 1))
    return xf.reshape(batch, seq, d)
```

```python
import functools

import jax
import jax.numpy as jnp
from jax import lax
from jax.experimental import pallas as pl
from jax.experimental.pallas import tpu as pltpu

F32 = jnp.float32
BF16 = jnp.bfloat16

D_MODEL = 1024
CONV_CH = 512
CONV_K = 31
ATT_HEADS = 8
HEAD_DIM = 64
ATT_WIDTH = ATT_HEADS * HEAD_DIM
SG_GROUPS = 8
SG_WIDTH = 512
SG_CHUNK = 128
N_BRANCH = 3
OFF_CONV = 0
OFF_Q = OFF_CONV + 2 * CONV_CH
OFF_K = OFF_Q + ATT_WIDTH
OFF_V = OFF_K + ATT_WIDTH
OFF_F = OFF_V + ATT_WIDTH
OFF_SG = OFF_F + ATT_HEADS
OFF_GATE = OFF_SG + 2 * SG_WIDTH
D_FF = 2816
N_EXPERTS = 8
D_FF_EXPERT = 3584
EPS = 1e-6

LANES = 128
SUBLANES = 8
MIB = 1024 * 1024

HEAD_PAD = LANES
P_K = 0
P_CONV = P_K + ATT_HEADS * HEAD_PAD
P_SG = P_CONV + 2 * CONV_CH
P_GATE = P_SG + 2 * SG_WIDTH
P_Q = P_GATE + N_BRANCH * D_MODEL
P_COLS = P_Q + ATT_HEADS * HEAD_PAD
N_SPLIT = 3
LOG2E = 1.4426950408889634

TM_PROJ = 1024
TN_PROJ = 1024
TM_BRANCH = 512
CONV_HIST = 32
CONV_RB = 64
TQ = 512
TK = 512
TM_FFN = 1024
TF_FFN = 256
MOE_BLK = 512
TF_MOE = 512
TM_ROW = 256


def _params(vmem_mib, n_axes):
    return pltpu.CompilerParams(dimension_semantics=("arbitrary",) * n_axes,
                                vmem_limit_bytes=vmem_mib * MIB)


def _rms(x, g):
    return x * lax.rsqrt(jnp.mean(x * x, axis=-1, keepdims=True) + EPS) * g


def _layer_norm(x, g, b):
    mu = jnp.mean(x, axis=-1, keepdims=True)
    d = x - mu
    var = jnp.mean(d * d, axis=-1, keepdims=True)
    return d * lax.rsqrt(var + EPS) * g + b


def _sigmoid(x):
    return 1.0 / (1.0 + jnp.exp(-x))


def _split3(x):
    hi = x.astype(BF16)
    r1 = x - hi.astype(F32)
    mid = r1.astype(BF16)
    lo = (r1 - mid.astype(F32)).astype(BF16)
    return hi, mid, lo


def _inproj_kernel(x_ref, g_ref, w_ref, sc_ref, bi_ref, wf_ref, bf_ref, lt_ref, pl_ref, wvt_ref,
                   vb_ref, p_ref, vt_ref, xn_ref, c3_ref, carry_ref, *, tiles_per_batch):
    i = pl.program_id(0)
    j = pl.program_id(1)
    tm = x_ref.shape[0]
    tk = vt_ref.shape[-1]

    @pl.when(jnp.logical_and(j == 0, i % tiles_per_batch == 0))
    def _():
        carry_ref[...] = jnp.zeros_like(carry_ref)

    @pl.when(j == 0)
    def _():
        xn = _rms(x_ref[...], g_ref[...]).astype(BF16)
        xn_ref[...] = xn
        f = jnp.dot(xn, wf_ref[...], preferred_element_type=F32) + bf_ref[...]
        ls = jnp.minimum(f, 0.0) - jnp.log1p(jnp.exp(-jnp.abs(f)))
        lt = lt_ref[...]
        terms = _split3(ls)
        offset = carry_ref[...]
        for blk in range(tm // LANES):
            rows = slice(blk * LANES, (blk + 1) * LANES)
            c = sum(jnp.dot(lt, term[rows], preferred_element_type=F32) for term in terms) + offset
            offset = c[LANES - 1:LANES, :]
            for r, term in enumerate(_split3(c * LOG2E)):
                c3_ref[rows, r * LANES:(r + 1) * LANES] = term
        carry_ref[...] = offset
        vt = lax.dot_general(wvt_ref[...], xn, (((1,), (1,)), ((), ())),
                             preferred_element_type=F32) + vb_ref[...]
        for ch in range(tm // tk):
            vt_ref[ch] = vt[:, ch * tk:(ch + 1) * tk].astype(BF16)
        decay = jnp.dot(c3_ref[...], pl_ref[...], preferred_element_type=F32)
        p_ref[...] = (jnp.dot(xn, w_ref[...], preferred_element_type=F32) + decay).astype(BF16)

    @pl.when(j > 0)
    def _():
        acc = jnp.dot(xn_ref[...], w_ref[...], preferred_element_type=F32)
        p_ref[...] = (acc * sc_ref[...] + bi_ref[...]).astype(BF16)


def _inproj(x, g, w, col_scale, col_bias, wf, bf, lt, place, wvt, vbias, seq, layer):
    n = x.shape[0]
    tm, tn = TM_PROJ, TN_PROJ
    assert P_K == 0 and tn == ATT_HEADS * HEAD_PAD
    kern = functools.partial(_inproj_kernel, tiles_per_batch=seq // tm)
    vt_rows = ATT_HEADS * HEAD_PAD
    const = lambda shape: pl.BlockSpec(shape, lambda i, j: (0,) * len(shape))
    return pl.pallas_call(
        kern,
        grid=(n // tm, P_COLS // tn),
        in_specs=[
            pl.BlockSpec((tm, D_MODEL), lambda i, j: (i, 0)),
            const((1, D_MODEL)),
            pl.BlockSpec((None, D_MODEL, tn), lambda i, j: (layer, 0, j)),
            pl.BlockSpec((1, tn), lambda i, j: (0, j)),
            pl.BlockSpec((1, tn), lambda i, j: (0, j)),
            pl.BlockSpec((None, D_MODEL, LANES), lambda i, j: (layer, 0, 0)),
            const((1, LANES)),
            const((LANES, LANES)),
            const((N_SPLIT * LANES, tn)),
            pl.BlockSpec((None, vt_rows, D_MODEL), lambda i, j: (layer, 0, 0)),
            const((vt_rows, 1)),
        ],
        out_specs=[
            pl.BlockSpec((tm, tn), lambda i, j: (i, j)),
            pl.BlockSpec((tm // TK, vt_rows, TK), lambda i, j: (i, 0, 0)),
        ],
        out_shape=[
            jax.ShapeDtypeStruct((n, P_COLS), BF16),
            jax.ShapeDtypeStruct((n // TK, vt_rows, TK), BF16),
        ],
        scratch_shapes=[pltpu.VMEM((tm, D_MODEL), BF16), pltpu.VMEM((tm, N_SPLIT * LANES), BF16),
                        pltpu.VMEM((1, LANES), F32)],
        compiler_params=_params(48, 2),
        name="inproj",
    )(x, g, w, col_scale, col_bias, wf, bf, lt, place, wvt, vbias)


def _conv_kernel(a1_ref, a2_ref, w_ref, cb_ref, g_ref, b_ref, o_ref, hext_ref, hsh_ref,
                 *, tiles_per_batch):
    i = pl.program_id(0)
    tm = a1_ref.shape[0]
    span = hsh_ref.shape[1]

    @pl.when(i % tiles_per_batch == 0)
    def _():
        hext_ref[0:CONV_HIST, :] = jnp.zeros((CONV_HIST, CONV_CH), F32)

    @pl.when(i % tiles_per_batch != 0)
    def _():
        hext_ref[0:CONV_HIST, :] = hext_ref[tm:tm + CONV_HIST, :]

    hext_ref[CONV_HIST:CONV_HIST + tm, :] = (
        a1_ref[...].astype(F32) * _sigmoid(a2_ref[...].astype(F32)))
    for s in range(1, SUBLANES):
        hsh_ref[s] = hext_ref[s:s + span, :]

    g = g_ref[...]
    b = b_ref[...]
    base = CONV_HIST - (CONV_K - 1)
    for r in range(0, tm, CONV_RB):
        acc = jnp.broadcast_to(cb_ref[...], (CONV_RB, CONV_CH))
        for j in range(CONV_K):
            s = (base + j) % SUBLANES
            a = r + base + j - s
            rows = hext_ref[a:a + CONV_RB, :] if s == 0 else hsh_ref[s, a:a + CONV_RB, :]
            acc = acc + w_ref[j:j + 1, :] * rows
        y = _layer_norm(acc, g, b)
        o_ref[r:r + CONV_RB, :] = (y * _sigmoid(y)).astype(BF16)


def _conv_branch(p, w, cb, g, b, seq):
    n = p.shape[0]
    tm = TM_BRANCH
    kern = functools.partial(_conv_kernel, tiles_per_batch=seq // tm)
    c0 = P_CONV // CONV_CH
    vec = pl.BlockSpec((1, CONV_CH), lambda i: (0, 0))
    return pl.pallas_call(
        kern,
        grid=(n // tm,),
        in_specs=[
            pl.BlockSpec((tm, CONV_CH), lambda i: (i, c0)),
            pl.BlockSpec((tm, CONV_CH), lambda i: (i, c0 + 1)),
            pl.BlockSpec((CONV_K + 1, CONV_CH), lambda i: (0, 0)),
            vec, vec, vec,
        ],
        out_specs=pl.BlockSpec((tm, CONV_CH), lambda i: (i, 0)),
        out_shape=jax.ShapeDtypeStruct((n, CONV_CH), BF16),
        scratch_shapes=[pltpu.VMEM((tm + CONV_HIST, CONV_CH), F32),
                        pltpu.VMEM((SUBLANES, tm + CONV_HIST - SUBLANES, CONV_CH), F32)],
        compiler_params=_params(32, 1),
        name="conv_branch",
    )(p, p, w, cb, g, b)


def _sg_kernel(u_ref, v_ref, g_ref, b_ref, w_ref, bias_ref, o_ref):
    tm = u_ref.shape[0]

    def gelu(z):
        return 0.5 * z * (1.0 + lax.erf(z * 0.7071067811865476))

    zu = gelu(u_ref[...].astype(F32))
    vn = _layer_norm(gelu(v_ref[...].astype(F32)), g_ref[...], b_ref[...]).astype(BF16)
    lane = lax.broadcasted_iota(jnp.int32, (SG_CHUNK, LANES), 1)
    first_group = lane < (SG_WIDTH // SG_GROUPS)
    for c in range(tm // SG_CHUNK):
        rows = slice(c * SG_CHUNK, (c + 1) * SG_CHUNK)
        for pr in range(SG_WIDTH // LANES):
            cols = slice(pr * LANES, (pr + 1) * LANES)
            vp = vn[rows, cols]
            m0 = jnp.dot(w_ref[2 * pr], vp, preferred_element_type=F32)
            m1 = jnp.dot(w_ref[2 * pr + 1], vp, preferred_element_type=F32)
            mixed = jnp.where(first_group, m0, m1) + bias_ref[:, cols]
            o_ref[rows, cols] = (zu[rows, cols] * mixed).astype(BF16)


def _sg_branch(p, g, b, w_tril, bias_full):
    n = p.shape[0]
    tm = TM_BRANCH
    c0 = P_SG // SG_WIDTH
    vec = pl.BlockSpec((1, SG_WIDTH), lambda i: (0, 0))
    return pl.pallas_call(
        _sg_kernel,
        grid=(n // tm,),
        in_specs=[
            pl.BlockSpec((tm, SG_WIDTH), lambda i: (i, c0)),
            pl.BlockSpec((tm, SG_WIDTH), lambda i: (i, c0 + 1)),
            vec, vec,
            pl.BlockSpec((SG_GROUPS, SG_CHUNK, SG_CHUNK), lambda i: (0, 0, 0)),
            pl.BlockSpec((SG_CHUNK, SG_WIDTH), lambda i: (0, 0)),
        ],
        out_specs=pl.BlockSpec((tm, SG_WIDTH), lambda i: (i, 0)),
        out_shape=jax.ShapeDtypeStruct((n, SG_WIDTH), BF16),
        compiler_params=_params(32, 1),
        name="sg_branch",
    )(p, p, g, b, w_tril, bias_full)


def _attn_kernel(ti_ref, tj_ref, q_ref, k_ref, vt_ref, o_ref, sa_ref, sb_ref, m_ref, acc_ref, tri_ref,
                 *, n_below, n_diag):
    tq = sa_ref.shape[-1]
    tk = vt_ref.shape[-1]
    head_cols = [slice(hh * HEAD_PAD, (hh + 1) * HEAD_PAD) for hh in range(2)]

    def scores(t, dst_ref):
        q0 = pl.multiple_of(ti_ref[t] * tq, tq)
        k0 = pl.multiple_of(tj_ref[t] * tk, tk)
        for hh, cols in enumerate(head_cols):
            dst_ref[hh] = lax.dot_general(k_ref[pl.ds(k0, tk), cols], q_ref[pl.ds(q0, tq), cols],
                                          (((1,), (1,)), ((), ())), preferred_element_type=F32)

    def consume(t, src_ref, masked):
        i = ti_ref[t]
        j = tj_ref[t]
        probs = []
        for hh in range(2):
            s = src_ref[hh]
            if masked:
                s = s + tri_ref[...]
            m = m_ref[i, hh]
            m_new = jnp.maximum(m, jnp.max(s, axis=0, keepdims=True))
            m_ref[i, hh] = m_new
            probs.append((jnp.exp2(m - m_new), jnp.exp2(s - m_new).astype(BF16)))
        for hh, (alpha, pexp) in enumerate(probs):
            pv = jnp.dot(vt_ref[j, head_cols[hh], :], pexp, preferred_element_type=F32)
            acc_ref[i, hh] = alpha * acc_ref[i, hh] + pv

    def run(base, count, masked):
        scores(base, sa_ref)

        def two_tiles(u, carry):
            t = base + 2 * u
            scores(t + 1, sb_ref)
            consume(t, sa_ref, masked)
            scores(t + 2, sa_ref)
            consume(t + 1, sb_ref, masked)
            return carry

        lax.fori_loop(0, count // 2, two_tiles, 0)
        if count % 2:
            consume(base + count - 1, sa_ref, masked)

    tri_ref[...] = jnp.where(lax.broadcasted_iota(jnp.int32, (tk, tq), 0)
                             <= lax.broadcasted_iota(jnp.int32, (tk, tq), 1), 0.0, -jnp.inf)
    m_ref[...] = jnp.full(m_ref.shape, -jnp.inf, F32)
    acc_ref[...] = jnp.zeros(acc_ref.shape, F32)
    if n_below:
        run(0, n_below, False)
    run(n_below + 1, n_diag, True)

    row = lax.broadcasted_iota(jnp.int32, (HEAD_PAD, tq), 0)
    eye = (lax.broadcasted_iota(jnp.int32, (tq, tq), 0)
           == lax.broadcasted_iota(jnp.int32, (tq, tq), 1)).astype(BF16)
    for i in range(q_ref.shape[0] // tq):
        even = acc_ref[i, 0]
        odd = acc_ref[i, 1]
        merged = jnp.where(row < HEAD_DIM, even / even[HEAD_DIM:HEAD_DIM + 1, :], odd / odd[0:1, :])
        o_ref[i * tq:(i + 1) * tq, :] = lax.dot_general(
            eye, merged.astype(BF16), (((1,), (1,)), ((), ())),
            preferred_element_type=F32).astype(BF16)


def _attention(p, vt, batch, seq):
    n = p.shape[0]
    pairs = ATT_HEADS // 2
    assert TQ == TK
    nq = seq // TQ
    below = [(i, j) for i in range(nq) for j in range(i)]
    diag = [(i, i) for i in range(nq)]
    tiles = below + below[-1:] + diag + diag[-1:] if below else [(0, 0)] + diag + diag[-1:]
    ti = jnp.asarray([t[0] for t in tiles], jnp.int32)
    tj = jnp.asarray([t[1] for t in tiles], jnp.int32)
    pw = 2 * HEAD_PAD
    grid_spec = pltpu.PrefetchScalarGridSpec(
        num_scalar_prefetch=2,
        grid=(batch, pairs),
        in_specs=[
            pl.BlockSpec((seq, pw), lambda b, h, ti, tj: (b, P_Q // pw + h)),
            pl.BlockSpec((seq, pw), lambda b, h, ti, tj: (b, P_K // pw + h)),
            pl.BlockSpec((seq // TK, pw, TK), lambda b, h, ti, tj: (b, h, 0)),
        ],
        out_specs=pl.BlockSpec((seq, LANES), lambda b, h, ti, tj: (b, h)),
        scratch_shapes=[pltpu.VMEM((2, TK, TQ), F32), pltpu.VMEM((2, TK, TQ), F32),
                        pltpu.VMEM((nq, 2, 1, TQ), F32), pltpu.VMEM((nq, 2, HEAD_PAD, TQ), F32),
                        pltpu.VMEM((TK, TQ), F32)],
    )
    return pl.pallas_call(
        functools.partial(_attn_kernel, n_below=len(below), n_diag=len(diag)),
        grid_spec=grid_spec,
        out_shape=jax.ShapeDtypeStruct((n, ATT_WIDTH), BF16),
        compiler_params=_params(40, 2),
        name="fox_attention",
    )(ti, tj, p, p, vt)


def _merge_kernel(x_ref, hc_ref, ha_ref, hs_ref, g0_ref, g1_ref, g2_ref, bg_ref,
                  wc_ref, wa_ref, ws_ref, wm_ref, o_ref):
    merged = None
    for k, (h_ref, w_ref, gl_ref) in enumerate(((hc_ref, wc_ref, g0_ref), (ha_ref, wa_ref, g1_ref),
                                                (hs_ref, ws_ref, g2_ref))):
        y = jnp.dot(h_ref[...], w_ref[...], preferred_element_type=F32)
        gate = _sigmoid(gl_ref[...].astype(F32) + bg_ref[:, k * D_MODEL:(k + 1) * D_MODEL])
        merged = gate * y if merged is None else merged + gate * y
    o_ref[...] = x_ref[...] + jnp.dot(merged.astype(BF16), wm_ref[...], preferred_element_type=F32)


def _merge(x, hc, ha, hs, p, bg, wc, wa, ws, wm, layer):
    n = x.shape[0]
    tm = TM_BRANCH
    g0 = P_GATE // D_MODEL
    half = lambda: pl.BlockSpec((tm, CONV_CH), lambda i: (i, 0))
    wspec = lambda k: pl.BlockSpec((None, k, D_MODEL), lambda i: (layer, 0, 0))
    return pl.pallas_call(
        _merge_kernel,
        grid=(n // tm,),
        in_specs=[
            pl.BlockSpec((tm, D_MODEL), lambda i: (i, 0)),
            half(), half(), half(),
            pl.BlockSpec((tm, D_MODEL), lambda i: (i, g0)),
            pl.BlockSpec((tm, D_MODEL), lambda i: (i, g0 + 1)),
            pl.BlockSpec((tm, D_MODEL), lambda i: (i, g0 + 2)),
            pl.BlockSpec((1, N_BRANCH * D_MODEL), lambda i: (0, 0)),
            wspec(CONV_CH), wspec(ATT_WIDTH), wspec(SG_WIDTH), wspec(D_MODEL),
        ],
        out_specs=pl.BlockSpec((tm, D_MODEL), lambda i: (i, 0)),
        out_shape=jax.ShapeDtypeStruct((n, D_MODEL), F32),
        compiler_params=_params(48, 1),
        name="merge",
    )(x, hc, ha, hs, p, p, p, bg, wc, wa, ws, wm)


def _ffn_kernel(x_ref, g_ref, w1_ref, w3_ref, w2_ref, o_ref, hn_ref, acc_ref):
    f = pl.program_id(1)

    @pl.when(f == 0)
    def _():
        hn_ref[...] = _rms(x_ref[...], g_ref[...]).astype(BF16)
        acc_ref[...] = jnp.zeros_like(acc_ref)

    hn = hn_ref[...]
    a = jnp.dot(hn, w1_ref[...], preferred_element_type=F32)
    b = jnp.dot(hn, w3_ref[...], preferred_element_type=F32)
    t = (a * _sigmoid(a) * b).astype(BF16)
    acc_ref[...] += jnp.dot(t, w2_ref[...], preferred_element_type=F32)

    @pl.when(f == pl.num_programs(1) - 1)
    def _():
        o_ref[...] = x_ref[...] + acc_ref[...]


def _ffn(x, g, w1, w3, w2, layer):
    n = x.shape[0]
    tm, tf = TM_FFN, TF_FFN
    return pl.pallas_call(
        _ffn_kernel,
        grid=(n // tm, D_FF // tf),
        in_specs=[
            pl.BlockSpec((tm, D_MODEL), lambda i, f: (i, 0)),
            pl.BlockSpec((1, D_MODEL), lambda i, f: (0, 0)),
            pl.BlockSpec((None, D_MODEL, tf), lambda i, f: (layer, 0, f)),
            pl.BlockSpec((None, D_MODEL, tf), lambda i, f: (layer, 0, f)),
            pl.BlockSpec((None, tf, D_MODEL), lambda i, f: (layer, f, 0)),
        ],
        out_specs=pl.BlockSpec((tm, D_MODEL), lambda i, f: (i, 0)),
        out_shape=jax.ShapeDtypeStruct((n, D_MODEL), F32),
        scratch_shapes=[pltpu.VMEM((tm, D_MODEL), BF16), pltpu.VMEM((tm, D_MODEL), F32)],
        compiler_params=_params(48, 2),
        name="ffn_dense",
    )(x, g, w1, w3, w2)


def _router_kernel(x_ref, g_ref, wr_ref, lt_ref, h_ref, meta_ref, cnt_ref, carry_ref):
    i = pl.program_id(0)
    tm = x_ref.shape[0]

    @pl.when(i == 0)
    def _():
        carry_ref[...] = jnp.zeros_like(carry_ref)

    h = _rms(x_ref[...], g_ref[...])
    h_ref[...] = h
    logits = jnp.dot(h, wr_ref[...], preferred_element_type=F32, precision=lax.Precision.HIGHEST)
    lane = lax.broadcasted_iota(jnp.int32, (tm, LANES), 1)
    lanef = lane.astype(F32)
    lg = jnp.where(lane < N_EXPERTS, logits, -jnp.inf)
    v1 = jnp.max(lg, axis=1, keepdims=True)
    i1 = jnp.min(jnp.where(lg == v1, lanef, float(LANES)), axis=1, keepdims=True)
    lg2 = jnp.where(lanef == i1, -jnp.inf, lg)
    v2 = jnp.max(lg2, axis=1, keepdims=True)
    i2 = jnp.min(jnp.where(lg2 == v2, lanef, float(LANES)), axis=1, keepdims=True)
    e = jnp.exp(v2 - v1)
    g1 = 1.0 / (1.0 + e)
    g2 = e / (1.0 + e)
    oh1 = lanef == i1
    oh2 = lanef == i2
    cnt = (oh1.astype(F32) + oh2.astype(F32))
    before = jnp.dot(lt_ref[...], cnt.astype(BF16), preferred_element_type=F32) + carry_ref[...]
    r1 = jnp.sum(jnp.where(oh1, before, 0.0), axis=1, keepdims=True)
    r2 = jnp.sum(jnp.where(oh2, before, 0.0), axis=1, keepdims=True)
    total = carry_ref[...] + jnp.sum(cnt, axis=0, keepdims=True)
    carry_ref[...] = total
    cnt_ref[...] = jnp.broadcast_to(total, cnt_ref.shape)
    meta = jnp.zeros((tm, LANES), F32)
    for k, val in enumerate((i1, i2, g1, g2, r1, r2)):
        meta = jnp.where(lane == k, val, meta)
    meta_ref[...] = meta


def _router(x, g, wr, lt):
    n = x.shape[0]
    tm = TM_BRANCH
    return pl.pallas_call(
        _router_kernel,
        grid=(n // tm,),
        in_specs=[
            pl.BlockSpec((tm, D_MODEL), lambda i: (i, 0)),
            pl.BlockSpec((1, D_MODEL), lambda i: (0, 0)),
            pl.BlockSpec((D_MODEL, LANES), lambda i: (0, 0)),
            pl.BlockSpec((tm, tm), lambda i: (0, 0)),
        ],
        out_specs=[
            pl.BlockSpec((tm, D_MODEL), lambda i: (i, 0)),
            pl.BlockSpec((tm, LANES), lambda i: (i, 0)),
            pl.BlockSpec((8, LANES), lambda i: (0, 0)),
        ],
        out_shape=[
            jax.ShapeDtypeStruct((n, D_MODEL), F32),
            jax.ShapeDtypeStruct((n, LANES), F32),
            jax.ShapeDtypeStruct((8, LANES), F32),
        ],
        scratch_shapes=[pltpu.VMEM((1, LANES), F32)],
        compiler_params=_params(32, 1),
        name="router",
    )(x, g, wr, lt)


def _row_copy(src_ref, src_row, dst_ref, dst_row, sem):
    return pltpu.make_async_copy(src_ref.at[pl.ds(src_row, 1), :], dst_ref.at[pl.ds(dst_row, 1), :], sem)


def _dispatch_kernel(bounds_ref, dest_ref, h_ref, xs_ref, zero_ref, sem, zero_sem):
    tm = h_ref.shape[0]
    n_blk = xs_ref.shape[0] // MOE_BLK

    @pl.when(pl.program_id(0) == 0)
    def _():
        zero_ref[...] = jnp.zeros_like(zero_ref)

        def zero_block(blk):
            start = pl.multiple_of(blk * MOE_BLK, MOE_BLK)
            copy = pltpu.make_async_copy(zero_ref, xs_ref.at[pl.ds(start, MOE_BLK), :], zero_sem)
            copy.start()
            copy.wait()

        for e in range(N_EXPERTS):
            @pl.when(bounds_ref[e + 1] > bounds_ref[e])
            def _():
                zero_block(bounds_ref[e + 1] // MOE_BLK - 1)

        def unused(blk, carry):
            zero_block(blk)
            return carry

        lax.fori_loop(bounds_ref[N_EXPERTS] // MOE_BLK, n_blk, unused, 0)

    def issue(r, carry):
        for k in range(2):
            _row_copy(h_ref, r, xs_ref, dest_ref[2 * r + k], sem).start()
        return carry

    lax.fori_loop(0, tm, issue, 0, unroll=8)
    for _ in range(2):
        pltpu.make_async_copy(h_ref, xs_ref.at[pl.ds(0, tm), :], sem).wait()


def _dispatch(bounds, dest_flat, h, n_rows):
    n = h.shape[0]
    tm = TM_ROW
    return pl.pallas_call(
        _dispatch_kernel,
        grid=(n // tm,),
        in_specs=[
            pl.BlockSpec(memory_space=pltpu.SMEM),
            pl.BlockSpec((2 * tm,), lambda i: (i,), memory_space=pltpu.SMEM),
            pl.BlockSpec((tm, D_MODEL), lambda i: (i, 0)),
        ],
        out_specs=pl.BlockSpec(memory_space=pl.ANY),
        out_shape=jax.ShapeDtypeStruct((n_rows, D_MODEL), F32),
        scratch_shapes=[pltpu.VMEM((MOE_BLK, D_MODEL), F32), pltpu.SemaphoreType.DMA(()),
                        pltpu.SemaphoreType.DMA(())],
        compiler_params=_params(32, 1),
        name="moe_dispatch",
    )(bounds, dest_flat, h)


def _first_block_of_expert(b, be_ref):
    return jnp.logical_or(b == 0, be_ref[b] != be_ref[jnp.maximum(b - 1, 0)])


def _moe_kernel(be_ref, nu_ref, x_ref, w1_ref, w3_ref, w2_ref, y_ref, xb_ref, acc_ref,
                wb1_ref, wb3_ref, wb2_ref):
    b = pl.program_id(0)
    f = pl.program_id(1)

    @pl.when(b < nu_ref[0])
    def _():
        @pl.when(_first_block_of_expert(b, be_ref))
        def _():
            wb1_ref[f] = w1_ref[...].astype(BF16)
            wb3_ref[f] = w3_ref[...].astype(BF16)
            wb2_ref[f] = w2_ref[...].astype(BF16)

        @pl.when(f == 0)
        def _():
            xb_ref[...] = x_ref[...].astype(BF16)
            acc_ref[...] = jnp.zeros_like(acc_ref)

        xb = xb_ref[...]
        a = jnp.dot(xb, wb1_ref[f], preferred_element_type=F32)
        c = jnp.dot(xb, wb3_ref[f], preferred_element_type=F32)
        t = (a * _sigmoid(a) * c).astype(BF16)
        acc_ref[...] += jnp.dot(t, wb2_ref[f], preferred_element_type=F32)

        @pl.when(f == pl.num_programs(1) - 1)
        def _():
            y_ref[...] = acc_ref[...]

    @pl.when(jnp.logical_and(b >= nu_ref[0], f == pl.num_programs(1) - 1))
    def _():
        y_ref[...] = jnp.zeros_like(y_ref)


def _moe_experts(blk_e, n_used, xs, w1, w3, w2, layer):
    rows = xs.shape[0]
    n_blk = rows // MOE_BLK
    nf = D_FF_EXPERT // TF_MOE

    def row_idx(b, f, be, nu):
        return (jnp.minimum(b, nu[0] - 1), 0)

    def f_idx(b, f, be, nu):
        fetch = jnp.logical_and(b < nu[0], _first_block_of_expert(b, be))
        return jnp.where(fetch, f, nf - 1)

    grid_spec = pltpu.PrefetchScalarGridSpec(
        num_scalar_prefetch=2,
        grid=(n_blk, nf),
        in_specs=[
            pl.BlockSpec((MOE_BLK, D_MODEL), row_idx),
            pl.BlockSpec((None, None, D_MODEL, TF_MOE),
                         lambda b, f, be, nu: (layer, be[b], 0, f_idx(b, f, be, nu))),
            pl.BlockSpec((None, None, D_MODEL, TF_MOE),
                         lambda b, f, be, nu: (layer, be[b], 0, f_idx(b, f, be, nu))),
            pl.BlockSpec((None, None, TF_MOE, D_MODEL),
                         lambda b, f, be, nu: (layer, be[b], f_idx(b, f, be, nu), 0)),
        ],
        out_specs=pl.BlockSpec((MOE_BLK, D_MODEL), lambda b, f, be, nu: (b, 0)),
        scratch_shapes=[pltpu.VMEM((MOE_BLK, D_MODEL), BF16), pltpu.VMEM((MOE_BLK, D_MODEL), F32),
                        pltpu.VMEM((nf, D_MODEL, TF_MOE), BF16), pltpu.VMEM((nf, D_MODEL, TF_MOE), BF16),
                        pltpu.VMEM((nf, TF_MOE, D_MODEL), BF16)],
    )
    return pl.pallas_call(
        _moe_kernel,
        grid_spec=grid_spec,
        out_shape=jax.ShapeDtypeStruct((rows, D_MODEL), F32),
        compiler_params=_params(56, 2),
        name="moe_experts",
    )(blk_e, n_used, xs, w1, w3, w2)


def _combine_kernel(dest_ref, x_ref, meta_ref, gfin_ref, y_ref, o_ref, ybuf_ref, sem, *, final_norm):
    tm = x_ref.shape[0]

    def issue(r, carry):
        for k in range(2):
            _row_copy(y_ref, dest_ref[2 * r + k], ybuf_ref.at[k], r, sem).start()
        return carry

    lax.fori_loop(0, tm, issue, 0, unroll=8)
    for k in range(2):
        pltpu.make_async_copy(y_ref.at[pl.ds(0, tm), :], ybuf_ref.at[k], sem).wait()

    g1 = meta_ref[:, 2:3]
    g2 = meta_ref[:, 3:4]
    out = x_ref[...] + (g1 * ybuf_ref[0] + g2 * ybuf_ref[1])
    if final_norm:
        out = _rms(out, gfin_ref[...])
    o_ref[...] = out


def _combine(dest_flat, x, meta, gfin, y, final_norm):
    n = x.shape[0]
    tm = TM_ROW
    kern = functools.partial(_combine_kernel, final_norm=final_norm)
    return pl.pallas_call(
        kern,
        grid=(n // tm,),
        in_specs=[
            pl.BlockSpec((2 * tm,), lambda i: (i,), memory_space=pltpu.SMEM),
            pl.BlockSpec((tm, D_MODEL), lambda i: (i, 0)),
            pl.BlockSpec((tm, LANES), lambda i: (i, 0)),
            pl.BlockSpec((1, D_MODEL), lambda i: (0, 0)),
            pl.BlockSpec(memory_space=pl.ANY),
        ],
        out_specs=pl.BlockSpec((tm, D_MODEL), lambda i: (i, 0)),
        out_shape=jax.ShapeDtypeStruct((n, D_MODEL), F32),
        scratch_shapes=[pltpu.VMEM((2, tm, D_MODEL), F32), pltpu.SemaphoreType.DMA(())],
        compiler_params=_params(32, 1),
        name="moe_combine",
    )(dest_flat, x, meta, gfin, y)


def _moe_layer(x, g, wr, lt, w1, w3, w2, layer, gfin, final_norm):
    n = x.shape[0]
    h, meta, cnt = _router(x, g, wr, lt)
    expert = meta[:, 0:2].astype(jnp.int32)
    rank = meta[:, 4:6].astype(jnp.int32)
    counts = cnt[0, :N_EXPERTS].astype(jnp.int32)
    padded = (counts + MOE_BLK - 1) // MOE_BLK * MOE_BLK
    pad_end = jnp.cumsum(padded)
    pad_start = pad_end - padded
    dest = (pad_start[expert] + rank).reshape(-1)
    n_blk = (2 * n) // MOE_BLK + N_EXPERTS
    blk_start = jnp.arange(n_blk, dtype=jnp.int32) * MOE_BLK
    blk_e = jnp.minimum(jnp.sum(blk_start[:, None] >= pad_end[None, :], axis=1), N_EXPERTS - 1)
    n_used = (pad_end[-1:] // MOE_BLK).astype(jnp.int32)
    bounds = jnp.concatenate([jnp.zeros((1,), jnp.int32), pad_end.astype(jnp.int32)])
    xs = _dispatch(bounds, dest, h, n_blk * MOE_BLK)
    y = _moe_experts(blk_e.astype(jnp.int32), n_used, xs, w1, w3, w2, layer)
    return _combine(dest, x, meta, gfin, y, final_norm)


def _prepare_mixer(w_in, b_forget, conv_w, w_spatial, b_spatial, w_conv_out, w_att_out, w_sg_out,
                   w_mix_out):
    depth, d, _ = w_in.shape

    def pad_heads(w):
        w = w.reshape(depth, d, ATT_HEADS, HEAD_DIM)
        w = jnp.pad(w, ((0, 0), (0, 0), (0, 0), (0, HEAD_PAD - HEAD_DIM)))
        return w.reshape(depth, d, ATT_HEADS * HEAD_PAD)

    prep = {}
    prep["w_main"] = jnp.concatenate(
        [pad_heads(w_in[:, :, OFF_K:OFF_V]), w_in[:, :, OFF_CONV:OFF_Q], w_in[:, :, OFF_SG:OFF_GATE],
         w_in[:, :, OFF_GATE:], pad_heads(w_in[:, :, OFF_Q:OFF_K])], axis=2).astype(BF16)
    col = jnp.arange(P_COLS)
    is_q = col >= P_Q
    spare = (col % HEAD_PAD >= HEAD_DIM) & (col % HEAD_PAD < HEAD_DIM + N_SPLIT)
    prep["col_scale"] = jnp.where(is_q, LOG2E * HEAD_DIM ** -0.5, 1.0).astype(F32)[None, :]
    prep["col_bias"] = jnp.where(is_q & spare, 1.0, 0.0).astype(F32)[None, :]
    kcol = jnp.arange(ATT_HEADS * HEAD_PAD)
    place = -((kcol[None, None, :] // HEAD_PAD == jnp.arange(LANES)[None, :, None])
              & (kcol[None, None, :] % HEAD_PAD == HEAD_DIM + jnp.arange(N_SPLIT)[:, None, None])
              ).astype(BF16)
    prep["place"] = place.reshape(N_SPLIT * LANES, ATT_HEADS * HEAD_PAD)
    prep["wf"] = jnp.pad(w_in[:, :, OFF_F:OFF_SG], ((0, 0), (0, 0), (0, LANES - ATT_HEADS))).astype(BF16)
    prep["bf"] = jnp.pad(b_forget, ((0, 0), (0, LANES - ATT_HEADS)))[:, None, :]
    wv_t = jnp.swapaxes(w_in[:, :, OFF_V:OFF_F], 1, 2).reshape(depth, ATT_HEADS // 2, 2, HEAD_DIM, d)
    zeros_h = jnp.zeros_like(wv_t[:, :, 0])
    wvt = jnp.concatenate([wv_t[:, :, 0], zeros_h, zeros_h, wv_t[:, :, 1]], axis=2)
    prep["wvt"] = wvt.reshape(depth, ATT_HEADS * HEAD_PAD, d).astype(BF16)
    vrow = jnp.arange(ATT_HEADS * HEAD_PAD) % (2 * HEAD_PAD)
    prep["vbias"] = ((vrow == HEAD_DIM) | (vrow == HEAD_PAD)).astype(F32)[:, None]
    prep["lt_incl"] = jnp.tril(jnp.ones((LANES, LANES), F32)).astype(BF16)
    prep["conv_w"] = jnp.pad(conv_w, ((0, 0), (0, 1), (0, 0)))
    causal = jnp.tril(jnp.ones((SG_CHUNK, SG_CHUNK), bool))
    prep["w_sp"] = jnp.where(causal[None, None], w_spatial, 0).astype(BF16)
    prep["sg_bias"] = jnp.repeat(jnp.swapaxes(b_spatial, 1, 2), SG_WIDTH // SG_GROUPS, axis=2)
    for name, w in (("wc", w_conv_out), ("wa", w_att_out), ("ws", w_sg_out), ("wm", w_mix_out)):
        prep[name] = w.astype(BF16)
    return prep


def _token_mixer(xf, prep, vecs, layer, batch, seq):
    row = lambda name: vecs[name][layer][None, :]
    p, vt = _inproj(xf, row("mix_norm_g"), prep["w_main"], prep["col_scale"], prep["col_bias"],
                    prep["wf"], prep["bf"][layer], prep["lt_incl"], prep["place"],
                    prep["wvt"], prep["vbias"], seq, layer)
    hc = _conv_branch(p, prep["conv_w"][layer], row("conv_b"), row("conv_ln_g"), row("conv_ln_b"), seq)
    hs = _sg_branch(p, row("sg_ln_g"), row("sg_ln_b"), prep["w_sp"][layer], prep["sg_bias"][layer])
    ha = _attention(p, vt, batch, seq)
    return _merge(xf, hc, ha, hs, p, row("b_gate"), prep["wc"], prep["wa"], prep["ws"], prep["wm"], layer)


def kernel(x, mix_norm_g, w_in, b_forget, b_gate, conv_w, conv_b, conv_ln_g, conv_ln_b, w_conv_out,
           w_att_out, sg_ln_g, sg_ln_b, w_spatial, b_spatial, w_sg_out, w_mix_out, ffn_norm_g,
           ffn_w1, ffn_w3, ffn_w2, router_w, moe_w1, moe_w3, moe_w2, final_norm_g):
    batch, seq, d = x.shape
    depth = w_in.shape[0]
    if depth % 2 == 1:
        raise NotImplementedError("the final norm is fused into the last (expert) layer")
    xf = x.reshape(batch * seq, d)

    prep = _prepare_mixer(w_in, b_forget, conv_w, w_spatial, b_spatial, w_conv_out, w_att_out,
                          w_sg_out, w_mix_out)
    vecs = dict(mix_norm_g=mix_norm_g, conv_b=conv_b, conv_ln_g=conv_ln_g, conv_ln_b=conv_ln_b,
                sg_ln_g=sg_ln_g, sg_ln_b=sg_ln_b, b_gate=b_gate)
    f1, f3, f2 = (w.astype(BF16) for w in (ffn_w1, ffn_w3, ffn_w2))
    m1, m3, m2 = moe_w1, moe_w3, moe_w2
    wr = jnp.pad(router_w, ((0, 0), (0, 0), (0, LANES - N_EXPERTS)))
    lt = jnp.tril(jnp.ones((TM_BRANCH, TM_BRANCH), F32), -1).astype(BF16)
    gfin = final_norm_g[None, :]

    for layer in range(depth):
        xf = _token_mixer(xf, prep, vecs, layer, batch, seq)
        g_ffn = ffn_norm_g[layer][None, :]
        i = layer // 2
        if layer % 2 == 0:
            xf = _ffn(xf, g_ffn, f1, f3, f2, i)
        else:
            xf = _moe_layer(xf, g_ffn, wr[i], lt, m1, m3, m2, i, gfin,
                            final_norm=(layer == depth - 1))
    return xf.reshape(batch, seq, d)
```

```python
import functools

import jax
import jax.numpy as jnp
from jax import lax
from jax.experimental import pallas as pl
from jax.experimental.pallas import tpu as pltpu

F32 = jnp.float32
BF16 = jnp.bfloat16

D_MODEL = 1024
CONV_CH = 512
CONV_K = 31
ATT_HEADS = 8
HEAD_DIM = 64
ATT_WIDTH = ATT_HEADS * HEAD_DIM
SG_GROUPS = 8
SG_WIDTH = 512
SG_CHUNK = 128
N_BRANCH = 3
OFF_CONV = 0
OFF_Q = OFF_CONV + 2 * CONV_CH
OFF_K = OFF_Q + ATT_WIDTH
OFF_V = OFF_K + ATT_WIDTH
OFF_F = OFF_V + ATT_WIDTH
OFF_SG = OFF_F + ATT_HEADS
OFF_GATE = OFF_SG + 2 * SG_WIDTH
D_FF = 2816
N_EXPERTS = 8
D_FF_EXPERT = 3584
EPS = 1e-6

LANES = 128
SUBLANES = 8
MIB = 1024 * 1024

HEAD_PAD = LANES
P_K = 0
P_CONV = P_K + ATT_HEADS * HEAD_PAD
P_SG = P_CONV + 2 * CONV_CH
P_GATE = P_SG + 2 * SG_WIDTH
P_Q = P_GATE + N_BRANCH * D_MODEL
P_COLS = P_Q + ATT_HEADS * HEAD_PAD
N_SPLIT = 3
LOG2E = 1.4426950408889634

TM_PROJ = 1024
TN_PROJ = 1024
TM_BRANCH = 512
CONV_HIST = 32
CONV_RB = 64
TQ = 512
TK = 512
ATT_UNROLL = 14
TM_FFN = 1024
TF_FFN = 256
MOE_BLK = 512
TF_MOE = 512
TM_ROW = 256


def _params(vmem_mib, n_axes):
    return pltpu.CompilerParams(dimension_semantics=("arbitrary",) * n_axes,
                                vmem_limit_bytes=vmem_mib * MIB)


def _rms(x, g):
    return x * lax.rsqrt(jnp.mean(x * x, axis=-1, keepdims=True) + EPS) * g


def _layer_norm(x, g, b):
    mu = jnp.mean(x, axis=-1, keepdims=True)
    d = x - mu
    var = jnp.mean(d * d, axis=-1, keepdims=True)
    return d * lax.rsqrt(var + EPS) * g + b


def _sigmoid(x):
    return 1.0 / (1.0 + jnp.exp(-x))


def _split3(x):
    hi = x.astype(BF16)
    r1 = x - hi.astype(F32)
    mid = r1.astype(BF16)
    lo = (r1 - mid.astype(F32)).astype(BF16)
    return hi, mid, lo


def _inproj_kernel(x_ref, g_ref, w_ref, sc_ref, bi_ref, wf_ref, bf_ref, lt_ref, pl_ref, wvt_ref,
                   vb_ref, p_ref, vt_ref, xn_ref, c3_ref, carry_ref, *, tiles_per_batch):
    i = pl.program_id(0)
    j = pl.program_id(1)
    tm = x_ref.shape[0]
    tk = vt_ref.shape[-1]

    @pl.when(jnp.logical_and(j == 0, i % tiles_per_batch == 0))
    def _():
        carry_ref[...] = jnp.zeros_like(carry_ref)

    @pl.when(j == 0)
    def _():
        xn = _rms(x_ref[...], g_ref[...]).astype(BF16)
        xn_ref[...] = xn
        f = jnp.dot(xn, wf_ref[...], preferred_element_type=F32) + bf_ref[...]
        ls = jnp.minimum(f, 0.0) - jnp.log1p(jnp.exp(-jnp.abs(f)))
        lt = lt_ref[...]
        terms = _split3(ls)
        offset = carry_ref[...]
        for blk in range(tm // LANES):
            rows = slice(blk * LANES, (blk + 1) * LANES)
            c = sum(jnp.dot(lt, term[rows], preferred_element_type=F32) for term in terms) + offset
            offset = c[LANES - 1:LANES, :]
            for r, term in enumerate(_split3(c * LOG2E)):
                c3_ref[rows, r * LANES:(r + 1) * LANES] = term
        carry_ref[...] = offset
        vt = lax.dot_general(wvt_ref[...], xn, (((1,), (1,)), ((), ())),
                             preferred_element_type=F32) + vb_ref[...]
        for ch in range(tm // tk):
            vt_ref[ch] = vt[:, ch * tk:(ch + 1) * tk].astype(BF16)
        decay = jnp.dot(c3_ref[...], pl_ref[...], preferred_element_type=F32)
        p_ref[...] = (jnp.dot(xn, w_ref[...], preferred_element_type=F32) + decay).astype(BF16)

    @pl.when(j > 0)
    def _():
        acc = jnp.dot(xn_ref[...], w_ref[...], preferred_element_type=F32)
        p_ref[...] = (acc * sc_ref[...] + bi_ref[...]).astype(BF16)


def _inproj(x, g, w, col_scale, col_bias, wf, bf, lt, place, wvt, vbias, seq, layer):
    n = x.shape[0]
    tm, tn = TM_PROJ, TN_PROJ
    assert P_K == 0 and tn == ATT_HEADS * HEAD_PAD
    kern = functools.partial(_inproj_kernel, tiles_per_batch=seq // tm)
    vt_rows = ATT_HEADS * HEAD_PAD
    const = lambda shape: pl.BlockSpec(shape, lambda i, j: (0,) * len(shape))
    return pl.pallas_call(
        kern,
        grid=(n // tm, P_COLS // tn),
        in_specs=[
            pl.BlockSpec((tm, D_MODEL), lambda i, j: (i, 0)),
            const((1, D_MODEL)),
            pl.BlockSpec((None, D_MODEL, tn), lambda i, j: (layer, 0, j)),
            pl.BlockSpec((1, tn), lambda i, j: (0, j)),
            pl.BlockSpec((1, tn), lambda i, j: (0, j)),
            pl.BlockSpec((None, D_MODEL, LANES), lambda i, j: (layer, 0, 0)),
            const((1, LANES)),
            const((LANES, LANES)),
            const((N_SPLIT * LANES, tn)),
            pl.BlockSpec((None, vt_rows, D_MODEL), lambda i, j: (layer, 0, 0)),
            const((vt_rows, 1)),
        ],
        out_specs=[
            pl.BlockSpec((tm, tn), lambda i, j: (i, j)),
            pl.BlockSpec((tm // TK, vt_rows, TK), lambda i, j: (i, 0, 0)),
        ],
        out_shape=[
            jax.ShapeDtypeStruct((n, P_COLS), BF16),
            jax.ShapeDtypeStruct((n // TK, vt_rows, TK), BF16),
        ],
        scratch_shapes=[pltpu.VMEM((tm, D_MODEL), BF16), pltpu.VMEM((tm, N_SPLIT * LANES), BF16),
                        pltpu.VMEM((1, LANES), F32)],
        compiler_params=_params(48, 2),
        name="inproj",
    )(x, g, w, col_scale, col_bias, wf, bf, lt, place, wvt, vbias)


def _conv_kernel(a1_ref, a2_ref, w_ref, cb_ref, g_ref, b_ref, o_ref, hext_ref, hsh_ref,
                 *, tiles_per_batch):
    i = pl.program_id(0)
    tm = a1_ref.shape[0]
    span = hsh_ref.shape[1]

    @pl.when(i % tiles_per_batch == 0)
    def _():
        hext_ref[0:CONV_HIST, :] = jnp.zeros((CONV_HIST, CONV_CH), F32)

    @pl.when(i % tiles_per_batch != 0)
    def _():
        hext_ref[0:CONV_HIST, :] = hext_ref[tm:tm + CONV_HIST, :]

    hext_ref[CONV_HIST:CONV_HIST + tm, :] = (
        a1_ref[...].astype(F32) * _sigmoid(a2_ref[...].astype(F32)))
    for s in range(1, SUBLANES):
        hsh_ref[s] = hext_ref[s:s + span, :]

    g = g_ref[...]
    b = b_ref[...]
    base = CONV_HIST - (CONV_K - 1)
    for r in range(0, tm, CONV_RB):
        acc = jnp.broadcast_to(cb_ref[...], (CONV_RB, CONV_CH))
        for j in range(CONV_K):
            s = (base + j) % SUBLANES
            a = r + base + j - s
            rows = hext_ref[a:a + CONV_RB, :] if s == 0 else hsh_ref[s, a:a + CONV_RB, :]
            acc = acc + w_ref[j:j + 1, :] * rows
        y = _layer_norm(acc, g, b)
        o_ref[r:r + CONV_RB, :] = (y * _sigmoid(y)).astype(BF16)


def _conv_branch(p, w, cb, g, b, seq):
    n = p.shape[0]
    tm = TM_BRANCH
    kern = functools.partial(_conv_kernel, tiles_per_batch=seq // tm)
    c0 = P_CONV // CONV_CH
    vec = pl.BlockSpec((1, CONV_CH), lambda i: (0, 0))
    return pl.pallas_call(
        kern,
        grid=(n // tm,),
        in_specs=[
            pl.BlockSpec((tm, CONV_CH), lambda i: (i, c0)),
            pl.BlockSpec((tm, CONV_CH), lambda i: (i, c0 + 1)),
            pl.BlockSpec((CONV_K + 1, CONV_CH), lambda i: (0, 0)),
            vec, vec, vec,
        ],
        out_specs=pl.BlockSpec((tm, CONV_CH), lambda i: (i, 0)),
        out_shape=jax.ShapeDtypeStruct((n, CONV_CH), BF16),
        scratch_shapes=[pltpu.VMEM((tm + CONV_HIST, CONV_CH), F32),
                        pltpu.VMEM((SUBLANES, tm + CONV_HIST - SUBLANES, CONV_CH), F32)],
        compiler_params=_params(32, 1),
        name="conv_branch",
    )(p, p, w, cb, g, b)


def _sg_kernel(u_ref, v_ref, g_ref, b_ref, w_ref, bias_ref, o_ref):
    tm = u_ref.shape[0]

    def gelu(z):
        return 0.5 * z * (1.0 + lax.erf(z * 0.7071067811865476))

    zu = gelu(u_ref[...].astype(F32))
    vn = _layer_norm(gelu(v_ref[...].astype(F32)), g_ref[...], b_ref[...]).astype(BF16)
    lane = lax.broadcasted_iota(jnp.int32, (SG_CHUNK, LANES), 1)
    first_group = lane < (SG_WIDTH // SG_GROUPS)
    for c in range(tm // SG_CHUNK):
        rows = slice(c * SG_CHUNK, (c + 1) * SG_CHUNK)
        for pr in range(SG_WIDTH // LANES):
            cols = slice(pr * LANES, (pr + 1) * LANES)
            vp = vn[rows, cols]
            m0 = jnp.dot(w_ref[2 * pr], vp, preferred_element_type=F32)
            m1 = jnp.dot(w_ref[2 * pr + 1], vp, preferred_element_type=F32)
            mixed = jnp.where(first_group, m0, m1) + bias_ref[:, cols]
            o_ref[rows, cols] = (zu[rows, cols] * mixed).astype(BF16)


def _sg_branch(p, g, b, w_tril, bias_full):
    n = p.shape[0]
    tm = TM_BRANCH
    c0 = P_SG // SG_WIDTH
    vec = pl.BlockSpec((1, SG_WIDTH), lambda i: (0, 0))
    return pl.pallas_call(
        _sg_kernel,
        grid=(n // tm,),
        in_specs=[
            pl.BlockSpec((tm, SG_WIDTH), lambda i: (i, c0)),
            pl.BlockSpec((tm, SG_WIDTH), lambda i: (i, c0 + 1)),
            vec, vec,
            pl.BlockSpec((SG_GROUPS, SG_CHUNK, SG_CHUNK), lambda i: (0, 0, 0)),
            pl.BlockSpec((SG_CHUNK, SG_WIDTH), lambda i: (0, 0)),
        ],
        out_specs=pl.BlockSpec((tm, SG_WIDTH), lambda i: (i, 0)),
        out_shape=jax.ShapeDtypeStruct((n, SG_WIDTH), BF16),
        compiler_params=_params(32, 1),
        name="sg_branch",
    )(p, p, g, b, w_tril, bias_full)


def _attn_kernel(ti_ref, tj_ref, q_ref, k_ref, vt_ref, o_ref, sa_ref, sb_ref, m_ref, acc_ref, tri_ref,
                 *, n_below, n_diag):
    tq = sa_ref.shape[-1]
    tk = vt_ref.shape[-1]
    head_cols = [slice(hh * HEAD_PAD, (hh + 1) * HEAD_PAD) for hh in range(2)]

    def scores(t, dst_ref):
        q0 = pl.multiple_of(ti_ref[t] * tq, tq)
        k0 = pl.multiple_of(tj_ref[t] * tk, tk)
        for hh, cols in enumerate(head_cols):
            dst_ref[hh] = lax.dot_general(k_ref[pl.ds(k0, tk), cols], q_ref[pl.ds(q0, tq), cols],
                                          (((1,), (1,)), ((), ())), preferred_element_type=F32)

    def consume(t, src_ref, masked):
        i = ti_ref[t]
        j = tj_ref[t]
        probs = []
        for hh in range(2):
            s = src_ref[hh]
            if masked:
                s = s + tri_ref[...]
            m = m_ref[i, hh]
            m_new = jnp.maximum(m, jnp.max(s, axis=0, keepdims=True))
            m_ref[i, hh] = m_new
            probs.append((jnp.exp2(m - m_new), jnp.exp2(s - m_new).astype(BF16)))
        for hh, (alpha, pexp) in enumerate(probs):
            pv = jnp.dot(vt_ref[j, head_cols[hh], :], pexp, preferred_element_type=F32)
            acc_ref[i, hh] = alpha * acc_ref[i, hh] + pv

    def run(base, count, masked):
        bufs = (sa_ref, sb_ref)
        scores(base, bufs[0])

        def trip(u, carry):
            t = base + ATT_UNROLL * u
            for k in range(ATT_UNROLL):
                scores(t + k + 1, bufs[(k + 1) % 2])
                consume(t + k, bufs[k % 2], masked)
            return carry

        lax.fori_loop(0, count // ATT_UNROLL, trip, 0)
        done = count - count % ATT_UNROLL
        for k in range(done, count):
            if k + 1 < count:
                scores(base + k + 1, bufs[(k + 1) % 2])
            consume(base + k, bufs[k % 2], masked)

    tri_ref[...] = jnp.where(lax.broadcasted_iota(jnp.int32, (tk, tq), 0)
                             <= lax.broadcasted_iota(jnp.int32, (tk, tq), 1), 0.0, -jnp.inf)
    m_ref[...] = jnp.full(m_ref.shape, -jnp.inf, F32)
    acc_ref[...] = jnp.zeros(acc_ref.shape, F32)
    if n_below:
        run(0, n_below, False)
    run(n_below + 1, n_diag, True)

    row = lax.broadcasted_iota(jnp.int32, (HEAD_PAD, tq), 0)
    eye = (lax.broadcasted_iota(jnp.int32, (tq, tq), 0)
           == lax.broadcasted_iota(jnp.int32, (tq, tq), 1)).astype(BF16)
    for i in range(q_ref.shape[0] // tq):
        even = acc_ref[i, 0]
        odd = acc_ref[i, 1]
        merged = jnp.where(row < HEAD_DIM, even / even[HEAD_DIM:HEAD_DIM + 1, :], odd / odd[0:1, :])
        o_ref[i * tq:(i + 1) * tq, :] = lax.dot_general(
            eye, merged.astype(BF16), (((1,), (1,)), ((), ())),
            preferred_element_type=F32).astype(BF16)


def _attention(p, vt, batch, seq):
    n = p.shape[0]
    pairs = ATT_HEADS // 2
    assert TQ == TK
    nq = seq // TQ
    below = [(i, j) for i in range(nq) for j in range(i)]
    diag = [(i, i) for i in range(nq)]
    tiles = below + below[-1:] + diag + diag[-1:] if below else [(0, 0)] + diag + diag[-1:]
    ti = jnp.asarray([t[0] for t in tiles], jnp.int32)
    tj = jnp.asarray([t[1] for t in tiles], jnp.int32)
    pw = 2 * HEAD_PAD
    grid_spec = pltpu.PrefetchScalarGridSpec(
        num_scalar_prefetch=2,
        grid=(batch, pairs),
        in_specs=[
            pl.BlockSpec((seq, pw), lambda b, h, ti, tj: (b, P_Q // pw + h)),
            pl.BlockSpec((seq, pw), lambda b, h, ti, tj: (b, P_K // pw + h)),
            pl.BlockSpec((seq // TK, pw, TK), lambda b, h, ti, tj: (b, h, 0)),
        ],
        out_specs=pl.BlockSpec((seq, LANES), lambda b, h, ti, tj: (b, h)),
        scratch_shapes=[pltpu.VMEM((2, TK, TQ), F32), pltpu.VMEM((2, TK, TQ), F32),
                        pltpu.VMEM((nq, 2, 1, TQ), F32), pltpu.VMEM((nq, 2, HEAD_PAD, TQ), F32),
                        pltpu.VMEM((TK, TQ), F32)],
    )
    return pl.pallas_call(
        functools.partial(_attn_kernel, n_below=len(below), n_diag=len(diag)),
        grid_spec=grid_spec,
        out_shape=jax.ShapeDtypeStruct((n, ATT_WIDTH), BF16),
        compiler_params=_params(40, 2),
        name="fox_attention",
    )(ti, tj, p, p, vt)


def _merge_kernel(x_ref, hc_ref, ha_ref, hs_ref, g0_ref, g1_ref, g2_ref, bg_ref,
                  wc_ref, wa_ref, ws_ref, wm_ref, o_ref):
    merged = None
    for k, (h_ref, w_ref, gl_ref) in enumerate(((hc_ref, wc_ref, g0_ref), (ha_ref, wa_ref, g1_ref),
                                                (hs_ref, ws_ref, g2_ref))):
        y = jnp.dot(h_ref[...], w_ref[...], preferred_element_type=F32)
        gate = _sigmoid(gl_ref[...].astype(F32) + bg_ref[:, k * D_MODEL:(k + 1) * D_MODEL])
        merged = gate * y if merged is None else merged + gate * y
    o_ref[...] = x_ref[...] + jnp.dot(merged.astype(BF16), wm_ref[...], preferred_element_type=F32)


def _merge(x, hc, ha, hs, p, bg, wc, wa, ws, wm, layer):
    n = x.shape[0]
    tm = TM_BRANCH
    g0 = P_GATE // D_MODEL
    half = lambda: pl.BlockSpec((tm, CONV_CH), lambda i: (i, 0))
    wspec = lambda k: pl.BlockSpec((None, k, D_MODEL), lambda i: (layer, 0, 0))
    return pl.pallas_call(
        _merge_kernel,
        grid=(n // tm,),
        in_specs=[
            pl.BlockSpec((tm, D_MODEL), lambda i: (i, 0)),
            half(), half(), half(),
            pl.BlockSpec((tm, D_MODEL), lambda i: (i, g0)),
            pl.BlockSpec((tm, D_MODEL), lambda i: (i, g0 + 1)),
            pl.BlockSpec((tm, D_MODEL), lambda i: (i, g0 + 2)),
            pl.BlockSpec((1, N_BRANCH * D_MODEL), lambda i: (0, 0)),
            wspec(CONV_CH), wspec(ATT_WIDTH), wspec(SG_WIDTH), wspec(D_MODEL),
        ],
        out_specs=pl.BlockSpec((tm, D_MODEL), lambda i: (i, 0)),
        out_shape=jax.ShapeDtypeStruct((n, D_MODEL), F32),
        compiler_params=_params(48, 1),
        name="merge",
    )(x, hc, ha, hs, p, p, p, bg, wc, wa, ws, wm)


def _ffn_kernel(x_ref, g_ref, w1_ref, w3_ref, w2_ref, o_ref, hn_ref, acc_ref,
                wb1_ref, wb3_ref, wb2_ref):
    f = pl.program_id(1)

    @pl.when(pl.program_id(0) == 0)
    def _():
        wb1_ref[f] = w1_ref[...].astype(BF16)
        wb3_ref[f] = w3_ref[...].astype(BF16)
        wb2_ref[f] = w2_ref[...].astype(BF16)

    @pl.when(f == 0)
    def _():
        hn_ref[...] = _rms(x_ref[...], g_ref[...]).astype(BF16)
        acc_ref[...] = jnp.zeros_like(acc_ref)

    hn = hn_ref[...]
    a = jnp.dot(hn, wb1_ref[f], preferred_element_type=F32)
    b = jnp.dot(hn, wb3_ref[f], preferred_element_type=F32)
    t = (a * _sigmoid(a) * b).astype(BF16)
    acc_ref[...] += jnp.dot(t, wb2_ref[f], preferred_element_type=F32)

    @pl.when(f == pl.num_programs(1) - 1)
    def _():
        o_ref[...] = x_ref[...] + acc_ref[...]


def _ffn(x, g, w1, w3, w2, layer):
    n = x.shape[0]
    tm, tf = TM_FFN, TF_FFN
    nf = D_FF // tf

    def f_idx(i, f):
        return jnp.where(i == 0, f, nf - 1)

    return pl.pallas_call(
        _ffn_kernel,
        grid=(n // tm, nf),
        in_specs=[
            pl.BlockSpec((tm, D_MODEL), lambda i, f: (i, 0)),
            pl.BlockSpec((1, D_MODEL), lambda i, f: (0, 0)),
            pl.BlockSpec((None, D_MODEL, tf), lambda i, f: (layer, 0, f_idx(i, f))),
            pl.BlockSpec((None, D_MODEL, tf), lambda i, f: (layer, 0, f_idx(i, f))),
            pl.BlockSpec((None, tf, D_MODEL), lambda i, f: (layer, f_idx(i, f), 0)),
        ],
        out_specs=pl.BlockSpec((tm, D_MODEL), lambda i, f: (i, 0)),
        out_shape=jax.ShapeDtypeStruct((n, D_MODEL), F32),
        scratch_shapes=[pltpu.VMEM((tm, D_MODEL), BF16), pltpu.VMEM((tm, D_MODEL), F32),
                        pltpu.VMEM((nf, D_MODEL, tf), BF16), pltpu.VMEM((nf, D_MODEL, tf), BF16),
                        pltpu.VMEM((nf, tf, D_MODEL), BF16)],
        compiler_params=_params(56, 2),
        name="ffn_dense",
    )(x, g, w1, w3, w2)


def _router_kernel(x_ref, g_ref, wr_ref, lt_ref, h_ref, meta_ref, cnt_ref, carry_ref):
    i = pl.program_id(0)
    tm = x_ref.shape[0]

    @pl.when(i == 0)
    def _():
        carry_ref[...] = jnp.zeros_like(carry_ref)

    h = _rms(x_ref[...], g_ref[...])
    h_ref[...] = h
    h_hi = h.astype(BF16)
    h_lo = (h - h_hi.astype(F32)).astype(BF16)
    w = wr_ref[...]
    w_hi = w.astype(BF16)
    w_lo = (w - w_hi.astype(F32)).astype(BF16)
    logits = (jnp.dot(h_hi, w_hi, preferred_element_type=F32)
              + jnp.dot(h_hi, w_lo, preferred_element_type=F32)
              + jnp.dot(h_lo, w_hi, preferred_element_type=F32))
    lane = lax.broadcasted_iota(jnp.int32, (tm, LANES), 1)
    lanef = lane.astype(F32)
    lg = jnp.where(lane < N_EXPERTS, logits, -jnp.inf)
    v1 = jnp.max(lg, axis=1, keepdims=True)
    i1 = jnp.min(jnp.where(lg == v1, lanef, float(LANES)), axis=1, keepdims=True)
    lg2 = jnp.where(lanef == i1, -jnp.inf, lg)
    v2 = jnp.max(lg2, axis=1, keepdims=True)
    i2 = jnp.min(jnp.where(lg2 == v2, lanef, float(LANES)), axis=1, keepdims=True)
    e = jnp.exp(v2 - v1)
    g1 = 1.0 / (1.0 + e)
    g2 = e / (1.0 + e)
    oh1 = lanef == i1
    oh2 = lanef == i2
    cnt = (oh1.astype(F32) + oh2.astype(F32))
    before = jnp.dot(lt_ref[...], cnt.astype(BF16), preferred_element_type=F32) + carry_ref[...]
    r1 = jnp.sum(jnp.where(oh1, before, 0.0), axis=1, keepdims=True)
    r2 = jnp.sum(jnp.where(oh2, before, 0.0), axis=1, keepdims=True)
    total = carry_ref[...] + jnp.sum(cnt, axis=0, keepdims=True)
    carry_ref[...] = total
    cnt_ref[...] = jnp.broadcast_to(total, cnt_ref.shape)
    meta = jnp.zeros((tm, LANES), F32)
    for k, val in enumerate((i1, i2, g1, g2, r1, r2)):
        meta = jnp.where(lane == k, val, meta)
    meta_ref[...] = meta


def _router(x, g, wr, lt):
    n = x.shape[0]
    tm = TM_BRANCH
    return pl.pallas_call(
        _router_kernel,
        grid=(n // tm,),
        in_specs=[
            pl.BlockSpec((tm, D_MODEL), lambda i: (i, 0)),
            pl.BlockSpec((1, D_MODEL), lambda i: (0, 0)),
            pl.BlockSpec((D_MODEL, LANES), lambda i: (0, 0)),
            pl.BlockSpec((tm, tm), lambda i: (0, 0)),
        ],
        out_specs=[
            pl.BlockSpec((tm, D_MODEL), lambda i: (i, 0)),
            pl.BlockSpec((tm, LANES), lambda i: (i, 0)),
            pl.BlockSpec((8, LANES), lambda i: (0, 0)),
        ],
        out_shape=[
            jax.ShapeDtypeStruct((n, D_MODEL), F32),
            jax.ShapeDtypeStruct((n, LANES), F32),
            jax.ShapeDtypeStruct((8, LANES), F32),
        ],
        scratch_shapes=[pltpu.VMEM((1, LANES), F32)],
        compiler_params=_params(32, 1),
        name="router",
    )(x, g, wr, lt)


def _row_copy(src_ref, src_row, dst_ref, dst_row, sem):
    return pltpu.make_async_copy(src_ref.at[pl.ds(src_row, 1), :], dst_ref.at[pl.ds(dst_row, 1), :], sem)


def _dispatch_kernel(bounds_ref, dest_ref, h_ref, xs_ref, zero_ref, sem, zero_sem):
    tm = h_ref.shape[0]
    n_blk = xs_ref.shape[0] // MOE_BLK

    @pl.when(pl.program_id(0) == 0)
    def _():
        zero_ref[...] = jnp.zeros_like(zero_ref)

        def zero_block(blk):
            start = pl.multiple_of(blk * MOE_BLK, MOE_BLK)
            copy = pltpu.make_async_copy(zero_ref, xs_ref.at[pl.ds(start, MOE_BLK), :], zero_sem)
            copy.start()
            copy.wait()

        for e in range(N_EXPERTS):
            @pl.when(bounds_ref[e + 1] > bounds_ref[e])
            def _():
                zero_block(bounds_ref[e + 1] // MOE_BLK - 1)

        def unused(blk, carry):
            zero_block(blk)
            return carry

        lax.fori_loop(bounds_ref[N_EXPERTS] // MOE_BLK, n_blk, unused, 0)

    def issue(r, carry):
        for k in range(2):
            _row_copy(h_ref, r, xs_ref, dest_ref[2 * r + k], sem).start()
        return carry

    lax.fori_loop(0, tm, issue, 0, unroll=8)
    for _ in range(2):
        pltpu.make_async_copy(h_ref, xs_ref.at[pl.ds(0, tm), :], sem).wait()


def _dispatch(bounds, dest_flat, h, n_rows):
    n = h.shape[0]
    tm = TM_ROW
    return pl.pallas_call(
        _dispatch_kernel,
        grid=(n // tm,),
        in_specs=[
            pl.BlockSpec(memory_space=pltpu.SMEM),
            pl.BlockSpec((2 * tm,), lambda i: (i,), memory_space=pltpu.SMEM),
            pl.BlockSpec((tm, D_MODEL), lambda i: (i, 0)),
        ],
        out_specs=pl.BlockSpec(memory_space=pl.ANY),
        out_shape=jax.ShapeDtypeStruct((n_rows, D_MODEL), F32),
        scratch_shapes=[pltpu.VMEM((MOE_BLK, D_MODEL), F32), pltpu.SemaphoreType.DMA(()),
                        pltpu.SemaphoreType.DMA(())],
        compiler_params=_params(32, 1),
        name="moe_dispatch",
    )(bounds, dest_flat, h)


def _first_block_of_expert(b, be_ref):
    return jnp.logical_or(b == 0, be_ref[b] != be_ref[jnp.maximum(b - 1, 0)])


def _moe_kernel(be_ref, nu_ref, x_ref, w1_ref, w3_ref, w2_ref, y_ref, xb_ref, acc_ref,
                wb1_ref, wb3_ref, wb2_ref):
    b = pl.program_id(0)
    f = pl.program_id(1)

    @pl.when(b < nu_ref[0])
    def _():
        @pl.when(_first_block_of_expert(b, be_ref))
        def _():
            wb1_ref[f] = w1_ref[...].astype(BF16)
            wb3_ref[f] = w3_ref[...].astype(BF16)
            wb2_ref[f] = w2_ref[...].astype(BF16)

        @pl.when(f == 0)
        def _():
            xb_ref[...] = x_ref[...].astype(BF16)
            acc_ref[...] = jnp.zeros_like(acc_ref)

        xb = xb_ref[...]
        a = jnp.dot(xb, wb1_ref[f], preferred_element_type=F32)
        c = jnp.dot(xb, wb3_ref[f], preferred_element_type=F32)
        t = (a * _sigmoid(a) * c).astype(BF16)
        acc_ref[...] += jnp.dot(t, wb2_ref[f], preferred_element_type=F32)

        @pl.when(f == pl.num_programs(1) - 1)
        def _():
            y_ref[...] = acc_ref[...]

    @pl.when(jnp.logical_and(b >= nu_ref[0], f == pl.num_programs(1) - 1))
    def _():
        y_ref[...] = jnp.zeros_like(y_ref)


def _moe_experts(blk_e, n_used, xs, w1, w3, w2, layer):
    rows = xs.shape[0]
    n_blk = rows // MOE_BLK
    nf = D_FF_EXPERT // TF_MOE

    def row_idx(b, f, be, nu):
        return (jnp.minimum(b, nu[0] - 1), 0)

    def f_idx(b, f, be, nu):
        fetch = jnp.logical_and(b < nu[0], _first_block_of_expert(b, be))
        return jnp.where(fetch, f, nf - 1)

    grid_spec = pltpu.PrefetchScalarGridSpec(
        num_scalar_prefetch=2,
        grid=(n_blk, nf),
        in_specs=[
            pl.BlockSpec((MOE_BLK, D_MODEL), row_idx),
            pl.BlockSpec((None, None, D_MODEL, TF_MOE),
                         lambda b, f, be, nu: (layer, be[b], 0, f_idx(b, f, be, nu))),
            pl.BlockSpec((None, None, D_MODEL, TF_MOE),
                         lambda b, f, be, nu: (layer, be[b], 0, f_idx(b, f, be, nu))),
            pl.BlockSpec((None, None, TF_MOE, D_MODEL),
                         lambda b, f, be, nu: (layer, be[b], f_idx(b, f, be, nu), 0)),
        ],
        out_specs=pl.BlockSpec((MOE_BLK, D_MODEL), lambda b, f, be, nu: (b, 0)),
        scratch_shapes=[pltpu.VMEM((MOE_BLK, D_MODEL), BF16), pltpu.VMEM((MOE_BLK, D_MODEL), F32),
                        pltpu.VMEM((nf, D_MODEL, TF_MOE), BF16), pltpu.VMEM((nf, D_MODEL, TF_MOE), BF16),
                        pltpu.VMEM((nf, TF_MOE, D_MODEL), BF16)],
    )
    return pl.pallas_call(
        _moe_kernel,
        grid_spec=grid_spec,
        out_shape=jax.ShapeDtypeStruct((rows, D_MODEL), F32),
        compiler_params=_params(56, 2),
        name="moe_experts",
    )(blk_e, n_used, xs, w1, w3, w2)


def _combine_kernel(dest_ref, x_ref, meta_ref, gfin_ref, y_ref, o_ref, ybuf_ref, sem, *, final_norm):
    tm = x_ref.shape[0]

    def issue(r, carry):
        for k in range(2):
            _row_copy(y_ref, dest_ref[2 * r + k], ybuf_ref.at[k], r, sem).start()
        return carry

    lax.fori_loop(0, tm, issue, 0, unroll=8)
    for k in range(2):
        pltpu.make_async_copy(y_ref.at[pl.ds(0, tm), :], ybuf_ref.at[k], sem).wait()

    g1 = meta_ref[:, 2:3]
    g2 = meta_ref[:, 3:4]
    out = x_ref[...] + (g1 * ybuf_ref[0] + g2 * ybuf_ref[1])
    if final_norm:
        out = _rms(out, gfin_ref[...])
    o_ref[...] = out


def _combine(dest_flat, x, meta, gfin, y, final_norm):
    n = x.shape[0]
    tm = TM_ROW
    kern = functools.partial(_combine_kernel, final_norm=final_norm)
    return pl.pallas_call(
        kern,
        grid=(n // tm,),
        in_specs=[
            pl.BlockSpec((2 * tm,), lambda i: (i,), memory_space=pltpu.SMEM),
            pl.BlockSpec((tm, D_MODEL), lambda i: (i, 0)),
            pl.BlockSpec((tm, LANES), lambda i: (i, 0)),
            pl.BlockSpec((1, D_MODEL), lambda i: (0, 0)),
            pl.BlockSpec(memory_space=pl.ANY),
        ],
        out_specs=pl.BlockSpec((tm, D_MODEL), lambda i: (i, 0)),
        out_shape=jax.ShapeDtypeStruct((n, D_MODEL), F32),
        scratch_shapes=[pltpu.VMEM((2, tm, D_MODEL), F32), pltpu.SemaphoreType.DMA(())],
        compiler_params=_params(32, 1),
        name="moe_combine",
    )(dest_flat, x, meta, gfin, y)


def _moe_layer(x, g, wr, lt, w1, w3, w2, layer, gfin, final_norm):
    n = x.shape[0]
    h, meta, cnt = _router(x, g, wr, lt)
    expert = meta[:, 0:2].astype(jnp.int32)
    rank = meta[:, 4:6].astype(jnp.int32)
    counts = cnt[0, :N_EXPERTS].astype(jnp.int32)
    padded = (counts + MOE_BLK - 1) // MOE_BLK * MOE_BLK
    pad_end = jnp.cumsum(padded)
    pad_start = pad_end - padded
    dest = (pad_start[expert] + rank).reshape(-1)
    n_blk = (2 * n) // MOE_BLK + N_EXPERTS
    blk_start = jnp.arange(n_blk, dtype=jnp.int32) * MOE_BLK
    blk_e = jnp.minimum(jnp.sum(blk_start[:, None] >= pad_end[None, :], axis=1), N_EXPERTS - 1)
    n_used = (pad_end[-1:] // MOE_BLK).astype(jnp.int32)
    bounds = jnp.concatenate([jnp.zeros((1,), jnp.int32), pad_end.astype(jnp.int32)])
    xs = _dispatch(bounds, dest, h, n_blk * MOE_BLK)
    y = _moe_experts(blk_e.astype(jnp.int32), n_used, xs, w1, w3, w2, layer)
    return _combine(dest, x, meta, gfin, y, final_norm)


def _prepare_mixer(w_in, b_forget, conv_w, w_spatial, b_spatial, w_conv_out, w_att_out, w_sg_out,
                   w_mix_out):
    depth, d, _ = w_in.shape

    def pad_heads(w):
        w = w.reshape(depth, d, ATT_HEADS, HEAD_DIM)
        w = jnp.pad(w, ((0, 0), (0, 0), (0, 0), (0, HEAD_PAD - HEAD_DIM)))
        return w.reshape(depth, d, ATT_HEADS * HEAD_PAD)

    prep = {}
    prep["w_main"] = jnp.concatenate(
        [pad_heads(w_in[:, :, OFF_K:OFF_V]), w_in[:, :, OFF_CONV:OFF_Q], w_in[:, :, OFF_SG:OFF_GATE],
         w_in[:, :, OFF_GATE:], pad_heads(w_in[:, :, OFF_Q:OFF_K])], axis=2).astype(BF16)
    col = jnp.arange(P_COLS)
    is_q = col >= P_Q
    spare = (col % HEAD_PAD >= HEAD_DIM) & (col % HEAD_PAD < HEAD_DIM + N_SPLIT)
    prep["col_scale"] = jnp.where(is_q, LOG2E * HEAD_DIM ** -0.5, 1.0).astype(F32)[None, :]
    prep["col_bias"] = jnp.where(is_q & spare, 1.0, 0.0).astype(F32)[None, :]
    kcol = jnp.arange(ATT_HEADS * HEAD_PAD)
    place = -((kcol[None, None, :] // HEAD_PAD == jnp.arange(LANES)[None, :, None])
              & (kcol[None, None, :] % HEAD_PAD == HEAD_DIM + jnp.arange(N_SPLIT)[:, None, None])
              ).astype(BF16)
    prep["place"] = place.reshape(N_SPLIT * LANES, ATT_HEADS * HEAD_PAD)
    prep["wf"] = jnp.pad(w_in[:, :, OFF_F:OFF_SG], ((0, 0), (0, 0), (0, LANES - ATT_HEADS))).astype(BF16)
    prep["bf"] = jnp.pad(b_forget, ((0, 0), (0, LANES - ATT_HEADS)))[:, None, :]
    wv_t = jnp.swapaxes(w_in[:, :, OFF_V:OFF_F], 1, 2).reshape(depth, ATT_HEADS // 2, 2, HEAD_DIM, d)
    zeros_h = jnp.zeros_like(wv_t[:, :, 0])
    wvt = jnp.concatenate([wv_t[:, :, 0], zeros_h, zeros_h, wv_t[:, :, 1]], axis=2)
    prep["wvt"] = wvt.reshape(depth, ATT_HEADS * HEAD_PAD, d).astype(BF16)
    vrow = jnp.arange(ATT_HEADS * HEAD_PAD) % (2 * HEAD_PAD)
    prep["vbias"] = ((vrow == HEAD_DIM) | (vrow == HEAD_PAD)).astype(F32)[:, None]
    prep["lt_incl"] = jnp.tril(jnp.ones((LANES, LANES), F32)).astype(BF16)
    prep["conv_w"] = jnp.pad(conv_w, ((0, 0), (0, 1), (0, 0)))
    causal = jnp.tril(jnp.ones((SG_CHUNK, SG_CHUNK), bool))
    prep["w_sp"] = jnp.where(causal[None, None], w_spatial, 0).astype(BF16)
    prep["sg_bias"] = jnp.repeat(jnp.swapaxes(b_spatial, 1, 2), SG_WIDTH // SG_GROUPS, axis=2)
    for name, w in (("wc", w_conv_out), ("wa", w_att_out), ("ws", w_sg_out), ("wm", w_mix_out)):
        prep[name] = w.astype(BF16)
    return prep


def _token_mixer(xf, prep, vecs, layer, batch, seq):
    row = lambda name: vecs[name][layer][None, :]
    p, vt = _inproj(xf, row("mix_norm_g"), prep["w_main"], prep["col_scale"], prep["col_bias"],
                    prep["wf"], prep["bf"][layer], prep["lt_incl"], prep["place"],
                    prep["wvt"], prep["vbias"], seq, layer)
    hc = _conv_branch(p, prep["conv_w"][layer], row("conv_b"), row("conv_ln_g"), row("conv_ln_b"), seq)
    hs = _sg_branch(p, row("sg_ln_g"), row("sg_ln_b"), prep["w_sp"][layer], prep["sg_bias"][layer])
    ha = _attention(p, vt, batch, seq)
    return _merge(xf, hc, ha, hs, p, row("b_gate"), prep["wc"], prep["wa"], prep["ws"], prep["wm"], layer)


def kernel(x, mix_norm_g, w_in, b_forget, b_gate, conv_w, conv_b, conv_ln_g, conv_ln_b, w_conv_out,
           w_att_out, sg_ln_g, sg_ln_b, w_spatial, b_spatial, w_sg_out, w_mix_out, ffn_norm_g,
           ffn_w1, ffn_w3, ffn_w2, router_w, moe_w1, moe_w3, moe_w2, final_norm_g):
    batch, seq, d = x.shape
    depth = w_in.shape[0]
    if depth % 2 == 1:
        raise NotImplementedError("the final norm is fused into the last (expert) layer")
    xf = x.reshape(batch * seq, d)

    prep = _prepare_mixer(w_in, b_forget, conv_w, w_spatial, b_spatial, w_conv_out, w_att_out,
                          w_sg_out, w_mix_out)
    vecs = dict(mix_norm_g=mix_norm_g, conv_b=conv_b, conv_ln_g=conv_ln_g, conv_ln_b=conv_ln_b,
                sg_ln_g=sg_ln_g, sg_ln_b=sg_ln_b, b_gate=b_gate)
    f1, f3, f2 = ffn_w1, ffn_w3, ffn_w2
    m1, m3, m2 = moe_w1, moe_w3, moe_w2
    wr = jnp.pad(router_w, ((0, 0), (0, 0), (0, LANES - N_EXPERTS)))
    lt = jnp.tril(jnp.ones((TM_BRANCH, TM_BRANCH), F32), -1).astype(BF16)
    gfin = final_norm_g[None, :]

    for layer in range(depth):
        xf = _token_mixer(xf, prep, vecs, layer, batch, seq)
        g_ffn = ffn_norm_g[layer][None, :]
        i = layer // 2
        if layer % 2 == 0:
            xf = _ffn(xf, g_ffn, f1, f3, f2, i)
        else:
            xf = _moe_layer(xf, g_ffn, wr[i], lt, m1, m3, m2, i, gfin,
                            final_norm=(layer == depth - 1))
    return xf.reshape(batch, seq, d)
```

```python
import functools

import jax
import jax.numpy as jnp
from jax import lax
from jax.experimental import pallas as pl
from jax.experimental.pallas import tpu as pltpu

F32 = jnp.float32
BF16 = jnp.bfloat16

D_MODEL = 1024
CONV_CH = 512
CONV_K = 31
ATT_HEADS = 8
HEAD_DIM = 64
ATT_WIDTH = ATT_HEADS * HEAD_DIM
SG_GROUPS = 8
SG_WIDTH = 512
SG_CHUNK = 128
N_BRANCH = 3
OFF_CONV = 0
OFF_Q = OFF_CONV + 2 * CONV_CH
OFF_K = OFF_Q + ATT_WIDTH
OFF_V = OFF_K + ATT_WIDTH
OFF_F = OFF_V + ATT_WIDTH
OFF_SG = OFF_F + ATT_HEADS
OFF_GATE = OFF_SG + 2 * SG_WIDTH
D_FF = 2816
N_EXPERTS = 8
D_FF_EXPERT = 3584
EPS = 1e-6

LANES = 128
SUBLANES = 8
MIB = 1024 * 1024

HEAD_PAD = LANES
P_K = 0
P_CONV = P_K + ATT_HEADS * HEAD_PAD
P_SG = P_CONV + 2 * CONV_CH
P_GATE = P_SG + 2 * SG_WIDTH
P_Q = P_GATE + N_BRANCH * D_MODEL
P_COLS = P_Q + ATT_HEADS * HEAD_PAD
N_SPLIT = 3
V_ROWS = HEAD_DIM + 16
LOG2E = 1.4426950408889634

TM_PROJ = 1024
TN_PROJ = 1024
TM_BRANCH = 512
CONV_HIST = 32
CONV_RB = 64
TQ = 512
TK = 512
ATT_UNROLL = 14
TM_FFN = 1024
TF_FFN = 256
MOE_BLK = 512
TF_MOE = 512
TM_ROW = 512


def _params(vmem_mib, n_axes):
    return pltpu.CompilerParams(dimension_semantics=("arbitrary",) * n_axes,
                                vmem_limit_bytes=vmem_mib * MIB)


def _rms(x, g):
    return x * lax.rsqrt(jnp.mean(x * x, axis=-1, keepdims=True) + EPS) * g


def _layer_norm(x, g, b):
    mu = jnp.mean(x, axis=-1, keepdims=True)
    d = x - mu
    var = jnp.mean(d * d, axis=-1, keepdims=True)
    return d * lax.rsqrt(var + EPS) * g + b


def _sigmoid(x):
    return 1.0 / (1.0 + jnp.exp(-x))


def _split3(x):
    hi = x.astype(BF16)
    r1 = x - hi.astype(F32)
    mid = r1.astype(BF16)
    lo = (r1 - mid.astype(F32)).astype(BF16)
    return hi, mid, lo


def _inproj_kernel(x_ref, g_ref, w_ref, sc_ref, bi_ref, wf_ref, bf_ref, lt_ref, pl_ref, wvt_ref,
                   vb_ref, p_ref, vt_ref, xn_ref, c3_ref, carry_ref, *, tiles_per_batch):
    i = pl.program_id(0)
    j = pl.program_id(1)
    tm = x_ref.shape[0]
    tk = vt_ref.shape[-1]

    @pl.when(jnp.logical_and(j == 0, i % tiles_per_batch == 0))
    def _():
        carry_ref[...] = jnp.zeros_like(carry_ref)

    @pl.when(j == 0)
    def _():
        xn = _rms(x_ref[...], g_ref[...]).astype(BF16)
        xn_ref[...] = xn
        f = jnp.dot(xn, wf_ref[...], preferred_element_type=F32) + bf_ref[...]
        ls = jnp.minimum(f, 0.0) - jnp.log1p(jnp.exp(-jnp.abs(f)))
        lt = lt_ref[...]
        terms = _split3(ls)
        offset = carry_ref[...]
        for blk in range(tm // LANES):
            rows = slice(blk * LANES, (blk + 1) * LANES)
            c = sum(jnp.dot(lt, term[rows], preferred_element_type=F32) for term in terms) + offset
            offset = c[LANES - 1:LANES, :]
            for r, term in enumerate(_split3(c * LOG2E)):
                c3_ref[rows, r * LANES:(r + 1) * LANES] = term
        carry_ref[...] = offset
        vt = lax.dot_general(wvt_ref[...], xn, (((1,), (1,)), ((), ())),
                             preferred_element_type=F32) + vb_ref[...]
        for ch in range(tm // tk):
            vt_ref[ch] = vt[:, ch * tk:(ch + 1) * tk].astype(BF16)
        decay = jnp.dot(c3_ref[...], pl_ref[...], preferred_element_type=F32)
        p_ref[...] = (jnp.dot(xn, w_ref[...], preferred_element_type=F32) + decay).astype(BF16)

    @pl.when(j > 0)
    def _():
        acc = jnp.dot(xn_ref[...], w_ref[...], preferred_element_type=F32)
        p_ref[...] = (acc * sc_ref[...] + bi_ref[...]).astype(BF16)


def _inproj(x, g, w, col_scale, col_bias, wf, bf, lt, place, wvt, vbias, seq, layer):
    n = x.shape[0]
    tm, tn = TM_PROJ, TN_PROJ
    assert P_K == 0 and tn == ATT_HEADS * HEAD_PAD
    kern = functools.partial(_inproj_kernel, tiles_per_batch=seq // tm)
    vt_rows = ATT_HEADS * V_ROWS
    const = lambda shape: pl.BlockSpec(shape, lambda i, j: (0,) * len(shape))
    return pl.pallas_call(
        kern,
        grid=(n // tm, P_COLS // tn),
        in_specs=[
            pl.BlockSpec((tm, D_MODEL), lambda i, j: (i, 0)),
            const((1, D_MODEL)),
            pl.BlockSpec((None, D_MODEL, tn), lambda i, j: (layer, 0, j)),
            pl.BlockSpec((1, tn), lambda i, j: (0, j)),
            pl.BlockSpec((1, tn), lambda i, j: (0, j)),
            pl.BlockSpec((None, D_MODEL, LANES), lambda i, j: (layer, 0, 0)),
            const((1, LANES)),
            const((LANES, LANES)),
            const((N_SPLIT * LANES, tn)),
            pl.BlockSpec((None, vt_rows, D_MODEL), lambda i, j: (layer, 0, 0)),
            const((vt_rows, 1)),
        ],
        out_specs=[
            pl.BlockSpec((tm, tn), lambda i, j: (i, j)),
            pl.BlockSpec((tm // TK, vt_rows, TK), lambda i, j: (i, 0, 0)),
        ],
        out_shape=[
            jax.ShapeDtypeStruct((n, P_COLS), BF16),
            jax.ShapeDtypeStruct((n // TK, vt_rows, TK), BF16),
        ],
        scratch_shapes=[pltpu.VMEM((tm, D_MODEL), BF16), pltpu.VMEM((tm, N_SPLIT * LANES), BF16),
                        pltpu.VMEM((1, LANES), F32)],
        compiler_params=_params(48, 2),
        name="inproj",
    )(x, g, w, col_scale, col_bias, wf, bf, lt, place, wvt, vbias)


def _conv_kernel(a1_ref, a2_ref, w_ref, cb_ref, g_ref, b_ref, o_ref, hext_ref, hsh_ref,
                 *, tiles_per_batch):
    i = pl.program_id(0)
    tm = a1_ref.shape[0]
    span = hsh_ref.shape[1]

    @pl.when(i % tiles_per_batch == 0)
    def _():
        hext_ref[0:CONV_HIST, :] = jnp.zeros((CONV_HIST, CONV_CH), F32)

    @pl.when(i % tiles_per_batch != 0)
    def _():
        hext_ref[0:CONV_HIST, :] = hext_ref[tm:tm + CONV_HIST, :]

    hext_ref[CONV_HIST:CONV_HIST + tm, :] = (
        a1_ref[...].astype(F32) * _sigmoid(a2_ref[...].astype(F32)))
    for s in range(1, SUBLANES):
        hsh_ref[s] = hext_ref[s:s + span, :]

    g = g_ref[...]
    b = b_ref[...]
    base = CONV_HIST - (CONV_K - 1)
    for r in range(0, tm, CONV_RB):
        acc = jnp.broadcast_to(cb_ref[...], (CONV_RB, CONV_CH))
        for j in range(CONV_K):
            s = (base + j) % SUBLANES
            a = r + base + j - s
            rows = hext_ref[a:a + CONV_RB, :] if s == 0 else hsh_ref[s, a:a + CONV_RB, :]
            acc = acc + w_ref[j:j + 1, :] * rows
        y = _layer_norm(acc, g, b)
        o_ref[r:r + CONV_RB, :] = (y * _sigmoid(y)).astype(BF16)


def _conv_branch(p, w, cb, g, b, seq):
    n = p.shape[0]
    tm = TM_BRANCH
    kern = functools.partial(_conv_kernel, tiles_per_batch=seq // tm)
    c0 = P_CONV // CONV_CH
    vec = pl.BlockSpec((1, CONV_CH), lambda i: (0, 0))
    return pl.pallas_call(
        kern,
        grid=(n // tm,),
        in_specs=[
            pl.BlockSpec((tm, CONV_CH), lambda i: (i, c0)),
            pl.BlockSpec((tm, CONV_CH), lambda i: (i, c0 + 1)),
            pl.BlockSpec((CONV_K + 1, CONV_CH), lambda i: (0, 0)),
            vec, vec, vec,
        ],
        out_specs=pl.BlockSpec((tm, CONV_CH), lambda i: (i, 0)),
        out_shape=jax.ShapeDtypeStruct((n, CONV_CH), BF16),
        scratch_shapes=[pltpu.VMEM((tm + CONV_HIST, CONV_CH), F32),
                        pltpu.VMEM((SUBLANES, tm + CONV_HIST - SUBLANES, CONV_CH), F32)],
        compiler_params=_params(32, 1),
        name="conv_branch",
    )(p, p, w, cb, g, b)


def _sg_kernel(u_ref, v_ref, g_ref, b_ref, w_ref, bias_ref, o_ref):
    tm = u_ref.shape[0]

    def gelu(z):
        return 0.5 * z * (1.0 + lax.erf(z * 0.7071067811865476))

    zu = gelu(u_ref[...].astype(F32))
    vn = _layer_norm(gelu(v_ref[...].astype(F32)), g_ref[...], b_ref[...]).astype(BF16)
    lane = lax.broadcasted_iota(jnp.int32, (SG_CHUNK, LANES), 1)
    first_group = lane < (SG_WIDTH // SG_GROUPS)
    for c in range(tm // SG_CHUNK):
        rows = slice(c * SG_CHUNK, (c + 1) * SG_CHUNK)
        for pr in range(SG_WIDTH // LANES):
            cols = slice(pr * LANES, (pr + 1) * LANES)
            vp = vn[rows, cols]
            m0 = jnp.dot(w_ref[2 * pr], vp, preferred_element_type=F32)
            m1 = jnp.dot(w_ref[2 * pr + 1], vp, preferred_element_type=F32)
            mixed = jnp.where(first_group, m0, m1) + bias_ref[:, cols]
            o_ref[rows, cols] = (zu[rows, cols] * mixed).astype(BF16)


def _sg_branch(p, g, b, w_tril, bias_full):
    n = p.shape[0]
    tm = TM_BRANCH
    c0 = P_SG // SG_WIDTH
    vec = pl.BlockSpec((1, SG_WIDTH), lambda i: (0, 0))
    return pl.pallas_call(
        _sg_kernel,
        grid=(n // tm,),
        in_specs=[
            pl.BlockSpec((tm, SG_WIDTH), lambda i: (i, c0)),
            pl.BlockSpec((tm, SG_WIDTH), lambda i: (i, c0 + 1)),
            vec, vec,
            pl.BlockSpec((SG_GROUPS, SG_CHUNK, SG_CHUNK), lambda i: (0, 0, 0)),
            pl.BlockSpec((SG_CHUNK, SG_WIDTH), lambda i: (0, 0)),
        ],
        out_specs=pl.BlockSpec((tm, SG_WIDTH), lambda i: (i, 0)),
        out_shape=jax.ShapeDtypeStruct((n, SG_WIDTH), BF16),
        compiler_params=_params(32, 1),
        name="sg_branch",
    )(p, p, g, b, w_tril, bias_full)


def _attn_kernel(ti_ref, tj_ref, q_ref, k_ref, vt_ref, o_ref, sa_ref, sb_ref, m_ref, acc_ref, tri_ref,
                 *, n_below, n_diag):
    tq = sa_ref.shape[-1]
    tk = vt_ref.shape[-1]
    head_cols = [slice(hh * HEAD_PAD, (hh + 1) * HEAD_PAD) for hh in range(2)]

    def scores(t, dst_ref):
        q0 = pl.multiple_of(ti_ref[t] * tq, tq)
        k0 = pl.multiple_of(tj_ref[t] * tk, tk)
        for hh, cols in enumerate(head_cols):
            dst_ref[hh] = lax.dot_general(k_ref[pl.ds(k0, tk), cols], q_ref[pl.ds(q0, tq), cols],
                                          (((1,), (1,)), ((), ())), preferred_element_type=F32)

    def consume(t, src_ref, masked):
        i = ti_ref[t]
        j = tj_ref[t]
        probs = []
        for hh in range(2):
            s = src_ref[hh]
            if masked:
                s = s + tri_ref[...]
            m = m_ref[i, hh]
            m_new = jnp.maximum(m, jnp.max(s, axis=0, keepdims=True))
            m_ref[i, hh] = m_new
            probs.append((jnp.exp2(m - m_new), jnp.exp2(s - m_new).astype(BF16)))
        for hh, (alpha, pexp) in enumerate(probs):
            pv = jnp.dot(vt_ref[j, hh * V_ROWS:(hh + 1) * V_ROWS, :], pexp, preferred_element_type=F32)
            acc_ref[i, hh] = alpha * acc_ref[i, hh] + pv

    def run(base, count, masked):
        bufs = (sa_ref, sb_ref)
        scores(base, bufs[0])

        def trip(u, carry):
            t = base + ATT_UNROLL * u
            for k in range(ATT_UNROLL):
                scores(t + k + 1, bufs[(k + 1) % 2])
                consume(t + k, bufs[k % 2], masked)
            return carry

        lax.fori_loop(0, count // ATT_UNROLL, trip, 0)
        done = count - count % ATT_UNROLL
        for k in range(done, count):
            if k + 1 < count:
                scores(base + k + 1, bufs[(k + 1) % 2])
            consume(base + k, bufs[k % 2], masked)

    tri_ref[...] = jnp.where(lax.broadcasted_iota(jnp.int32, (tk, tq), 0)
                             <= lax.broadcasted_iota(jnp.int32, (tk, tq), 1), 0.0, -jnp.inf)
    m_ref[...] = jnp.full(m_ref.shape, -jnp.inf, F32)
    acc_ref[...] = jnp.zeros(acc_ref.shape, F32)
    if n_below:
        run(0, n_below, False)
    run(n_below + 1, n_diag, True)

    eye = (lax.broadcasted_iota(jnp.int32, (tq, tq), 0)
           == lax.broadcasted_iota(jnp.int32, (tq, tq), 1)).astype(BF16)
    for i in range(q_ref.shape[0] // tq):
        even = acc_ref[i, 0]
        odd = acc_ref[i, 1]
        merged = jnp.concatenate([even[:HEAD_DIM] / even[HEAD_DIM:HEAD_DIM + 1, :],
                                  odd[V_ROWS - HEAD_DIM:] / odd[0:1, :]], axis=0)
        o_ref[i * tq:(i + 1) * tq, :] = lax.dot_general(
            eye, merged.astype(BF16), (((1,), (1,)), ((), ())),
            preferred_element_type=F32).astype(BF16)


def _attention(p, vt, batch, seq):
    n = p.shape[0]
    pairs = ATT_HEADS // 2
    assert TQ == TK
    nq = seq // TQ
    below = [(i, j) for i in range(nq) for j in range(i)]
    diag = [(i, i) for i in range(nq)]
    tiles = below + below[-1:] + diag + diag[-1:] if below else [(0, 0)] + diag + diag[-1:]
    ti = jnp.asarray([t[0] for t in tiles], jnp.int32)
    tj = jnp.asarray([t[1] for t in tiles], jnp.int32)
    pw = 2 * HEAD_PAD
    grid_spec = pltpu.PrefetchScalarGridSpec(
        num_scalar_prefetch=2,
        grid=(batch, pairs),
        in_specs=[
            pl.BlockSpec((seq, pw), lambda b, h, ti, tj: (b, P_Q // pw + h)),
            pl.BlockSpec((seq, pw), lambda b, h, ti, tj: (b, P_K // pw + h)),
            pl.BlockSpec((seq // TK, 2 * V_ROWS, TK), lambda b, h, ti, tj: (b, h, 0)),
        ],
        out_specs=pl.BlockSpec((seq, LANES), lambda b, h, ti, tj: (b, h)),
        scratch_shapes=[pltpu.VMEM((2, TK, TQ), F32), pltpu.VMEM((2, TK, TQ), F32),
                        pltpu.VMEM((nq, 2, 1, TQ), F32), pltpu.VMEM((nq, 2, V_ROWS, TQ), F32),
                        pltpu.VMEM((TK, TQ), F32)],
    )
    return pl.pallas_call(
        functools.partial(_attn_kernel, n_below=len(below), n_diag=len(diag)),
        grid_spec=grid_spec,
        out_shape=jax.ShapeDtypeStruct((n, ATT_WIDTH), BF16),
        compiler_params=_params(40, 2),
        name="fox_attention",
    )(ti, tj, p, p, vt)


def _merge_kernel(x_ref, hc_ref, ha_ref, hs_ref, g0_ref, g1_ref, g2_ref, bg_ref,
                  wc_ref, wa_ref, ws_ref, wm_ref, o_ref):
    merged = None
    for k, (h_ref, w_ref, gl_ref) in enumerate(((hc_ref, wc_ref, g0_ref), (ha_ref, wa_ref, g1_ref),
                                                (hs_ref, ws_ref, g2_ref))):
        y = jnp.dot(h_ref[...], w_ref[...], preferred_element_type=F32)
        gate = _sigmoid(gl_ref[...].astype(F32) + bg_ref[:, k * D_MODEL:(k + 1) * D_MODEL])
        merged = gate * y if merged is None else merged + gate * y
    o_ref[...] = x_ref[...] + jnp.dot(merged.astype(BF16), wm_ref[...], preferred_element_type=F32)


def _merge(x, hc, ha, hs, p, bg, wc, wa, ws, wm, layer):
    n = x.shape[0]
    tm = TM_BRANCH
    g0 = P_GATE // D_MODEL
    half = lambda: pl.BlockSpec((tm, CONV_CH), lambda i: (i, 0))
    wspec = lambda k: pl.BlockSpec((None, k, D_MODEL), lambda i: (layer, 0, 0))
    return pl.pallas_call(
        _merge_kernel,
        grid=(n // tm,),
        in_specs=[
            pl.BlockSpec((tm, D_MODEL), lambda i: (i, 0)),
            half(), half(), half(),
            pl.BlockSpec((tm, D_MODEL), lambda i: (i, g0)),
            pl.BlockSpec((tm, D_MODEL), lambda i: (i, g0 + 1)),
            pl.BlockSpec((tm, D_MODEL), lambda i: (i, g0 + 2)),
            pl.BlockSpec((1, N_BRANCH * D_MODEL), lambda i: (0, 0)),
            wspec(CONV_CH), wspec(ATT_WIDTH), wspec(SG_WIDTH), wspec(D_MODEL),
        ],
        out_specs=pl.BlockSpec((tm, D_MODEL), lambda i: (i, 0)),
        out_shape=jax.ShapeDtypeStruct((n, D_MODEL), F32),
        compiler_params=_params(48, 1),
        name="merge",
    )(x, hc, ha, hs, p, p, p, bg, wc, wa, ws, wm)


def _ffn_kernel(x_ref, g_ref, w1_ref, w3_ref, w2_ref, o_ref, hn_ref, acc_ref,
                wb1_ref, wb3_ref, wb2_ref):
    f = pl.program_id(1)

    @pl.when(pl.program_id(0) == 0)
    def _():
        wb1_ref[f] = w1_ref[...].astype(BF16)
        wb3_ref[f] = w3_ref[...].astype(BF16)
        wb2_ref[f] = w2_ref[...].astype(BF16)

    @pl.when(f == 0)
    def _():
        hn_ref[...] = _rms(x_ref[...], g_ref[...]).astype(BF16)
        acc_ref[...] = jnp.zeros_like(acc_ref)

    hn = hn_ref[...]
    a = jnp.dot(hn, wb1_ref[f], preferred_element_type=F32)
    b = jnp.dot(hn, wb3_ref[f], preferred_element_type=F32)
    t = (a * _sigmoid(a) * b).astype(BF16)
    acc_ref[...] += jnp.dot(t, wb2_ref[f], preferred_element_type=F32)

    @pl.when(f == pl.num_programs(1) - 1)
    def _():
        o_ref[...] = x_ref[...] + acc_ref[...]


def _ffn(x, g, w1, w3, w2, layer):
    n = x.shape[0]
    tm, tf = TM_FFN, TF_FFN
    nf = D_FF // tf

    def f_idx(i, f):
        return jnp.where(i == 0, f, nf - 1)

    return pl.pallas_call(
        _ffn_kernel,
        grid=(n // tm, nf),
        in_specs=[
            pl.BlockSpec((tm, D_MODEL), lambda i, f: (i, 0)),
            pl.BlockSpec((1, D_MODEL), lambda i, f: (0, 0)),
            pl.BlockSpec((None, D_MODEL, tf), lambda i, f: (layer, 0, f_idx(i, f))),
            pl.BlockSpec((None, D_MODEL, tf), lambda i, f: (layer, 0, f_idx(i, f))),
            pl.BlockSpec((None, tf, D_MODEL), lambda i, f: (layer, f_idx(i, f), 0)),
        ],
        out_specs=pl.BlockSpec((tm, D_MODEL), lambda i, f: (i, 0)),
        out_shape=jax.ShapeDtypeStruct((n, D_MODEL), F32),
        scratch_shapes=[pltpu.VMEM((tm, D_MODEL), BF16), pltpu.VMEM((tm, D_MODEL), F32),
                        pltpu.VMEM((nf, D_MODEL, tf), BF16), pltpu.VMEM((nf, D_MODEL, tf), BF16),
                        pltpu.VMEM((nf, tf, D_MODEL), BF16)],
        compiler_params=_params(56, 2),
        name="ffn_dense",
    )(x, g, w1, w3, w2)


def _router_kernel(x_ref, g_ref, wr_ref, lt_ref, h_ref, meta_ref, cnt_ref, carry_ref):
    i = pl.program_id(0)
    tm = x_ref.shape[0]

    @pl.when(i == 0)
    def _():
        carry_ref[...] = jnp.zeros_like(carry_ref)

    h = _rms(x_ref[...], g_ref[...])
    h_ref[...] = h
    h_hi = h.astype(BF16)
    h_lo = (h - h_hi.astype(F32)).astype(BF16)
    w = wr_ref[...]
    w_hi = w.astype(BF16)
    w_lo = (w - w_hi.astype(F32)).astype(BF16)
    logits = (jnp.dot(h_hi, w_hi, preferred_element_type=F32)
              + jnp.dot(h_hi, w_lo, preferred_element_type=F32)
              + jnp.dot(h_lo, w_hi, preferred_element_type=F32))
    lane = lax.broadcasted_iota(jnp.int32, (tm, LANES), 1)
    lanef = lane.astype(F32)
    lg = jnp.where(lane < N_EXPERTS, logits, -jnp.inf)
    v1 = jnp.max(lg, axis=1, keepdims=True)
    i1 = jnp.min(jnp.where(lg == v1, lanef, float(LANES)), axis=1, keepdims=True)
    lg2 = jnp.where(lanef == i1, -jnp.inf, lg)
    v2 = jnp.max(lg2, axis=1, keepdims=True)
    i2 = jnp.min(jnp.where(lg2 == v2, lanef, float(LANES)), axis=1, keepdims=True)
    e = jnp.exp(v2 - v1)
    g1 = 1.0 / (1.0 + e)
    g2 = e / (1.0 + e)
    oh1 = lanef == i1
    oh2 = lanef == i2
    cnt = (oh1.astype(F32) + oh2.astype(F32))
    before = jnp.dot(lt_ref[...], cnt.astype(BF16), preferred_element_type=F32) + carry_ref[...]
    r1 = jnp.sum(jnp.where(oh1, before, 0.0), axis=1, keepdims=True)
    r2 = jnp.sum(jnp.where(oh2, before, 0.0), axis=1, keepdims=True)
    total = carry_ref[...] + jnp.sum(cnt, axis=0, keepdims=True)
    carry_ref[...] = total
    cnt_ref[...] = jnp.broadcast_to(total, cnt_ref.shape)
    meta = jnp.zeros((tm, LANES), F32)
    for k, val in enumerate((i1, i2, g1, g2, r1, r2)):
        meta = jnp.where(lane == k, val, meta)
    meta_ref[...] = meta


def _router(x, g, wr, lt):
    n = x.shape[0]
    tm = TM_BRANCH
    return pl.pallas_call(
        _router_kernel,
        grid=(n // tm,),
        in_specs=[
            pl.BlockSpec((tm, D_MODEL), lambda i: (i, 0)),
            pl.BlockSpec((1, D_MODEL), lambda i: (0, 0)),
            pl.BlockSpec((D_MODEL, LANES), lambda i: (0, 0)),
            pl.BlockSpec((tm, tm), lambda i: (0, 0)),
        ],
        out_specs=[
            pl.BlockSpec((tm, D_MODEL), lambda i: (i, 0)),
            pl.BlockSpec((tm, LANES), lambda i: (i, 0)),
            pl.BlockSpec((8, LANES), lambda i: (0, 0)),
        ],
        out_shape=[
            jax.ShapeDtypeStruct((n, D_MODEL), F32),
            jax.ShapeDtypeStruct((n, LANES), F32),
            jax.ShapeDtypeStruct((8, LANES), F32),
        ],
        scratch_shapes=[pltpu.VMEM((1, LANES), F32)],
        compiler_params=_params(32, 1),
        name="router",
    )(x, g, wr, lt)


def _row_copy(src_ref, src_row, dst_ref, dst_row, sem):
    return pltpu.make_async_copy(src_ref.at[pl.ds(src_row, 1), :], dst_ref.at[pl.ds(dst_row, 1), :], sem)


def _dispatch_kernel(bounds_ref, dest_ref, h_ref, xs_ref, zero_ref, sem, zero_sem):
    tm = h_ref.shape[0]
    n_blk = xs_ref.shape[0] // MOE_BLK

    @pl.when(pl.program_id(0) == 0)
    def _():
        zero_ref[...] = jnp.zeros_like(zero_ref)

        def zero_block(blk):
            start = pl.multiple_of(blk * MOE_BLK, MOE_BLK)
            copy = pltpu.make_async_copy(zero_ref, xs_ref.at[pl.ds(start, MOE_BLK), :], zero_sem)
            copy.start()
            copy.wait()

        for e in range(N_EXPERTS):
            @pl.when(bounds_ref[e + 1] > bounds_ref[e])
            def _():
                zero_block(bounds_ref[e + 1] // MOE_BLK - 1)

        def unused(blk, carry):
            zero_block(blk)
            return carry

        lax.fori_loop(bounds_ref[N_EXPERTS] // MOE_BLK, n_blk, unused, 0)

    def issue(r, carry):
        for k in range(2):
            _row_copy(h_ref, r, xs_ref, dest_ref[2 * r + k], sem).start(priority=k)
        return carry

    lax.fori_loop(0, tm, issue, 0, unroll=8)
    for _ in range(2):
        pltpu.make_async_copy(h_ref, xs_ref.at[pl.ds(0, tm), :], sem).wait()


def _dispatch(bounds, dest_flat, h, n_rows):
    n = h.shape[0]
    tm = TM_ROW
    return pl.pallas_call(
        _dispatch_kernel,
        grid=(n // tm,),
        in_specs=[
            pl.BlockSpec(memory_space=pltpu.SMEM),
            pl.BlockSpec((2 * tm,), lambda i: (i,), memory_space=pltpu.SMEM),
            pl.BlockSpec((tm, D_MODEL), lambda i: (i, 0)),
        ],
        out_specs=pl.BlockSpec(memory_space=pl.ANY),
        out_shape=jax.ShapeDtypeStruct((n_rows, D_MODEL), F32),
        scratch_shapes=[pltpu.VMEM((MOE_BLK, D_MODEL), F32), pltpu.SemaphoreType.DMA(()),
                        pltpu.SemaphoreType.DMA(())],
        compiler_params=_params(32, 1),
        name="moe_dispatch",
    )(bounds, dest_flat, h)


def _first_block_of_expert(b, be_ref):
    return jnp.logical_or(b == 0, be_ref[b] != be_ref[jnp.maximum(b - 1, 0)])


def _moe_kernel(be_ref, nu_ref, x_ref, w1_ref, w3_ref, w2_ref, y_ref, xb_ref, acc_ref,
                wb1_ref, wb3_ref, wb2_ref):
    b = pl.program_id(0)
    f = pl.program_id(1)

    @pl.when(b < nu_ref[0])
    def _():
        @pl.when(_first_block_of_expert(b, be_ref))
        def _():
            wb1_ref[f] = w1_ref[...].astype(BF16)
            wb3_ref[f] = w3_ref[...].astype(BF16)
            wb2_ref[f] = w2_ref[...].astype(BF16)

        @pl.when(f == 0)
        def _():
            xb_ref[...] = x_ref[...].astype(BF16)
            acc_ref[...] = jnp.zeros_like(acc_ref)

        xb = xb_ref[...]
        a = jnp.dot(xb, wb1_ref[f], preferred_element_type=F32)
        c = jnp.dot(xb, wb3_ref[f], preferred_element_type=F32)
        t = (a * _sigmoid(a) * c).astype(BF16)
        acc_ref[...] += jnp.dot(t, wb2_ref[f], preferred_element_type=F32)

        @pl.when(f == pl.num_programs(1) - 1)
        def _():
            y_ref[...] = acc_ref[...]

    @pl.when(jnp.logical_and(b >= nu_ref[0], f == pl.num_programs(1) - 1))
    def _():
        y_ref[...] = jnp.zeros_like(y_ref)


def _moe_experts(blk_e, n_used, xs, w1, w3, w2, layer):
    rows = xs.shape[0]
    n_blk = rows // MOE_BLK
    nf = D_FF_EXPERT // TF_MOE

    def row_idx(b, f, be, nu):
        return (jnp.minimum(b, nu[0] - 1), 0)

    def f_idx(b, f, be, nu):
        fetch = jnp.logical_and(b < nu[0], _first_block_of_expert(b, be))
        return jnp.where(fetch, f, nf - 1)

    grid_spec = pltpu.PrefetchScalarGridSpec(
        num_scalar_prefetch=2,
        grid=(n_blk, nf),
        in_specs=[
            pl.BlockSpec((MOE_BLK, D_MODEL), row_idx),
            pl.BlockSpec((None, None, D_MODEL, TF_MOE),
                         lambda b, f, be, nu: (layer, be[b], 0, f_idx(b, f, be, nu))),
            pl.BlockSpec((None, None, D_MODEL, TF_MOE),
                         lambda b, f, be, nu: (layer, be[b], 0, f_idx(b, f, be, nu))),
            pl.BlockSpec((None, None, TF_MOE, D_MODEL),
                         lambda b, f, be, nu: (layer, be[b], f_idx(b, f, be, nu), 0)),
        ],
        out_specs=pl.BlockSpec((MOE_BLK, D_MODEL), lambda b, f, be, nu: (b, 0)),
        scratch_shapes=[pltpu.VMEM((MOE_BLK, D_MODEL), BF16), pltpu.VMEM((MOE_BLK, D_MODEL), F32),
                        pltpu.VMEM((nf, D_MODEL, TF_MOE), BF16), pltpu.VMEM((nf, D_MODEL, TF_MOE), BF16),
                        pltpu.VMEM((nf, TF_MOE, D_MODEL), BF16)],
    )
    return pl.pallas_call(
        _moe_kernel,
        grid_spec=grid_spec,
        out_shape=jax.ShapeDtypeStruct((rows, D_MODEL), F32),
        compiler_params=_params(56, 2),
        name="moe_experts",
    )(blk_e, n_used, xs, w1, w3, w2)


def _combine_kernel(dest_ref, x_ref, meta_ref, gfin_ref, y_ref, o_ref, ybuf_ref, sem, *, final_norm):
    tm = x_ref.shape[0]

    def issue(r, carry):
        for k in range(2):
            _row_copy(y_ref, dest_ref[2 * r + k], ybuf_ref.at[k], r, sem).start(priority=k)
        return carry

    lax.fori_loop(0, tm, issue, 0, unroll=8)
    for k in range(2):
        pltpu.make_async_copy(y_ref.at[pl.ds(0, tm), :], ybuf_ref.at[k], sem).wait()

    g1 = meta_ref[:, 2:3]
    g2 = meta_ref[:, 3:4]
    out = x_ref[...] + (g1 * ybuf_ref[0] + g2 * ybuf_ref[1])
    if final_norm:
        out = _rms(out, gfin_ref[...])
    o_ref[...] = out


def _combine(dest_flat, x, meta, gfin, y, final_norm):
    n = x.shape[0]
    tm = TM_ROW
    kern = functools.partial(_combine_kernel, final_norm=final_norm)
    return pl.pallas_call(
        kern,
        grid=(n // tm,),
        in_specs=[
            pl.BlockSpec((2 * tm,), lambda i: (i,), memory_space=pltpu.SMEM),
            pl.BlockSpec((tm, D_MODEL), lambda i: (i, 0)),
            pl.BlockSpec((tm, LANES), lambda i: (i, 0)),
            pl.BlockSpec((1, D_MODEL), lambda i: (0, 0)),
            pl.BlockSpec(memory_space=pl.ANY),
        ],
        out_specs=pl.BlockSpec((tm, D_MODEL), lambda i: (i, 0)),
        out_shape=jax.ShapeDtypeStruct((n, D_MODEL), F32),
        scratch_shapes=[pltpu.VMEM((2, tm, D_MODEL), F32), pltpu.SemaphoreType.DMA(())],
        compiler_params=_params(32, 1),
        name="moe_combine",
    )(dest_flat, x, meta, gfin, y)


def _moe_layer(x, g, wr, lt, w1, w3, w2, layer, gfin, final_norm):
    n = x.shape[0]
    h, meta, cnt = _router(x, g, wr, lt)
    expert = meta[:, 0:2].astype(jnp.int32)
    rank = meta[:, 4:6].astype(jnp.int32)
    counts = cnt[0, :N_EXPERTS].astype(jnp.int32)
    padded = (counts + MOE_BLK - 1) // MOE_BLK * MOE_BLK
    pad_end = jnp.cumsum(padded)
    pad_start = pad_end - padded
    dest = (pad_start[expert] + rank).reshape(-1)
    n_blk = (2 * n) // MOE_BLK + N_EXPERTS
    blk_start = jnp.arange(n_blk, dtype=jnp.int32) * MOE_BLK
    blk_e = jnp.minimum(jnp.sum(blk_start[:, None] >= pad_end[None, :], axis=1), N_EXPERTS - 1)
    n_used = (pad_end[-1:] // MOE_BLK).astype(jnp.int32)
    bounds = jnp.concatenate([jnp.zeros((1,), jnp.int32), pad_end.astype(jnp.int32)])
    xs = _dispatch(bounds, dest, h, n_blk * MOE_BLK)
    y = _moe_experts(blk_e.astype(jnp.int32), n_used, xs, w1, w3, w2, layer)
    return _combine(dest, x, meta, gfin, y, final_norm)


def _prepare_mixer(w_in, b_forget, conv_w, w_spatial, b_spatial, w_conv_out, w_att_out, w_sg_out,
                   w_mix_out):
    depth, d, _ = w_in.shape

    def pad_heads(w):
        w = w.reshape(depth, d, ATT_HEADS, HEAD_DIM)
        w = jnp.pad(w, ((0, 0), (0, 0), (0, 0), (0, HEAD_PAD - HEAD_DIM)))
        return w.reshape(depth, d, ATT_HEADS * HEAD_PAD)

    prep = {}
    prep["w_main"] = jnp.concatenate(
        [pad_heads(w_in[:, :, OFF_K:OFF_V]), w_in[:, :, OFF_CONV:OFF_Q], w_in[:, :, OFF_SG:OFF_GATE],
         w_in[:, :, OFF_GATE:], pad_heads(w_in[:, :, OFF_Q:OFF_K])], axis=2).astype(BF16)
    col = jnp.arange(P_COLS)
    is_q = col >= P_Q
    spare = (col % HEAD_PAD >= HEAD_DIM) & (col % HEAD_PAD < HEAD_DIM + N_SPLIT)
    prep["col_scale"] = jnp.where(is_q, LOG2E * HEAD_DIM ** -0.5, 1.0).astype(F32)[None, :]
    prep["col_bias"] = jnp.where(is_q & spare, 1.0, 0.0).astype(F32)[None, :]
    kcol = jnp.arange(ATT_HEADS * HEAD_PAD)
    place = -((kcol[None, None, :] // HEAD_PAD == jnp.arange(LANES)[None, :, None])
              & (kcol[None, None, :] % HEAD_PAD == HEAD_DIM + jnp.arange(N_SPLIT)[:, None, None])
              ).astype(BF16)
    prep["place"] = place.reshape(N_SPLIT * LANES, ATT_HEADS * HEAD_PAD)
    prep["wf"] = jnp.pad(w_in[:, :, OFF_F:OFF_SG], ((0, 0), (0, 0), (0, LANES - ATT_HEADS))).astype(BF16)
    prep["bf"] = jnp.pad(b_forget, ((0, 0), (0, LANES - ATT_HEADS)))[:, None, :]
    wv_t = jnp.swapaxes(w_in[:, :, OFF_V:OFF_F], 1, 2).reshape(depth, ATT_HEADS // 2, 2, HEAD_DIM, d)
    zeros_h = jnp.zeros_like(wv_t[:, :, 0, :V_ROWS - HEAD_DIM])
    wvt = jnp.concatenate([wv_t[:, :, 0], zeros_h, zeros_h, wv_t[:, :, 1]], axis=2)
    prep["wvt"] = wvt.reshape(depth, ATT_HEADS * V_ROWS, d).astype(BF16)
    vrow = jnp.arange(ATT_HEADS * V_ROWS) % (2 * V_ROWS)
    prep["vbias"] = ((vrow == HEAD_DIM) | (vrow == V_ROWS)).astype(F32)[:, None]
    prep["lt_incl"] = jnp.tril(jnp.ones((LANES, LANES), F32)).astype(BF16)
    prep["conv_w"] = jnp.pad(conv_w, ((0, 0), (0, 1), (0, 0)))
    causal = jnp.tril(jnp.ones((SG_CHUNK, SG_CHUNK), bool))
    prep["w_sp"] = jnp.where(causal[None, None], w_spatial, 0).astype(BF16)
    prep["sg_bias"] = jnp.repeat(jnp.swapaxes(b_spatial, 1, 2), SG_WIDTH // SG_GROUPS, axis=2)
    for name, w in (("wc", w_conv_out), ("wa", w_att_out), ("ws", w_sg_out), ("wm", w_mix_out)):
        prep[name] = w.astype(BF16)
    return prep


def _token_mixer(xf, prep, vecs, layer, batch, seq):
    row = lambda name: vecs[name][layer][None, :]
    p, vt = _inproj(xf, row("mix_norm_g"), prep["w_main"], prep["col_scale"], prep["col_bias"],
                    prep["wf"], prep["bf"][layer], prep["lt_incl"], prep["place"],
                    prep["wvt"], prep["vbias"], seq, layer)
    hc = _conv_branch(p, prep["conv_w"][layer], row("conv_b"), row("conv_ln_g"), row("conv_ln_b"), seq)
    hs = _sg_branch(p, row("sg_ln_g"), row("sg_ln_b"), prep["w_sp"][layer], prep["sg_bias"][layer])
    ha = _attention(p, vt, batch, seq)
    return _merge(xf, hc, ha, hs, p, row("b_gate"), prep["wc"], prep["wa"], prep["ws"], prep["wm"], layer)


def kernel(x, mix_norm_g, w_in, b_forget, b_gate, conv_w, conv_b, conv_ln_g, conv_ln_b, w_conv_out,
           w_att_out, sg_ln_g, sg_ln_b, w_spatial, b_spatial, w_sg_out, w_mix_out, ffn_norm_g,
           ffn_w1, ffn_w3, ffn_w2, router_w, moe_w1, moe_w3, moe_w2, final_norm_g):
    batch, seq, d = x.shape
    depth = w_in.shape[0]
    if depth % 2 == 1:
        raise NotImplementedError("the final norm is fused into the last (expert) layer")
    xf = x.reshape(batch * seq, d)

    prep = _prepare_mixer(w_in, b_forget, conv_w, w_spatial, b_spatial, w_conv_out, w_att_out,
                          w_sg_out, w_mix_out)
    vecs = dict(mix_norm_g=mix_norm_g, conv_b=conv_b, conv_ln_g=conv_ln_g, conv_ln_b=conv_ln_b,
                sg_ln_g=sg_ln_g, sg_ln_b=sg_ln_b, b_gate=b_gate)
    f1, f3, f2 = ffn_w1, ffn_w3, ffn_w2
    m1, m3, m2 = moe_w1, moe_w3, moe_w2
    wr = jnp.pad(router_w, ((0, 0), (0, 0), (0, LANES - N_EXPERTS)))
    lt = jnp.tril(jnp.ones((TM_BRANCH, TM_BRANCH), F32), -1).astype(BF16)
    gfin = final_norm_g[None, :]

    for layer in range(depth):
        xf = _token_mixer(xf, prep, vecs, layer, batch, seq)
        g_ffn = ffn_norm_g[layer][None, :]
        i = layer // 2
        if layer % 2 == 0:
            xf = _ffn(xf, g_ffn, f1, f3, f2, i)
        else:
            xf = _moe_layer(xf, g_ffn, wr[i], lt, m1, m3, m2, i, gfin,
                            final_norm=(layer == depth - 1))
    return xf.reshape(batch, seq, d)
```

```python
import functools

import jax
import jax.numpy as jnp
from jax import lax
from jax.experimental import pallas as pl
from jax.experimental.pallas import tpu as pltpu

F32 = jnp.float32
BF16 = jnp.bfloat16

D_MODEL = 1024
CONV_CH = 512
CONV_K = 31
ATT_HEADS = 8
HEAD_DIM = 64
ATT_WIDTH = ATT_HEADS * HEAD_DIM
SG_GROUPS = 8
SG_WIDTH = 512
SG_CHUNK = 128
N_BRANCH = 3
OFF_CONV = 0
OFF_Q = OFF_CONV + 2 * CONV_CH
OFF_K = OFF_Q + ATT_WIDTH
OFF_V = OFF_K + ATT_WIDTH
OFF_F = OFF_V + ATT_WIDTH
OFF_SG = OFF_F + ATT_HEADS
OFF_GATE = OFF_SG + 2 * SG_WIDTH
D_FF = 2816
N_EXPERTS = 8
D_FF_EXPERT = 3584
EPS = 1e-6

LANES = 128
SUBLANES = 8
MIB = 1024 * 1024

HEAD_PAD = LANES
P_K = 0
P_CONV = P_K + ATT_HEADS * HEAD_PAD
P_SG = P_CONV + 2 * CONV_CH
P_GATE = P_SG + 2 * SG_WIDTH
P_Q = P_GATE + N_BRANCH * D_MODEL
P_COLS = P_Q + ATT_HEADS * HEAD_PAD
N_SPLIT = 3
V_ROWS = HEAD_PAD
LOG2E = 1.4426950408889634

TM_PROJ = 1024
TN_PROJ = 1024
TM_BRANCH = 512
CONV_HIST = 32
CONV_RB = 64
TQ = 512
TK = 512
ATT_UNROLL = 14
TM_FFN = 1024
TF_FFN = 256
MOE_BLK = 512
TF_MOE = 512
TM_ROW = 1024


def _params(vmem_mib, n_axes):
    return pltpu.CompilerParams(dimension_semantics=("arbitrary",) * n_axes,
                                vmem_limit_bytes=vmem_mib * MIB)


def _rms(x, g):
    return x * lax.rsqrt(jnp.mean(x * x, axis=-1, keepdims=True) + EPS) * g


def _layer_norm(x, g, b):
    mu = jnp.mean(x, axis=-1, keepdims=True)
    d = x - mu
    var = jnp.mean(d * d, axis=-1, keepdims=True)
    return d * lax.rsqrt(var + EPS) * g + b


def _sigmoid(x):
    return 1.0 / (1.0 + jnp.exp(-x))


def _split3(x):
    hi = x.astype(BF16)
    r1 = x - hi.astype(F32)
    mid = r1.astype(BF16)
    lo = (r1 - mid.astype(F32)).astype(BF16)
    return hi, mid, lo


def _inproj_kernel(x_ref, g_ref, w_ref, sc_ref, bi_ref, wf_ref, bf_ref, lt_ref, pl_ref, wvt_ref,
                   vb_ref, p_ref, vt_ref, xn_ref, c3_ref, carry_ref, *, tiles_per_batch):
    i = pl.program_id(0)
    j = pl.program_id(1)
    tm = x_ref.shape[0]
    tk = vt_ref.shape[-1]

    @pl.when(jnp.logical_and(j == 0, i % tiles_per_batch == 0))
    def _():
        carry_ref[...] = jnp.zeros_like(carry_ref)

    @pl.when(j == 0)
    def _():
        xn = _rms(x_ref[...], g_ref[...]).astype(BF16)
        xn_ref[...] = xn
        f = jnp.dot(xn, wf_ref[...], preferred_element_type=F32) + bf_ref[...]
        ls = jnp.minimum(f, 0.0) - jnp.log1p(jnp.exp(-jnp.abs(f)))
        lt = lt_ref[...]
        terms = _split3(ls)
        offset = carry_ref[...]
        for blk in range(tm // LANES):
            rows = slice(blk * LANES, (blk + 1) * LANES)
            c = sum(jnp.dot(lt, term[rows], preferred_element_type=F32) for term in terms) + offset
            offset = c[LANES - 1:LANES, :]
            for r, term in enumerate(_split3(c * LOG2E)):
                c3_ref[rows, r * LANES:(r + 1) * LANES] = term
        carry_ref[...] = offset
        vt = lax.dot_general(wvt_ref[...], xn, (((1,), (1,)), ((), ())),
                             preferred_element_type=F32) + vb_ref[...]
        for ch in range(tm // tk):
            vt_ref[ch] = vt[:, ch * tk:(ch + 1) * tk].astype(BF16)
        decay = jnp.dot(c3_ref[...], pl_ref[...], preferred_element_type=F32)
        p_ref[...] = (jnp.dot(xn, w_ref[...], preferred_element_type=F32) + decay).astype(BF16)

    @pl.when(j > 0)
    def _():
        acc = jnp.dot(xn_ref[...], w_ref[...], preferred_element_type=F32)
        p_ref[...] = (acc * sc_ref[...] + bi_ref[...]).astype(BF16)


def _inproj(x, g, w, col_scale, col_bias, wf, bf, lt, place, wvt, vbias, seq, layer):
    n = x.shape[0]
    tm, tn = TM_PROJ, TN_PROJ
    assert P_K == 0 and tn == ATT_HEADS * HEAD_PAD
    kern = functools.partial(_inproj_kernel, tiles_per_batch=seq // tm)
    vt_rows = ATT_HEADS * V_ROWS
    const = lambda shape: pl.BlockSpec(shape, lambda i, j: (0,) * len(shape))
    return pl.pallas_call(
        kern,
        grid=(n // tm, P_COLS // tn),
        in_specs=[
            pl.BlockSpec((tm, D_MODEL), lambda i, j: (i, 0)),
            const((1, D_MODEL)),
            pl.BlockSpec((None, D_MODEL, tn), lambda i, j: (layer, 0, j)),
            pl.BlockSpec((1, tn), lambda i, j: (0, j)),
            pl.BlockSpec((1, tn), lambda i, j: (0, j)),
            pl.BlockSpec((None, D_MODEL, LANES), lambda i, j: (layer, 0, 0)),
            const((1, LANES)),
            const((LANES, LANES)),
            const((N_SPLIT * LANES, tn)),
            pl.BlockSpec((None, vt_rows, D_MODEL), lambda i, j: (layer, 0, 0)),
            const((vt_rows, 1)),
        ],
        out_specs=[
            pl.BlockSpec((tm, tn), lambda i, j: (i, j)),
            pl.BlockSpec((tm // TK, vt_rows, TK), lambda i, j: (i, 0, 0)),
        ],
        out_shape=[
            jax.ShapeDtypeStruct((n, P_COLS), BF16),
            jax.ShapeDtypeStruct((n // TK, vt_rows, TK), BF16),
        ],
        scratch_shapes=[pltpu.VMEM((tm, D_MODEL), BF16), pltpu.VMEM((tm, N_SPLIT * LANES), BF16),
                        pltpu.VMEM((1, LANES), F32)],
        compiler_params=_params(48, 2),
        name="inproj",
    )(x, g, w, col_scale, col_bias, wf, bf, lt, place, wvt, vbias)


def _conv_kernel(a1_ref, a2_ref, w_ref, cb_ref, g_ref, b_ref, o_ref, hext_ref, hsh_ref,
                 *, tiles_per_batch):
    i = pl.program_id(0)
    tm = a1_ref.shape[0]
    span = hsh_ref.shape[1]

    @pl.when(i % tiles_per_batch == 0)
    def _():
        hext_ref[0:CONV_HIST, :] = jnp.zeros((CONV_HIST, CONV_CH), F32)

    @pl.when(i % tiles_per_batch != 0)
    def _():
        hext_ref[0:CONV_HIST, :] = hext_ref[tm:tm + CONV_HIST, :]

    hext_ref[CONV_HIST:CONV_HIST + tm, :] = (
        a1_ref[...].astype(F32) * _sigmoid(a2_ref[...].astype(F32)))
    for s in range(1, SUBLANES):
        hsh_ref[s] = hext_ref[s:s + span, :]

    g = g_ref[...]
    b = b_ref[...]
    base = CONV_HIST - (CONV_K - 1)
    for r in range(0, tm, CONV_RB):
        acc = jnp.broadcast_to(cb_ref[...], (CONV_RB, CONV_CH))
        for j in range(CONV_K):
            s = (base + j) % SUBLANES
            a = r + base + j - s
            rows = hext_ref[a:a + CONV_RB, :] if s == 0 else hsh_ref[s, a:a + CONV_RB, :]
            acc = acc + w_ref[j:j + 1, :] * rows
        y = _layer_norm(acc, g, b)
        o_ref[r:r + CONV_RB, :] = (y * _sigmoid(y)).astype(BF16)


def _conv_branch(p, w, cb, g, b, seq):
    n = p.shape[0]
    tm = TM_BRANCH
    kern = functools.partial(_conv_kernel, tiles_per_batch=seq // tm)
    c0 = P_CONV // CONV_CH
    vec = pl.BlockSpec((1, CONV_CH), lambda i: (0, 0))
    return pl.pallas_call(
        kern,
        grid=(n // tm,),
        in_specs=[
            pl.BlockSpec((tm, CONV_CH), lambda i: (i, c0)),
            pl.BlockSpec((tm, CONV_CH), lambda i: (i, c0 + 1)),
            pl.BlockSpec((CONV_K + 1, CONV_CH), lambda i: (0, 0)),
            vec, vec, vec,
        ],
        out_specs=pl.BlockSpec((tm, CONV_CH), lambda i: (i, 0)),
        out_shape=jax.ShapeDtypeStruct((n, CONV_CH), BF16),
        scratch_shapes=[pltpu.VMEM((tm + CONV_HIST, CONV_CH), F32),
                        pltpu.VMEM((SUBLANES, tm + CONV_HIST - SUBLANES, CONV_CH), F32)],
        compiler_params=_params(32, 1),
        name="conv_branch",
    )(p, p, w, cb, g, b)


def _sg_kernel(u_ref, v_ref, g_ref, b_ref, w_ref, bias_ref, o_ref):
    tm = u_ref.shape[0]

    def gelu(z):
        return 0.5 * z * (1.0 + lax.erf(z * 0.7071067811865476))

    zu = gelu(u_ref[...].astype(F32))
    vn = _layer_norm(gelu(v_ref[...].astype(F32)), g_ref[...], b_ref[...]).astype(BF16)
    lane = lax.broadcasted_iota(jnp.int32, (SG_CHUNK, LANES), 1)
    first_group = lane < (SG_WIDTH // SG_GROUPS)
    for c in range(tm // SG_CHUNK):
        rows = slice(c * SG_CHUNK, (c + 1) * SG_CHUNK)
        for pr in range(SG_WIDTH // LANES):
            cols = slice(pr * LANES, (pr + 1) * LANES)
            vp = vn[rows, cols]
            m0 = jnp.dot(w_ref[2 * pr], vp, preferred_element_type=F32)
            m1 = jnp.dot(w_ref[2 * pr + 1], vp, preferred_element_type=F32)
            mixed = jnp.where(first_group, m0, m1) + bias_ref[:, cols]
            o_ref[rows, cols] = (zu[rows, cols] * mixed).astype(BF16)


def _sg_branch(p, g, b, w_tril, bias_full):
    n = p.shape[0]
    tm = TM_BRANCH
    c0 = P_SG // SG_WIDTH
    vec = pl.BlockSpec((1, SG_WIDTH), lambda i: (0, 0))
    return pl.pallas_call(
        _sg_kernel,
        grid=(n // tm,),
        in_specs=[
            pl.BlockSpec((tm, SG_WIDTH), lambda i: (i, c0)),
            pl.BlockSpec((tm, SG_WIDTH), lambda i: (i, c0 + 1)),
            vec, vec,
            pl.BlockSpec((SG_GROUPS, SG_CHUNK, SG_CHUNK), lambda i: (0, 0, 0)),
            pl.BlockSpec((SG_CHUNK, SG_WIDTH), lambda i: (0, 0)),
        ],
        out_specs=pl.BlockSpec((tm, SG_WIDTH), lambda i: (i, 0)),
        out_shape=jax.ShapeDtypeStruct((n, SG_WIDTH), BF16),
        compiler_params=_params(32, 1),
        name="sg_branch",
    )(p, p, g, b, w_tril, bias_full)


def _attn_kernel(ti_ref, tj_ref, q_ref, k_ref, vt_ref, o_ref, sa_ref, sb_ref, m_ref, acc_ref, tri_ref,
                 *, n_below, n_diag):
    tq = sa_ref.shape[-1]
    tk = vt_ref.shape[-1]
    head_cols = [slice(hh * HEAD_PAD, (hh + 1) * HEAD_PAD) for hh in range(2)]

    def scores(t, dst_ref):
        q0 = pl.multiple_of(ti_ref[t] * tq, tq)
        k0 = pl.multiple_of(tj_ref[t] * tk, tk)
        for hh, cols in enumerate(head_cols):
            dst_ref[hh] = lax.dot_general(k_ref[pl.ds(k0, tk), cols], q_ref[pl.ds(q0, tq), cols],
                                          (((1,), (1,)), ((), ())), preferred_element_type=F32)

    def consume(t, src_ref, masked):
        i = ti_ref[t]
        j = tj_ref[t]
        probs = []
        for hh in range(2):
            s = src_ref[hh]
            if masked:
                s = s + tri_ref[...]
            m = m_ref[i, hh]
            m_new = jnp.maximum(m, jnp.max(s, axis=0, keepdims=True))
            m_ref[i, hh] = m_new
            probs.append((jnp.exp2(m - m_new), jnp.exp2(s - m_new).astype(BF16)))
        for hh, (alpha, pexp) in enumerate(probs):
            pv = jnp.dot(vt_ref[j, hh * V_ROWS:(hh + 1) * V_ROWS, :], pexp, preferred_element_type=F32)
            acc_ref[i, hh] = alpha * acc_ref[i, hh] + pv

    def run(base, count, masked):
        bufs = (sa_ref, sb_ref)
        scores(base, bufs[0])

        def trip(u, carry):
            t = base + ATT_UNROLL * u
            for k in range(ATT_UNROLL):
                scores(t + k + 1, bufs[(k + 1) % 2])
                consume(t + k, bufs[k % 2], masked)
            return carry

        lax.fori_loop(0, count // ATT_UNROLL, trip, 0)
        done = count - count % ATT_UNROLL
        for k in range(done, count):
            if k + 1 < count:
                scores(base + k + 1, bufs[(k + 1) % 2])
            consume(base + k, bufs[k % 2], masked)

    tri_ref[...] = jnp.where(lax.broadcasted_iota(jnp.int32, (tk, tq), 0)
                             <= lax.broadcasted_iota(jnp.int32, (tk, tq), 1), 0.0, -jnp.inf)
    m_ref[...] = jnp.full(m_ref.shape, -jnp.inf, F32)
    acc_ref[...] = jnp.zeros(acc_ref.shape, F32)
    if n_below:
        run(0, n_below, False)
    run(n_below + 1, n_diag, True)

    eye = (lax.broadcasted_iota(jnp.int32, (tq, tq), 0)
           == lax.broadcasted_iota(jnp.int32, (tq, tq), 1)).astype(BF16)
    for i in range(q_ref.shape[0] // tq):
        even = acc_ref[i, 0]
        odd = acc_ref[i, 1]
        merged = jnp.concatenate([even[:HEAD_DIM] / even[HEAD_DIM:HEAD_DIM + 1, :],
                                  odd[V_ROWS - HEAD_DIM:] / odd[0:1, :]], axis=0)
        o_ref[i * tq:(i + 1) * tq, :] = lax.dot_general(
            eye, merged.astype(BF16), (((1,), (1,)), ((), ())),
            preferred_element_type=F32).astype(BF16)


def _attention(p, vt, batch, seq):
    n = p.shape[0]
    pairs = ATT_HEADS // 2
    assert TQ == TK
    nq = seq // TQ
    below = [(i, j) for i in range(nq) for j in range(i)]
    diag = [(i, i) for i in range(nq)]
    tiles = below + below[-1:] + diag + diag[-1:] if below else [(0, 0)] + diag + diag[-1:]
    ti = jnp.asarray([t[0] for t in tiles], jnp.int32)
    tj = jnp.asarray([t[1] for t in tiles], jnp.int32)
    pw = 2 * HEAD_PAD
    grid_spec = pltpu.PrefetchScalarGridSpec(
        num_scalar_prefetch=2,
        grid=(batch, pairs),
        in_specs=[
            pl.BlockSpec((seq, pw), lambda b, h, ti, tj: (b, P_Q // pw + h)),
            pl.BlockSpec((seq, pw), lambda b, h, ti, tj: (b, P_K // pw + h)),
            pl.BlockSpec((seq // TK, 2 * V_ROWS, TK), lambda b, h, ti, tj: (b, h, 0)),
        ],
        out_specs=pl.BlockSpec((seq, LANES), lambda b, h, ti, tj: (b, h)),
        scratch_shapes=[pltpu.VMEM((2, TK, TQ), F32), pltpu.VMEM((2, TK, TQ), F32),
                        pltpu.VMEM((nq, 2, 1, TQ), F32), pltpu.VMEM((nq, 2, V_ROWS, TQ), F32),
                        pltpu.VMEM((TK, TQ), F32)],
    )
    return pl.pallas_call(
        functools.partial(_attn_kernel, n_below=len(below), n_diag=len(diag)),
        grid_spec=grid_spec,
        out_shape=jax.ShapeDtypeStruct((n, ATT_WIDTH), BF16),
        compiler_params=_params(40, 2),
        name="fox_attention",
    )(ti, tj, p, p, vt)


def _merge_kernel(x_ref, hc_ref, ha_ref, hs_ref, g0_ref, g1_ref, g2_ref, bg_ref,
                  wc_ref, wa_ref, ws_ref, wm_ref, o_ref):
    merged = None
    for k, (h_ref, w_ref, gl_ref) in enumerate(((hc_ref, wc_ref, g0_ref), (ha_ref, wa_ref, g1_ref),
                                                (hs_ref, ws_ref, g2_ref))):
        y = jnp.dot(h_ref[...], w_ref[...], preferred_element_type=F32)
        gate = _sigmoid(gl_ref[...].astype(F32) + bg_ref[:, k * D_MODEL:(k + 1) * D_MODEL])
        merged = gate * y if merged is None else merged + gate * y
    o_ref[...] = x_ref[...] + jnp.dot(merged.astype(BF16), wm_ref[...], preferred_element_type=F32)


def _merge(x, hc, ha, hs, p, bg, wc, wa, ws, wm, layer):
    n = x.shape[0]
    tm = TM_BRANCH
    g0 = P_GATE // D_MODEL
    half = lambda: pl.BlockSpec((tm, CONV_CH), lambda i: (i, 0))
    wspec = lambda k: pl.BlockSpec((None, k, D_MODEL), lambda i: (layer, 0, 0))
    return pl.pallas_call(
        _merge_kernel,
        grid=(n // tm,),
        in_specs=[
            pl.BlockSpec((tm, D_MODEL), lambda i: (i, 0)),
            half(), half(), half(),
            pl.BlockSpec((tm, D_MODEL), lambda i: (i, g0)),
            pl.BlockSpec((tm, D_MODEL), lambda i: (i, g0 + 1)),
            pl.BlockSpec((tm, D_MODEL), lambda i: (i, g0 + 2)),
            pl.BlockSpec((1, N_BRANCH * D_MODEL), lambda i: (0, 0)),
            wspec(CONV_CH), wspec(ATT_WIDTH), wspec(SG_WIDTH), wspec(D_MODEL),
        ],
        out_specs=pl.BlockSpec((tm, D_MODEL), lambda i: (i, 0)),
        out_shape=jax.ShapeDtypeStruct((n, D_MODEL), F32),
        compiler_params=_params(48, 1),
        name="merge",
    )(x, hc, ha, hs, p, p, p, bg, wc, wa, ws, wm)


def _ffn_kernel(x_ref, g_ref, w1_ref, w3_ref, w2_ref, o_ref, hn_ref, acc_ref,
                wb1_ref, wb3_ref, wb2_ref):
    f = pl.program_id(1)

    @pl.when(pl.program_id(0) == 0)
    def _():
        wb1_ref[f] = w1_ref[...].astype(BF16)
        wb3_ref[f] = w3_ref[...].astype(BF16)
        wb2_ref[f] = w2_ref[...].astype(BF16)

    @pl.when(f == 0)
    def _():
        hn_ref[...] = _rms(x_ref[...], g_ref[...]).astype(BF16)
        acc_ref[...] = jnp.zeros_like(acc_ref)

    hn = hn_ref[...]
    a = jnp.dot(hn, wb1_ref[f], preferred_element_type=F32)
    b = jnp.dot(hn, wb3_ref[f], preferred_element_type=F32)
    t = (a * _sigmoid(a) * b).astype(BF16)
    acc_ref[...] += jnp.dot(t, wb2_ref[f], preferred_element_type=F32)

    @pl.when(f == pl.num_programs(1) - 1)
    def _():
        o_ref[...] = x_ref[...] + acc_ref[...]


def _ffn(x, g, w1, w3, w2, layer):
    n = x.shape[0]
    tm, tf = TM_FFN, TF_FFN
    nf = D_FF // tf

    def f_idx(i, f):
        return jnp.where(i == 0, f, nf - 1)

    return pl.pallas_call(
        _ffn_kernel,
        grid=(n // tm, nf),
        in_specs=[
            pl.BlockSpec((tm, D_MODEL), lambda i, f: (i, 0)),
            pl.BlockSpec((1, D_MODEL), lambda i, f: (0, 0)),
            pl.BlockSpec((None, D_MODEL, tf), lambda i, f: (layer, 0, f_idx(i, f))),
            pl.BlockSpec((None, D_MODEL, tf), lambda i, f: (layer, 0, f_idx(i, f))),
            pl.BlockSpec((None, tf, D_MODEL), lambda i, f: (layer, f_idx(i, f), 0)),
        ],
        out_specs=pl.BlockSpec((tm, D_MODEL), lambda i, f: (i, 0)),
        out_shape=jax.ShapeDtypeStruct((n, D_MODEL), F32),
        scratch_shapes=[pltpu.VMEM((tm, D_MODEL), BF16), pltpu.VMEM((tm, D_MODEL), F32),
                        pltpu.VMEM((nf, D_MODEL, tf), BF16), pltpu.VMEM((nf, D_MODEL, tf), BF16),
                        pltpu.VMEM((nf, tf, D_MODEL), BF16)],
        compiler_params=_params(56, 2),
        name="ffn_dense",
    )(x, g, w1, w3, w2)


def _router_kernel(x_ref, g_ref, wr_ref, lt_ref, h_ref, meta_ref, cnt_ref, carry_ref):
    i = pl.program_id(0)
    tm = x_ref.shape[0]

    @pl.when(i == 0)
    def _():
        carry_ref[...] = jnp.zeros_like(carry_ref)

    h = _rms(x_ref[...], g_ref[...])
    h_ref[...] = h
    h_hi = h.astype(BF16)
    h_lo = (h - h_hi.astype(F32)).astype(BF16)
    w = wr_ref[...]
    w_hi = w.astype(BF16)
    w_lo = (w - w_hi.astype(F32)).astype(BF16)
    logits = (jnp.dot(h_hi, w_hi, preferred_element_type=F32)
              + jnp.dot(h_hi, w_lo, preferred_element_type=F32)
              + jnp.dot(h_lo, w_hi, preferred_element_type=F32))
    lane = lax.broadcasted_iota(jnp.int32, (tm, LANES), 1)
    lanef = lane.astype(F32)
    lg = jnp.where(lane < N_EXPERTS, logits, -jnp.inf)
    v1 = jnp.max(lg, axis=1, keepdims=True)
    i1 = jnp.min(jnp.where(lg == v1, lanef, float(LANES)), axis=1, keepdims=True)
    lg2 = jnp.where(lanef == i1, -jnp.inf, lg)
    v2 = jnp.max(lg2, axis=1, keepdims=True)
    i2 = jnp.min(jnp.where(lg2 == v2, lanef, float(LANES)), axis=1, keepdims=True)
    e = jnp.exp(v2 - v1)
    g1 = 1.0 / (1.0 + e)
    g2 = e / (1.0 + e)
    oh1 = lanef == i1
    oh2 = lanef == i2
    cnt = (oh1.astype(F32) + oh2.astype(F32))
    before = jnp.dot(lt_ref[...], cnt.astype(BF16), preferred_element_type=F32) + carry_ref[...]
    r1 = jnp.sum(jnp.where(oh1, before, 0.0), axis=1, keepdims=True)
    r2 = jnp.sum(jnp.where(oh2, before, 0.0), axis=1, keepdims=True)
    total = carry_ref[...] + jnp.sum(cnt, axis=0, keepdims=True)
    carry_ref[...] = total
    cnt_ref[...] = jnp.broadcast_to(total, cnt_ref.shape)
    meta = jnp.zeros((tm, LANES), F32)
    for k, val in enumerate((i1, i2, g1, g2, r1, r2)):
        meta = jnp.where(lane == k, val, meta)
    meta_ref[...] = meta


def _router(x, g, wr, lt):
    n = x.shape[0]
    tm = TM_BRANCH
    return pl.pallas_call(
        _router_kernel,
        grid=(n // tm,),
        in_specs=[
            pl.BlockSpec((tm, D_MODEL), lambda i: (i, 0)),
            pl.BlockSpec((1, D_MODEL), lambda i: (0, 0)),
            pl.BlockSpec((D_MODEL, LANES), lambda i: (0, 0)),
            pl.BlockSpec((tm, tm), lambda i: (0, 0)),
        ],
        out_specs=[
            pl.BlockSpec((tm, D_MODEL), lambda i: (i, 0)),
            pl.BlockSpec((tm, LANES), lambda i: (i, 0)),
            pl.BlockSpec((8, LANES), lambda i: (0, 0)),
        ],
        out_shape=[
            jax.ShapeDtypeStruct((n, D_MODEL), F32),
            jax.ShapeDtypeStruct((n, LANES), F32),
            jax.ShapeDtypeStruct((8, LANES), F32),
        ],
        scratch_shapes=[pltpu.VMEM((1, LANES), F32)],
        compiler_params=_params(32, 1),
        name="router",
    )(x, g, wr, lt)


def _row_copy(src_ref, src_row, dst_ref, dst_row, sem):
    return pltpu.make_async_copy(src_ref.at[pl.ds(src_row, 1), :], dst_ref.at[pl.ds(dst_row, 1), :], sem)


def _dispatch_kernel(bounds_ref, dest_ref, h_ref, xs_ref, zero_ref, sem, zero_sem):
    tm = h_ref.shape[0]
    n_blk = xs_ref.shape[0] // MOE_BLK

    @pl.when(pl.program_id(0) == 0)
    def _():
        zero_ref[...] = jnp.zeros_like(zero_ref)

        def zero_block(blk):
            start = pl.multiple_of(blk * MOE_BLK, MOE_BLK)
            copy = pltpu.make_async_copy(zero_ref, xs_ref.at[pl.ds(start, MOE_BLK), :], zero_sem)
            copy.start()
            copy.wait()

        for e in range(N_EXPERTS):
            @pl.when(bounds_ref[e + 1] > bounds_ref[e])
            def _():
                zero_block(bounds_ref[e + 1] // MOE_BLK - 1)

        def unused(blk, carry):
            zero_block(blk)
            return carry

        lax.fori_loop(bounds_ref[N_EXPERTS] // MOE_BLK, n_blk, unused, 0)

    def issue(r, carry):
        for k in range(2):
            _row_copy(h_ref, r, xs_ref, dest_ref[2 * r + k], sem).start(priority=k)
        return carry

    lax.fori_loop(0, tm, issue, 0, unroll=8)
    for _ in range(2):
        pltpu.make_async_copy(h_ref, xs_ref.at[pl.ds(0, tm), :], sem).wait()


def _dispatch(bounds, dest_flat, h, n_rows):
    n = h.shape[0]
    tm = TM_ROW
    return pl.pallas_call(
        _dispatch_kernel,
        grid=(n // tm,),
        in_specs=[
            pl.BlockSpec(memory_space=pltpu.SMEM),
            pl.BlockSpec((2 * tm,), lambda i: (i,), memory_space=pltpu.SMEM),
            pl.BlockSpec((tm, D_MODEL), lambda i: (i, 0)),
        ],
        out_specs=pl.BlockSpec(memory_space=pl.ANY),
        out_shape=jax.ShapeDtypeStruct((n_rows, D_MODEL), F32),
        scratch_shapes=[pltpu.VMEM((MOE_BLK, D_MODEL), F32), pltpu.SemaphoreType.DMA(()),
                        pltpu.SemaphoreType.DMA(())],
        compiler_params=_params(32, 1),
        name="moe_dispatch",
    )(bounds, dest_flat, h)


def _first_block_of_expert(b, be_ref):
    return jnp.logical_or(b == 0, be_ref[b] != be_ref[jnp.maximum(b - 1, 0)])


def _moe_kernel(be_ref, nu_ref, x_ref, w1_ref, w3_ref, w2_ref, y_ref, xb_ref, acc_ref,
                wb1_ref, wb3_ref, wb2_ref):
    b = pl.program_id(0)
    f = pl.program_id(1)

    @pl.when(b < nu_ref[0])
    def _():
        @pl.when(_first_block_of_expert(b, be_ref))
        def _():
            wb1_ref[f] = w1_ref[...].astype(BF16)
            wb3_ref[f] = w3_ref[...].astype(BF16)
            wb2_ref[f] = w2_ref[...].astype(BF16)

        @pl.when(f == 0)
        def _():
            xb_ref[...] = x_ref[...].astype(BF16)
            acc_ref[...] = jnp.zeros_like(acc_ref)

        xb = xb_ref[...]
        a = jnp.dot(xb, wb1_ref[f], preferred_element_type=F32)
        c = jnp.dot(xb, wb3_ref[f], preferred_element_type=F32)
        t = (a * _sigmoid(a) * c).astype(BF16)
        acc_ref[...] += jnp.dot(t, wb2_ref[f], preferred_element_type=F32)

        @pl.when(f == pl.num_programs(1) - 1)
        def _():
            y_ref[...] = acc_ref[...]

    @pl.when(jnp.logical_and(b >= nu_ref[0], f == pl.num_programs(1) - 1))
    def _():
        y_ref[...] = jnp.zeros_like(y_ref)


def _moe_experts(blk_e, n_used, xs, w1, w3, w2, layer):
    rows = xs.shape[0]
    n_blk = rows // MOE_BLK
    nf = D_FF_EXPERT // TF_MOE

    def row_idx(b, f, be, nu):
        return (jnp.minimum(b, nu[0] - 1), 0)

    def f_idx(b, f, be, nu):
        fetch = jnp.logical_and(b < nu[0], _first_block_of_expert(b, be))
        return jnp.where(fetch, f, nf - 1)

    grid_spec = pltpu.PrefetchScalarGridSpec(
        num_scalar_prefetch=2,
        grid=(n_blk, nf),
        in_specs=[
            pl.BlockSpec((MOE_BLK, D_MODEL), row_idx),
            pl.BlockSpec((None, None, D_MODEL, TF_MOE),
                         lambda b, f, be, nu: (layer, be[b], 0, f_idx(b, f, be, nu))),
            pl.BlockSpec((None, None, D_MODEL, TF_MOE),
                         lambda b, f, be, nu: (layer, be[b], 0, f_idx(b, f, be, nu))),
            pl.BlockSpec((None, None, TF_MOE, D_MODEL),
                         lambda b, f, be, nu: (layer, be[b], f_idx(b, f, be, nu), 0)),
        ],
        out_specs=pl.BlockSpec((MOE_BLK, D_MODEL), lambda b, f, be, nu: (b, 0)),
        scratch_shapes=[pltpu.VMEM((MOE_BLK, D_MODEL), BF16), pltpu.VMEM((MOE_BLK, D_MODEL), F32),
                        pltpu.VMEM((nf, D_MODEL, TF_MOE), BF16), pltpu.VMEM((nf, D_MODEL, TF_MOE), BF16),
                        pltpu.VMEM((nf, TF_MOE, D_MODEL), BF16)],
    )
    return pl.pallas_call(
        _moe_kernel,
        grid_spec=grid_spec,
        out_shape=jax.ShapeDtypeStruct((rows, D_MODEL), F32),
        compiler_params=_params(56, 2),
        name="moe_experts",
    )(blk_e, n_used, xs, w1, w3, w2)


def _combine_kernel(dest_ref, x_ref, meta_ref, gfin_ref, y_ref, o_ref, ybuf_ref, sem, *, final_norm):
    tm = x_ref.shape[0]

    def issue(r, carry):
        for k in range(2):
            _row_copy(y_ref, dest_ref[2 * r + k], ybuf_ref.at[k], r, sem).start(priority=k)
        return carry

    lax.fori_loop(0, tm, issue, 0, unroll=8)
    for k in range(2):
        pltpu.make_async_copy(y_ref.at[pl.ds(0, tm), :], ybuf_ref.at[k], sem).wait()

    g1 = meta_ref[:, 2:3]
    g2 = meta_ref[:, 3:4]
    out = x_ref[...] + (g1 * ybuf_ref[0] + g2 * ybuf_ref[1])
    if final_norm:
        out = _rms(out, gfin_ref[...])
    o_ref[...] = out


def _combine(dest_flat, x, meta, gfin, y, final_norm):
    n = x.shape[0]
    tm = TM_ROW
    kern = functools.partial(_combine_kernel, final_norm=final_norm)
    return pl.pallas_call(
        kern,
        grid=(n // tm,),
        in_specs=[
            pl.BlockSpec((2 * tm,), lambda i: (i,), memory_space=pltpu.SMEM),
            pl.BlockSpec((tm, D_MODEL), lambda i: (i, 0)),
            pl.BlockSpec((tm, LANES), lambda i: (i, 0)),
            pl.BlockSpec((1, D_MODEL), lambda i: (0, 0)),
            pl.BlockSpec(memory_space=pl.ANY),
        ],
        out_specs=pl.BlockSpec((tm, D_MODEL), lambda i: (i, 0)),
        out_shape=jax.ShapeDtypeStruct((n, D_MODEL), F32),
        scratch_shapes=[pltpu.VMEM((2, tm, D_MODEL), F32), pltpu.SemaphoreType.DMA(())],
        compiler_params=_params(32, 1),
        name="moe_combine",
    )(dest_flat, x, meta, gfin, y)


def _moe_layer(x, g, wr, lt, w1, w3, w2, layer, gfin, final_norm):
    n = x.shape[0]
    h, meta, cnt = _router(x, g, wr, lt)
    expert = meta[:, 0:2].astype(jnp.int32)
    rank = meta[:, 4:6].astype(jnp.int32)
    counts = cnt[0, :N_EXPERTS].astype(jnp.int32)
    padded = (counts + MOE_BLK - 1) // MOE_BLK * MOE_BLK
    pad_end = jnp.cumsum(padded)
    pad_start = pad_end - padded
    dest = (pad_start[expert] + rank).reshape(-1)
    n_blk = (2 * n) // MOE_BLK + N_EXPERTS
    blk_start = jnp.arange(n_blk, dtype=jnp.int32) * MOE_BLK
    blk_e = jnp.minimum(jnp.sum(blk_start[:, None] >= pad_end[None, :], axis=1), N_EXPERTS - 1)
    n_used = (pad_end[-1:] // MOE_BLK).astype(jnp.int32)
    bounds = jnp.concatenate([jnp.zeros((1,), jnp.int32), pad_end.astype(jnp.int32)])
    xs = _dispatch(bounds, dest, h, n_blk * MOE_BLK)
    y = _moe_experts(blk_e.astype(jnp.int32), n_used, xs, w1, w3, w2, layer)
    return _combine(dest, x, meta, gfin, y, final_norm)


def _prepare_mixer(w_in, b_forget, conv_w, w_spatial, b_spatial, w_conv_out, w_att_out, w_sg_out,
                   w_mix_out):
    depth, d, _ = w_in.shape

    def pad_heads(w):
        w = w.reshape(depth, d, ATT_HEADS, HEAD_DIM)
        w = jnp.pad(w, ((0, 0), (0, 0), (0, 0), (0, HEAD_PAD - HEAD_DIM)))
        return w.reshape(depth, d, ATT_HEADS * HEAD_PAD)

    prep = {}
    prep["w_main"] = jnp.concatenate(
        [pad_heads(w_in[:, :, OFF_K:OFF_V]), w_in[:, :, OFF_CONV:OFF_Q], w_in[:, :, OFF_SG:OFF_GATE],
         w_in[:, :, OFF_GATE:], pad_heads(w_in[:, :, OFF_Q:OFF_K])], axis=2).astype(BF16)
    col = jnp.arange(P_COLS)
    is_q = col >= P_Q
    spare = (col % HEAD_PAD >= HEAD_DIM) & (col % HEAD_PAD < HEAD_DIM + N_SPLIT)
    prep["col_scale"] = jnp.where(is_q, LOG2E * HEAD_DIM ** -0.5, 1.0).astype(F32)[None, :]
    prep["col_bias"] = jnp.where(is_q & spare, 1.0, 0.0).astype(F32)[None, :]
    kcol = jnp.arange(ATT_HEADS * HEAD_PAD)
    place = -((kcol[None, None, :] // HEAD_PAD == jnp.arange(LANES)[None, :, None])
              & (kcol[None, None, :] % HEAD_PAD == HEAD_DIM + jnp.arange(N_SPLIT)[:, None, None])
              ).astype(BF16)
    prep["place"] = place.reshape(N_SPLIT * LANES, ATT_HEADS * HEAD_PAD)
    prep["wf"] = jnp.pad(w_in[:, :, OFF_F:OFF_SG], ((0, 0), (0, 0), (0, LANES - ATT_HEADS))).astype(BF16)
    prep["bf"] = jnp.pad(b_forget, ((0, 0), (0, LANES - ATT_HEADS)))[:, None, :]
    wv_t = jnp.swapaxes(w_in[:, :, OFF_V:OFF_F], 1, 2).reshape(depth, ATT_HEADS // 2, 2, HEAD_DIM, d)
    zeros_h = jnp.zeros_like(wv_t[:, :, 0, :V_ROWS - HEAD_DIM])
    wvt = jnp.concatenate([wv_t[:, :, 0], zeros_h, zeros_h, wv_t[:, :, 1]], axis=2)
    prep["wvt"] = wvt.reshape(depth, ATT_HEADS * V_ROWS, d).astype(BF16)
    vrow = jnp.arange(ATT_HEADS * V_ROWS) % (2 * V_ROWS)
    prep["vbias"] = ((vrow == HEAD_DIM) | (vrow == V_ROWS)).astype(F32)[:, None]
    prep["lt_incl"] = jnp.tril(jnp.ones((LANES, LANES), F32)).astype(BF16)
    prep["conv_w"] = jnp.pad(conv_w, ((0, 0), (0, 1), (0, 0)))
    causal = jnp.tril(jnp.ones((SG_CHUNK, SG_CHUNK), bool))
    prep["w_sp"] = jnp.where(causal[None, None], w_spatial, 0).astype(BF16)
    prep["sg_bias"] = jnp.repeat(jnp.swapaxes(b_spatial, 1, 2), SG_WIDTH // SG_GROUPS, axis=2)
    for name, w in (("wc", w_conv_out), ("wa", w_att_out), ("ws", w_sg_out), ("wm", w_mix_out)):
        prep[name] = w.astype(BF16)
    return prep


def _token_mixer(xf, prep, vecs, layer, batch, seq):
    row = lambda name: vecs[name][layer][None, :]
    p, vt = _inproj(xf, row("mix_norm_g"), prep["w_main"], prep["col_scale"], prep["col_bias"],
                    prep["wf"], prep["bf"][layer], prep["lt_incl"], prep["place"],
                    prep["wvt"], prep["vbias"], seq, layer)
    hc = _conv_branch(p, prep["conv_w"][layer], row("conv_b"), row("conv_ln_g"), row("conv_ln_b"), seq)
    hs = _sg_branch(p, row("sg_ln_g"), row("sg_ln_b"), prep["w_sp"][layer], prep["sg_bias"][layer])
    ha = _attention(p, vt, batch, seq)
    return _merge(xf, hc, ha, hs, p, row("b_gate"), prep["wc"], prep["wa"], prep["ws"], prep["wm"], layer)


def kernel(x, mix_norm_g, w_in, b_forget, b_gate, conv_w, conv_b, conv_ln_g, conv_ln_b, w_conv_out,
           w_att_out, sg_ln_g, sg_ln_b, w_spatial, b_spatial, w_sg_out, w_mix_out, ffn_norm_g,
           ffn_w1, ffn_w3, ffn_w2, router_w, moe_w1, moe_w3, moe_w2, final_norm_g):
    batch, seq, d = x.shape
    depth = w_in.shape[0]
    if depth % 2 == 1:
        raise NotImplementedError("the final norm is fused into the last (expert) layer")
    xf = x.reshape(batch * seq, d)

    prep = _prepare_mixer(w_in, b_forget, conv_w, w_spatial, b_spatial, w_conv_out, w_att_out,
                          w_sg_out, w_mix_out)
    vecs = dict(mix_norm_g=mix_norm_g, conv_b=conv_b, conv_ln_g=conv_ln_g, conv_ln_b=conv_ln_b,
                sg_ln_g=sg_ln_g, sg_ln_b=sg_ln_b, b_gate=b_gate)
    f1, f3, f2 = ffn_w1, ffn_w3, ffn_w2
    m1, m3, m2 = moe_w1, moe_w3, moe_w2
    wr = jnp.pad(router_w, ((0, 0), (0, 0), (0, LANES - N_EXPERTS)))
    lt = jnp.tril(jnp.ones((TM_BRANCH, TM_BRANCH), F32), -1).astype(BF16)
    gfin = final_norm_g[None, :]

    for layer in range(depth):
        xf = _token_mixer(xf, prep, vecs, layer, batch, seq)
        g_ffn = ffn_norm_g[layer][None, :]
        i = layer // 2
        if layer % 2 == 0:
            xf = _ffn(xf, g_ffn, f1, f3, f2, i)
        else:
            xf = _moe_layer(xf, g_ffn, wr[i], lt, m1, m3, m2, i, gfin,
                            final_norm=(layer == depth - 1))
    return xf.reshape(batch, seq, d)
```

```python
import functools

import jax
import jax.numpy as jnp
from jax import lax
from jax.experimental import pallas as pl
from jax.experimental.pallas import tpu as pltpu

F32 = jnp.float32
BF16 = jnp.bfloat16

D_MODEL = 1024
CONV_CH = 512
CONV_K = 31
ATT_HEADS = 8
HEAD_DIM = 64
ATT_WIDTH = ATT_HEADS * HEAD_DIM
SG_GROUPS = 8
SG_WIDTH = 512
SG_CHUNK = 128
N_BRANCH = 3
OFF_CONV = 0
OFF_Q = OFF_CONV + 2 * CONV_CH
OFF_K = OFF_Q + ATT_WIDTH
OFF_V = OFF_K + ATT_WIDTH
OFF_F = OFF_V + ATT_WIDTH
OFF_SG = OFF_F + ATT_HEADS
OFF_GATE = OFF_SG + 2 * SG_WIDTH
D_FF = 2816
N_EXPERTS = 8
D_FF_EXPERT = 3584
EPS = 1e-6

LANES = 128
SUBLANES = 8
MIB = 1024 * 1024

HEAD_PAD = LANES
P_K = 0
P_CONV = P_K + ATT_HEADS * HEAD_PAD
P_SG = P_CONV + 2 * CONV_CH
P_GATE = P_SG + 2 * SG_WIDTH
P_Q = P_GATE + N_BRANCH * D_MODEL
P_COLS = P_Q + ATT_HEADS * HEAD_PAD
N_SPLIT = 3
V_ROWS = HEAD_PAD
LOG2E = 1.4426950408889634

TM_PROJ = 1024
TN_PROJ = 1024
TM_BRANCH = 512
CONV_HIST = 32
CONV_RB = 64
TQ = 512
TK = 512
ATT_UNROLL = 14
TM_FFN = 1024
TF_FFN = 256
MOE_BLK = 512
TF_MOE = 512
TM_ROW = 1024


def _params(vmem_mib, n_axes):
    return pltpu.CompilerParams(dimension_semantics=("arbitrary",) * n_axes,
                                vmem_limit_bytes=vmem_mib * MIB)


def _rms(x, g):
    return x * lax.rsqrt(jnp.mean(x * x, axis=-1, keepdims=True) + EPS) * g


def _layer_norm(x, g, b):
    mu = jnp.mean(x, axis=-1, keepdims=True)
    d = x - mu
    var = jnp.mean(d * d, axis=-1, keepdims=True)
    return d * lax.rsqrt(var + EPS) * g + b


def _sigmoid(x):
    return 1.0 / (1.0 + jnp.exp(-x))


def _split3(x):
    hi = x.astype(BF16)
    r1 = x - hi.astype(F32)
    mid = r1.astype(BF16)
    lo = (r1 - mid.astype(F32)).astype(BF16)
    return hi, mid, lo


def _inproj_kernel(x_ref, g_ref, w_ref, sc_ref, bi_ref, wf_ref, bf_ref, lt_ref, pl_ref, wvt_ref,
                   vb_ref, p_ref, vt_ref, xn_ref, c3_ref, carry_ref, wres_ref, *, tiles_per_batch):
    i = pl.program_id(0)
    j = pl.program_id(1)
    tm = x_ref.shape[0]
    tk = vt_ref.shape[-1]

    @pl.when(i == 0)
    def _():
        wres_ref[j] = w_ref[...]

    @pl.when(jnp.logical_and(j == 0, i % tiles_per_batch == 0))
    def _():
        carry_ref[...] = jnp.zeros_like(carry_ref)

    @pl.when(j == 0)
    def _():
        xn = _rms(x_ref[...], g_ref[...]).astype(BF16)
        xn_ref[...] = xn
        f = jnp.dot(xn, wf_ref[...], preferred_element_type=F32) + bf_ref[...]
        ls = jnp.minimum(f, 0.0) - jnp.log1p(jnp.exp(-jnp.abs(f)))
        lt = lt_ref[...]
        terms = _split3(ls)
        offset = carry_ref[...]
        for blk in range(tm // LANES):
            rows = slice(blk * LANES, (blk + 1) * LANES)
            c = sum(jnp.dot(lt, term[rows], preferred_element_type=F32) for term in terms) + offset
            offset = c[LANES - 1:LANES, :]
            for r, term in enumerate(_split3(c * LOG2E)):
                c3_ref[rows, r * LANES:(r + 1) * LANES] = term
        carry_ref[...] = offset
        vt = lax.dot_general(wvt_ref[...], xn, (((1,), (1,)), ((), ())),
                             preferred_element_type=F32) + vb_ref[...]
        for ch in range(tm // tk):
            vt_ref[ch] = vt[:, ch * tk:(ch + 1) * tk].astype(BF16)
        decay = jnp.dot(c3_ref[...], pl_ref[...], preferred_element_type=F32)
        p_ref[...] = (jnp.dot(xn, wres_ref[0], preferred_element_type=F32) + decay).astype(BF16)

    @pl.when(j > 0)
    def _():
        acc = jnp.dot(xn_ref[...], wres_ref[j], preferred_element_type=F32)
        p_ref[...] = (acc * sc_ref[...] + bi_ref[...]).astype(BF16)


def _inproj(x, g, w, col_scale, col_bias, wf, bf, lt, place, wvt, vbias, seq, layer):
    n = x.shape[0]
    tm, tn = TM_PROJ, TN_PROJ
    assert P_K == 0 and tn == ATT_HEADS * HEAD_PAD
    kern = functools.partial(_inproj_kernel, tiles_per_batch=seq // tm)
    vt_rows = ATT_HEADS * V_ROWS
    n_col = P_COLS // tn
    const = lambda shape: pl.BlockSpec(shape, lambda i, j: (0,) * len(shape))
    return pl.pallas_call(
        kern,
        grid=(n // tm, n_col),
        in_specs=[
            pl.BlockSpec((tm, D_MODEL), lambda i, j: (i, 0)),
            const((1, D_MODEL)),
            pl.BlockSpec((None, D_MODEL, tn), lambda i, j: (layer, 0, jnp.where(i == 0, j, n_col - 1))),
            pl.BlockSpec((1, tn), lambda i, j: (0, j)),
            pl.BlockSpec((1, tn), lambda i, j: (0, j)),
            pl.BlockSpec((None, D_MODEL, LANES), lambda i, j: (layer, 0, 0)),
            const((1, LANES)),
            const((LANES, LANES)),
            const((N_SPLIT * LANES, tn)),
            pl.BlockSpec((None, vt_rows, D_MODEL), lambda i, j: (layer, 0, 0)),
            const((vt_rows, 1)),
        ],
        out_specs=[
            pl.BlockSpec((tm, tn), lambda i, j: (i, j)),
            pl.BlockSpec((tm // TK, vt_rows, TK), lambda i, j: (i, 0, 0)),
        ],
        out_shape=[
            jax.ShapeDtypeStruct((n, P_COLS), BF16),
            jax.ShapeDtypeStruct((n // TK, vt_rows, TK), BF16),
        ],
        scratch_shapes=[pltpu.VMEM((tm, D_MODEL), BF16), pltpu.VMEM((tm, N_SPLIT * LANES), BF16),
                        pltpu.VMEM((1, LANES), F32), pltpu.VMEM((n_col, D_MODEL, tn), BF16)],
        compiler_params=_params(56, 2),
        name="inproj",
    )(x, g, w, col_scale, col_bias, wf, bf, lt, place, wvt, vbias)


def _conv_kernel(a1_ref, a2_ref, w_ref, cb_ref, g_ref, b_ref, o_ref, hext_ref, hsh_ref,
                 *, tiles_per_batch):
    i = pl.program_id(0)
    tm = a1_ref.shape[0]
    span = hsh_ref.shape[1]

    @pl.when(i % tiles_per_batch == 0)
    def _():
        hext_ref[0:CONV_HIST, :] = jnp.zeros((CONV_HIST, CONV_CH), F32)

    @pl.when(i % tiles_per_batch != 0)
    def _():
        hext_ref[0:CONV_HIST, :] = hext_ref[tm:tm + CONV_HIST, :]

    hext_ref[CONV_HIST:CONV_HIST + tm, :] = (
        a1_ref[...].astype(F32) * _sigmoid(a2_ref[...].astype(F32)))
    for s in range(1, SUBLANES):
        hsh_ref[s] = hext_ref[s:s + span, :]

    g = g_ref[...]
    b = b_ref[...]
    base = CONV_HIST - (CONV_K - 1)
    for r in range(0, tm, CONV_RB):
        acc = jnp.broadcast_to(cb_ref[...], (CONV_RB, CONV_CH))
        for j in range(CONV_K):
            s = (base + j) % SUBLANES
            a = r + base + j - s
            rows = hext_ref[a:a + CONV_RB, :] if s == 0 else hsh_ref[s, a:a + CONV_RB, :]
            acc = acc + w_ref[j:j + 1, :] * rows
        y = _layer_norm(acc, g, b)
        o_ref[r:r + CONV_RB, :] = (y * _sigmoid(y)).astype(BF16)


def _conv_branch(p, w, cb, g, b, seq):
    n = p.shape[0]
    tm = TM_BRANCH
    kern = functools.partial(_conv_kernel, tiles_per_batch=seq // tm)
    c0 = P_CONV // CONV_CH
    vec = pl.BlockSpec((1, CONV_CH), lambda i: (0, 0))
    return pl.pallas_call(
        kern,
        grid=(n // tm,),
        in_specs=[
            pl.BlockSpec((tm, CONV_CH), lambda i: (i, c0)),
            pl.BlockSpec((tm, CONV_CH), lambda i: (i, c0 + 1)),
            pl.BlockSpec((CONV_K + 1, CONV_CH), lambda i: (0, 0)),
            vec, vec, vec,
        ],
        out_specs=pl.BlockSpec((tm, CONV_CH), lambda i: (i, 0)),
        out_shape=jax.ShapeDtypeStruct((n, CONV_CH), BF16),
        scratch_shapes=[pltpu.VMEM((tm + CONV_HIST, CONV_CH), F32),
                        pltpu.VMEM((SUBLANES, tm + CONV_HIST - SUBLANES, CONV_CH), F32)],
        compiler_params=_params(32, 1),
        name="conv_branch",
    )(p, p, w, cb, g, b)


def _sg_kernel(u_ref, v_ref, g_ref, b_ref, w_ref, bias_ref, o_ref):
    tm = u_ref.shape[0]

    def gelu(z):
        return 0.5 * z * (1.0 + lax.erf(z * 0.7071067811865476))

    zu = gelu(u_ref[...].astype(F32))
    vn = _layer_norm(gelu(v_ref[...].astype(F32)), g_ref[...], b_ref[...]).astype(BF16)
    lane = lax.broadcasted_iota(jnp.int32, (SG_CHUNK, LANES), 1)
    first_group = lane < (SG_WIDTH // SG_GROUPS)
    for c in range(tm // SG_CHUNK):
        rows = slice(c * SG_CHUNK, (c + 1) * SG_CHUNK)
        for pr in range(SG_WIDTH // LANES):
            cols = slice(pr * LANES, (pr + 1) * LANES)
            vp = vn[rows, cols]
            m0 = jnp.dot(w_ref[2 * pr], vp, preferred_element_type=F32)
            m1 = jnp.dot(w_ref[2 * pr + 1], vp, preferred_element_type=F32)
            mixed = jnp.where(first_group, m0, m1) + bias_ref[:, cols]
            o_ref[rows, cols] = (zu[rows, cols] * mixed).astype(BF16)


def _sg_branch(p, g, b, w_tril, bias_full):
    n = p.shape[0]
    tm = TM_BRANCH
    c0 = P_SG // SG_WIDTH
    vec = pl.BlockSpec((1, SG_WIDTH), lambda i: (0, 0))
    return pl.pallas_call(
        _sg_kernel,
        grid=(n // tm,),
        in_specs=[
            pl.BlockSpec((tm, SG_WIDTH), lambda i: (i, c0)),
            pl.BlockSpec((tm, SG_WIDTH), lambda i: (i, c0 + 1)),
            vec, vec,
            pl.BlockSpec((SG_GROUPS, SG_CHUNK, SG_CHUNK), lambda i: (0, 0, 0)),
            pl.BlockSpec((SG_CHUNK, SG_WIDTH), lambda i: (0, 0)),
        ],
        out_specs=pl.BlockSpec((tm, SG_WIDTH), lambda i: (i, 0)),
        out_shape=jax.ShapeDtypeStruct((n, SG_WIDTH), BF16),
        compiler_params=_params(32, 1),
        name="sg_branch",
    )(p, p, g, b, w_tril, bias_full)


def _attn_kernel(ti_ref, tj_ref, q_ref, k_ref, vt_ref, o_ref, sa_ref, sb_ref, m_ref, acc_ref, tri_ref,
                 *, n_below, n_diag):
    tq = sa_ref.shape[-1]
    tk = vt_ref.shape[-1]
    head_cols = [slice(hh * HEAD_PAD, (hh + 1) * HEAD_PAD) for hh in range(2)]

    def scores(t, dst_ref):
        q0 = pl.multiple_of(ti_ref[t] * tq, tq)
        k0 = pl.multiple_of(tj_ref[t] * tk, tk)
        for hh, cols in enumerate(head_cols):
            dst_ref[hh] = lax.dot_general(k_ref[pl.ds(k0, tk), cols], q_ref[pl.ds(q0, tq), cols],
                                          (((1,), (1,)), ((), ())), preferred_element_type=F32)

    def consume(t, src_ref, masked):
        i = ti_ref[t]
        j = tj_ref[t]
        probs = []
        for hh in range(2):
            s = src_ref[hh]
            if masked:
                s = s + tri_ref[...]
            m = m_ref[i, hh]
            m_new = jnp.maximum(m, jnp.max(s, axis=0, keepdims=True))
            m_ref[i, hh] = m_new
            probs.append((jnp.exp2(m - m_new), jnp.exp2(s - m_new).astype(BF16)))
        for hh, (alpha, pexp) in enumerate(probs):
            pv = jnp.dot(vt_ref[j, hh * V_ROWS:(hh + 1) * V_ROWS, :], pexp, preferred_element_type=F32)
            acc_ref[i, hh] = alpha * acc_ref[i, hh] + pv

    def run(base, count, masked):
        bufs = (sa_ref, sb_ref)
        scores(base, bufs[0])

        def trip(u, carry):
            t = base + ATT_UNROLL * u
            for k in range(ATT_UNROLL):
                scores(t + k + 1, bufs[(k + 1) % 2])
                consume(t + k, bufs[k % 2], masked)
            return carry

        lax.fori_loop(0, count // ATT_UNROLL, trip, 0)
        done = count - count % ATT_UNROLL
        for k in range(done, count):
            if k + 1 < count:
                scores(base + k + 1, bufs[(k + 1) % 2])
            consume(base + k, bufs[k % 2], masked)

    tri_ref[...] = jnp.where(lax.broadcasted_iota(jnp.int32, (tk, tq), 0)
                             <= lax.broadcasted_iota(jnp.int32, (tk, tq), 1), 0.0, -jnp.inf)
    m_ref[...] = jnp.full(m_ref.shape, -jnp.inf, F32)
    acc_ref[...] = jnp.zeros(acc_ref.shape, F32)
    if n_below:
        run(0, n_below, False)
    run(n_below + 1, n_diag, True)

    eye = (lax.broadcasted_iota(jnp.int32, (tq, tq), 0)
           == lax.broadcasted_iota(jnp.int32, (tq, tq), 1)).astype(BF16)
    for i in range(q_ref.shape[0] // tq):
        even = acc_ref[i, 0]
        odd = acc_ref[i, 1]
        merged = jnp.concatenate([even[:HEAD_DIM] / even[HEAD_DIM:HEAD_DIM + 1, :],
                                  odd[V_ROWS - HEAD_DIM:] / odd[0:1, :]], axis=0)
        o_ref[i * tq:(i + 1) * tq, :] = lax.dot_general(
            eye, merged.astype(BF16), (((1,), (1,)), ((), ())),
            preferred_element_type=F32).astype(BF16)


def _attention(p, vt, batch, seq):
    n = p.shape[0]
    pairs = ATT_HEADS // 2
    assert TQ == TK
    nq = seq // TQ
    below = [(i, j) for i in range(nq) for j in range(i)]
    diag = [(i, i) for i in range(nq)]
    tiles = below + below[-1:] + diag + diag[-1:] if below else [(0, 0)] + diag + diag[-1:]
    ti = jnp.asarray([t[0] for t in tiles], jnp.int32)
    tj = jnp.asarray([t[1] for t in tiles], jnp.int32)
    pw = 2 * HEAD_PAD
    grid_spec = pltpu.PrefetchScalarGridSpec(
        num_scalar_prefetch=2,
        grid=(batch, pairs),
        in_specs=[
            pl.BlockSpec((seq, pw), lambda b, h, ti, tj: (b, P_Q // pw + h)),
            pl.BlockSpec((seq, pw), lambda b, h, ti, tj: (b, P_K // pw + h)),
            pl.BlockSpec((seq // TK, 2 * V_ROWS, TK), lambda b, h, ti, tj: (b, h, 0)),
        ],
        out_specs=pl.BlockSpec((seq, LANES), lambda b, h, ti, tj: (b, h)),
        scratch_shapes=[pltpu.VMEM((2, TK, TQ), F32), pltpu.VMEM((2, TK, TQ), F32),
                        pltpu.VMEM((nq, 2, 1, TQ), F32), pltpu.VMEM((nq, 2, V_ROWS, TQ), F32),
                        pltpu.VMEM((TK, TQ), F32)],
    )
    return pl.pallas_call(
        functools.partial(_attn_kernel, n_below=len(below), n_diag=len(diag)),
        grid_spec=grid_spec,
        out_shape=jax.ShapeDtypeStruct((n, ATT_WIDTH), BF16),
        compiler_params=_params(40, 2),
        name="fox_attention",
    )(ti, tj, p, p, vt)


def _merge_kernel(x_ref, hc_ref, ha_ref, hs_ref, g0_ref, g1_ref, g2_ref, bg_ref,
                  wc_ref, wa_ref, ws_ref, wm_ref, o_ref):
    merged = None
    for k, (h_ref, w_ref, gl_ref) in enumerate(((hc_ref, wc_ref, g0_ref), (ha_ref, wa_ref, g1_ref),
                                                (hs_ref, ws_ref, g2_ref))):
        y = jnp.dot(h_ref[...], w_ref[...], preferred_element_type=F32)
        gate = _sigmoid(gl_ref[...].astype(F32) + bg_ref[:, k * D_MODEL:(k + 1) * D_MODEL])
        merged = gate * y if merged is None else merged + gate * y
    o_ref[...] = x_ref[...] + jnp.dot(merged.astype(BF16), wm_ref[...], preferred_element_type=F32)


def _merge(x, hc, ha, hs, p, bg, wc, wa, ws, wm, layer):
    n = x.shape[0]
    tm = TM_BRANCH
    g0 = P_GATE // D_MODEL
    half = lambda: pl.BlockSpec((tm, CONV_CH), lambda i: (i, 0))
    wspec = lambda k: pl.BlockSpec((None, k, D_MODEL), lambda i: (layer, 0, 0))
    return pl.pallas_call(
        _merge_kernel,
        grid=(n // tm,),
        in_specs=[
            pl.BlockSpec((tm, D_MODEL), lambda i: (i, 0)),
            half(), half(), half(),
            pl.BlockSpec((tm, D_MODEL), lambda i: (i, g0)),
            pl.BlockSpec((tm, D_MODEL), lambda i: (i, g0 + 1)),
            pl.BlockSpec((tm, D_MODEL), lambda i: (i, g0 + 2)),
            pl.BlockSpec((1, N_BRANCH * D_MODEL), lambda i: (0, 0)),
            wspec(CONV_CH), wspec(ATT_WIDTH), wspec(SG_WIDTH), wspec(D_MODEL),
        ],
        out_specs=pl.BlockSpec((tm, D_MODEL), lambda i: (i, 0)),
        out_shape=jax.ShapeDtypeStruct((n, D_MODEL), F32),
        compiler_params=_params(48, 1),
        name="merge",
    )(x, hc, ha, hs, p, p, p, bg, wc, wa, ws, wm)


def _ffn_kernel(x_ref, g_ref, w1_ref, w3_ref, w2_ref, o_ref, hn_ref, acc_ref,
                wb1_ref, wb3_ref, wb2_ref):
    f = pl.program_id(1)

    @pl.when(pl.program_id(0) == 0)
    def _():
        wb1_ref[f] = w1_ref[...].astype(BF16)
        wb3_ref[f] = w3_ref[...].astype(BF16)
        wb2_ref[f] = w2_ref[...].astype(BF16)

    @pl.when(f == 0)
    def _():
        hn_ref[...] = _rms(x_ref[...], g_ref[...]).astype(BF16)
        acc_ref[...] = jnp.zeros_like(acc_ref)

    hn = hn_ref[...]
    a = jnp.dot(hn, wb1_ref[f], preferred_element_type=F32)
    b = jnp.dot(hn, wb3_ref[f], preferred_element_type=F32)
    t = (a * _sigmoid(a) * b).astype(BF16)
    acc_ref[...] += jnp.dot(t, wb2_ref[f], preferred_element_type=F32)

    @pl.when(f == pl.num_programs(1) - 1)
    def _():
        o_ref[...] = x_ref[...] + acc_ref[...]


def _ffn(x, g, w1, w3, w2, layer):
    n = x.shape[0]
    tm, tf = TM_FFN, TF_FFN
    nf = D_FF // tf

    def f_idx(i, f):
        return jnp.where(i == 0, f, nf - 1)

    return pl.pallas_call(
        _ffn_kernel,
        grid=(n // tm, nf),
        in_specs=[
            pl.BlockSpec((tm, D_MODEL), lambda i, f: (i, 0)),
            pl.BlockSpec((1, D_MODEL), lambda i, f: (0, 0)),
            pl.BlockSpec((None, D_MODEL, tf), lambda i, f: (layer, 0, f_idx(i, f))),
            pl.BlockSpec((None, D_MODEL, tf), lambda i, f: (layer, 0, f_idx(i, f))),
            pl.BlockSpec((None, tf, D_MODEL), lambda i, f: (layer, f_idx(i, f), 0)),
        ],
        out_specs=pl.BlockSpec((tm, D_MODEL), lambda i, f: (i, 0)),
        out_shape=jax.ShapeDtypeStruct((n, D_MODEL), F32),
        scratch_shapes=[pltpu.VMEM((tm, D_MODEL), BF16), pltpu.VMEM((tm, D_MODEL), F32),
                        pltpu.VMEM((nf, D_MODEL, tf), BF16), pltpu.VMEM((nf, D_MODEL, tf), BF16),
                        pltpu.VMEM((nf, tf, D_MODEL), BF16)],
        compiler_params=_params(56, 2),
        name="ffn_dense",
    )(x, g, w1, w3, w2)


def _router_kernel(x_ref, g_ref, wr_ref, lt_ref, h_ref, meta_ref, cnt_ref, carry_ref):
    i = pl.program_id(0)
    tm = x_ref.shape[0]

    @pl.when(i == 0)
    def _():
        carry_ref[...] = jnp.zeros_like(carry_ref)

    h = _rms(x_ref[...], g_ref[...])
    h_ref[...] = h
    h_hi = h.astype(BF16)
    h_lo = (h - h_hi.astype(F32)).astype(BF16)
    w = wr_ref[...]
    w_hi = w.astype(BF16)
    w_lo = (w - w_hi.astype(F32)).astype(BF16)
    logits = (jnp.dot(h_hi, w_hi, preferred_element_type=F32)
              + jnp.dot(h_hi, w_lo, preferred_element_type=F32)
              + jnp.dot(h_lo, w_hi, preferred_element_type=F32))
    lane = lax.broadcasted_iota(jnp.int32, (tm, LANES), 1)
    lanef = lane.astype(F32)
    lg = jnp.where(lane < N_EXPERTS, logits, -jnp.inf)
    v1 = jnp.max(lg, axis=1, keepdims=True)
    i1 = jnp.min(jnp.where(lg == v1, lanef, float(LANES)), axis=1, keepdims=True)
    lg2 = jnp.where(lanef == i1, -jnp.inf, lg)
    v2 = jnp.max(lg2, axis=1, keepdims=True)
    i2 = jnp.min(jnp.where(lg2 == v2, lanef, float(LANES)), axis=1, keepdims=True)
    e = jnp.exp(v2 - v1)
    g1 = 1.0 / (1.0 + e)
    g2 = e / (1.0 + e)
    oh1 = lanef == i1
    oh2 = lanef == i2
    cnt = (oh1.astype(F32) + oh2.astype(F32))
    before = jnp.dot(lt_ref[...], cnt.astype(BF16), preferred_element_type=F32) + carry_ref[...]
    r1 = jnp.sum(jnp.where(oh1, before, 0.0), axis=1, keepdims=True)
    r2 = jnp.sum(jnp.where(oh2, before, 0.0), axis=1, keepdims=True)
    total = carry_ref[...] + jnp.sum(cnt, axis=0, keepdims=True)
    carry_ref[...] = total
    cnt_ref[...] = jnp.broadcast_to(total, cnt_ref.shape)
    meta = jnp.zeros((tm, LANES), F32)
    for k, val in enumerate((i1, i2, g1, g2, r1, r2)):
        meta = jnp.where(lane == k, val, meta)
    meta_ref[...] = meta


def _router(x, g, wr, lt):
    n = x.shape[0]
    tm = TM_BRANCH
    return pl.pallas_call(
        _router_kernel,
        grid=(n // tm,),
        in_specs=[
            pl.BlockSpec((tm, D_MODEL), lambda i: (i, 0)),
            pl.BlockSpec((1, D_MODEL), lambda i: (0, 0)),
            pl.BlockSpec((D_MODEL, LANES), lambda i: (0, 0)),
            pl.BlockSpec((tm, tm), lambda i: (0, 0)),
        ],
        out_specs=[
            pl.BlockSpec((tm, D_MODEL), lambda i: (i, 0)),
            pl.BlockSpec((tm, LANES), lambda i: (i, 0)),
            pl.BlockSpec((8, LANES), lambda i: (0, 0)),
        ],
        out_shape=[
            jax.ShapeDtypeStruct((n, D_MODEL), F32),
            jax.ShapeDtypeStruct((n, LANES), F32),
            jax.ShapeDtypeStruct((8, LANES), F32),
        ],
        scratch_shapes=[pltpu.VMEM((1, LANES), F32)],
        compiler_params=_params(32, 1),
        name="router",
    )(x, g, wr, lt)


def _row_copy(src_ref, src_row, dst_ref, dst_row, sem):
    return pltpu.make_async_copy(src_ref.at[pl.ds(src_row, 1), :], dst_ref.at[pl.ds(dst_row, 1), :], sem)


def _dispatch_kernel(bounds_ref, dest_ref, h_ref, xs_ref, zero_ref, sem, zero_sem):
    tm = h_ref.shape[0]
    n_blk = xs_ref.shape[0] // MOE_BLK

    @pl.when(pl.program_id(0) == 0)
    def _():
        zero_ref[...] = jnp.zeros_like(zero_ref)

        def zero_block(blk):
            start = pl.multiple_of(blk * MOE_BLK, MOE_BLK)
            copy = pltpu.make_async_copy(zero_ref, xs_ref.at[pl.ds(start, MOE_BLK), :], zero_sem)
            copy.start()
            copy.wait()

        for e in range(N_EXPERTS):
            @pl.when(bounds_ref[e + 1] > bounds_ref[e])
            def _():
                zero_block(bounds_ref[e + 1] // MOE_BLK - 1)

        def unused(blk, carry):
            zero_block(blk)
            return carry

        lax.fori_loop(bounds_ref[N_EXPERTS] // MOE_BLK, n_blk, unused, 0)

    def issue(r, carry):
        for k in range(2):
            _row_copy(h_ref, r, xs_ref, dest_ref[2 * r + k], sem).start(priority=k)
        return carry

    lax.fori_loop(0, tm, issue, 0, unroll=8)
    for _ in range(2):
        pltpu.make_async_copy(h_ref, xs_ref.at[pl.ds(0, tm), :], sem).wait()


def _dispatch(bounds, dest_flat, h, n_rows):
    n = h.shape[0]
    tm = TM_ROW
    return pl.pallas_call(
        _dispatch_kernel,
        grid=(n // tm,),
        in_specs=[
            pl.BlockSpec(memory_space=pltpu.SMEM),
            pl.BlockSpec((2 * tm,), lambda i: (i,), memory_space=pltpu.SMEM),
            pl.BlockSpec((tm, D_MODEL), lambda i: (i, 0)),
        ],
        out_specs=pl.BlockSpec(memory_space=pl.ANY),
        out_shape=jax.ShapeDtypeStruct((n_rows, D_MODEL), F32),
        scratch_shapes=[pltpu.VMEM((MOE_BLK, D_MODEL), F32), pltpu.SemaphoreType.DMA(()),
                        pltpu.SemaphoreType.DMA(())],
        compiler_params=_params(32, 1),
        name="moe_dispatch",
    )(bounds, dest_flat, h)


def _first_block_of_expert(b, be_ref):
    return jnp.logical_or(b == 0, be_ref[b] != be_ref[jnp.maximum(b - 1, 0)])


def _moe_kernel(be_ref, nu_ref, x_ref, w1_ref, w3_ref, w2_ref, y_ref, xb_ref, acc_ref,
                wb1_ref, wb3_ref, wb2_ref):
    b = pl.program_id(0)
    f = pl.program_id(1)

    @pl.when(b < nu_ref[0])
    def _():
        @pl.when(_first_block_of_expert(b, be_ref))
        def _():
            wb1_ref[f] = w1_ref[...].astype(BF16)
            wb3_ref[f] = w3_ref[...].astype(BF16)
            wb2_ref[f] = w2_ref[...].astype(BF16)

        @pl.when(f == 0)
        def _():
            xb_ref[...] = x_ref[...].astype(BF16)
            acc_ref[...] = jnp.zeros_like(acc_ref)

        xb = xb_ref[...]
        a = jnp.dot(xb, wb1_ref[f], preferred_element_type=F32)
        c = jnp.dot(xb, wb3_ref[f], preferred_element_type=F32)
        t = (a * _sigmoid(a) * c).astype(BF16)
        acc_ref[...] += jnp.dot(t, wb2_ref[f], preferred_element_type=F32)

        @pl.when(f == pl.num_programs(1) - 1)
        def _():
            y_ref[...] = acc_ref[...]

    @pl.when(jnp.logical_and(b >= nu_ref[0], f == pl.num_programs(1) - 1))
    def _():
        y_ref[...] = jnp.zeros_like(y_ref)


def _moe_experts(blk_e, n_used, xs, w1, w3, w2, layer):
    rows = xs.shape[0]
    n_blk = rows // MOE_BLK
    nf = D_FF_EXPERT // TF_MOE

    def row_idx(b, f, be, nu):
        return (jnp.minimum(b, nu[0] - 1), 0)

    def f_idx(b, f, be, nu):
        fetch = jnp.logical_and(b < nu[0], _first_block_of_expert(b, be))
        return jnp.where(fetch, f, nf - 1)

    grid_spec = pltpu.PrefetchScalarGridSpec(
        num_scalar_prefetch=2,
        grid=(n_blk, nf),
        in_specs=[
            pl.BlockSpec((MOE_BLK, D_MODEL), row_idx),
            pl.BlockSpec((None, None, D_MODEL, TF_MOE),
                         lambda b, f, be, nu: (layer, be[b], 0, f_idx(b, f, be, nu))),
            pl.BlockSpec((None, None, D_MODEL, TF_MOE),
                         lambda b, f, be, nu: (layer, be[b], 0, f_idx(b, f, be, nu))),
            pl.BlockSpec((None, None, TF_MOE, D_MODEL),
                         lambda b, f, be, nu: (layer, be[b], f_idx(b, f, be, nu), 0)),
        ],
        out_specs=pl.BlockSpec((MOE_BLK, D_MODEL), lambda b, f, be, nu: (b, 0)),
        scratch_shapes=[pltpu.VMEM((MOE_BLK, D_MODEL), BF16), pltpu.VMEM((MOE_BLK, D_MODEL), F32),
                        pltpu.VMEM((nf, D_MODEL, TF_MOE), BF16), pltpu.VMEM((nf, D_MODEL, TF_MOE), BF16),
                        pltpu.VMEM((nf, TF_MOE, D_MODEL), BF16)],
    )
    return pl.pallas_call(
        _moe_kernel,
        grid_spec=grid_spec,
        out_shape=jax.ShapeDtypeStruct((rows, D_MODEL), F32),
        compiler_params=_params(56, 2),
        name="moe_experts",
    )(blk_e, n_used, xs, w1, w3, w2)


def _combine_kernel(dest_ref, x_ref, meta_ref, gfin_ref, y_ref, o_ref, ybuf_ref, sem, *, final_norm):
    tm = x_ref.shape[0]

    def issue(r, carry):
        for k in range(2):
            _row_copy(y_ref, dest_ref[2 * r + k], ybuf_ref.at[k], r, sem).start(priority=k)
        return carry

    lax.fori_loop(0, tm, issue, 0, unroll=8)
    for k in range(2):
        pltpu.make_async_copy(y_ref.at[pl.ds(0, tm), :], ybuf_ref.at[k], sem).wait()

    g1 = meta_ref[:, 2:3]
    g2 = meta_ref[:, 3:4]
    out = x_ref[...] + (g1 * ybuf_ref[0] + g2 * ybuf_ref[1])
    if final_norm:
        out = _rms(out, gfin_ref[...])
    o_ref[...] = out


def _combine(dest_flat, x, meta, gfin, y, final_norm):
    n = x.shape[0]
    tm = TM_ROW
    kern = functools.partial(_combine_kernel, final_norm=final_norm)
    return pl.pallas_call(
        kern,
        grid=(n // tm,),
        in_specs=[
            pl.BlockSpec((2 * tm,), lambda i: (i,), memory_space=pltpu.SMEM),
            pl.BlockSpec((tm, D_MODEL), lambda i: (i, 0)),
            pl.BlockSpec((tm, LANES), lambda i: (i, 0)),
            pl.BlockSpec((1, D_MODEL), lambda i: (0, 0)),
            pl.BlockSpec(memory_space=pl.ANY),
        ],
        out_specs=pl.BlockSpec((tm, D_MODEL), lambda i: (i, 0)),
        out_shape=jax.ShapeDtypeStruct((n, D_MODEL), F32),
        scratch_shapes=[pltpu.VMEM((2, tm, D_MODEL), F32), pltpu.SemaphoreType.DMA(())],
        compiler_params=_params(32, 1),
        name="moe_combine",
    )(dest_flat, x, meta, gfin, y)


def _moe_layer(x, g, wr, lt, w1, w3, w2, layer, gfin, final_norm):
    n = x.shape[0]
    h, meta, cnt = _router(x, g, wr, lt)
    expert = meta[:, 0:2].astype(jnp.int32)
    rank = meta[:, 4:6].astype(jnp.int32)
    counts = cnt[0, :N_EXPERTS].astype(jnp.int32)
    padded = (counts + MOE_BLK - 1) // MOE_BLK * MOE_BLK
    pad_end = jnp.cumsum(padded)
    pad_start = pad_end - padded
    dest = (pad_start[expert] + rank).reshape(-1)
    n_blk = (2 * n) // MOE_BLK + N_EXPERTS
    blk_start = jnp.arange(n_blk, dtype=jnp.int32) * MOE_BLK
    blk_e = jnp.minimum(jnp.sum(blk_start[:, None] >= pad_end[None, :], axis=1), N_EXPERTS - 1)
    n_used = (pad_end[-1:] // MOE_BLK).astype(jnp.int32)
    bounds = jnp.concatenate([jnp.zeros((1,), jnp.int32), pad_end.astype(jnp.int32)])
    xs = _dispatch(bounds, dest, h, n_blk * MOE_BLK)
    y = _moe_experts(blk_e.astype(jnp.int32), n_used, xs, w1, w3, w2, layer)
    return _combine(dest, x, meta, gfin, y, final_norm)


def _prepare_mixer(w_in, b_forget, conv_w, w_spatial, b_spatial, w_conv_out, w_att_out, w_sg_out,
                   w_mix_out):
    depth, d, _ = w_in.shape

    def pad_heads(w):
        w = w.reshape(depth, d, ATT_HEADS, HEAD_DIM)
        w = jnp.pad(w, ((0, 0), (0, 0), (0, 0), (0, HEAD_PAD - HEAD_DIM)))
        return w.reshape(depth, d, ATT_HEADS * HEAD_PAD)

    prep = {}
    prep["w_main"] = jnp.concatenate(
        [pad_heads(w_in[:, :, OFF_K:OFF_V]), w_in[:, :, OFF_CONV:OFF_Q], w_in[:, :, OFF_SG:OFF_GATE],
         w_in[:, :, OFF_GATE:], pad_heads(w_in[:, :, OFF_Q:OFF_K])], axis=2).astype(BF16)
    col = jnp.arange(P_COLS)
    is_q = col >= P_Q
    spare = (col % HEAD_PAD >= HEAD_DIM) & (col % HEAD_PAD < HEAD_DIM + N_SPLIT)
    prep["col_scale"] = jnp.where(is_q, LOG2E * HEAD_DIM ** -0.5, 1.0).astype(F32)[None, :]
    prep["col_bias"] = jnp.where(is_q & spare, 1.0, 0.0).astype(F32)[None, :]
    kcol = jnp.arange(ATT_HEADS * HEAD_PAD)
    place = -((kcol[None, None, :] // HEAD_PAD == jnp.arange(LANES)[None, :, None])
              & (kcol[None, None, :] % HEAD_PAD == HEAD_DIM + jnp.arange(N_SPLIT)[:, None, None])
              ).astype(BF16)
    prep["place"] = place.reshape(N_SPLIT * LANES, ATT_HEADS * HEAD_PAD)
    prep["wf"] = jnp.pad(w_in[:, :, OFF_F:OFF_SG], ((0, 0), (0, 0), (0, LANES - ATT_HEADS))).astype(BF16)
    prep["bf"] = jnp.pad(b_forget, ((0, 0), (0, LANES - ATT_HEADS)))[:, None, :]
    wv_t = jnp.swapaxes(w_in[:, :, OFF_V:OFF_F], 1, 2).reshape(depth, ATT_HEADS // 2, 2, HEAD_DIM, d)
    zeros_h = jnp.zeros_like(wv_t[:, :, 0, :V_ROWS - HEAD_DIM])
    wvt = jnp.concatenate([wv_t[:, :, 0], zeros_h, zeros_h, wv_t[:, :, 1]], axis=2)
    prep["wvt"] = wvt.reshape(depth, ATT_HEADS * V_ROWS, d).astype(BF16)
    vrow = jnp.arange(ATT_HEADS * V_ROWS) % (2 * V_ROWS)
    prep["vbias"] = ((vrow == HEAD_DIM) | (vrow == V_ROWS)).astype(F32)[:, None]
    prep["lt_incl"] = jnp.tril(jnp.ones((LANES, LANES), F32)).astype(BF16)
    prep["conv_w"] = jnp.pad(conv_w, ((0, 0), (0, 1), (0, 0)))
    causal = jnp.tril(jnp.ones((SG_CHUNK, SG_CHUNK), bool))
    prep["w_sp"] = jnp.where(causal[None, None], w_spatial, 0).astype(BF16)
    prep["sg_bias"] = jnp.repeat(jnp.swapaxes(b_spatial, 1, 2), SG_WIDTH // SG_GROUPS, axis=2)
    for name, w in (("wc", w_conv_out), ("wa", w_att_out), ("ws", w_sg_out), ("wm", w_mix_out)):
        prep[name] = w.astype(BF16)
    return prep


def _token_mixer(xf, prep, vecs, layer, batch, seq):
    row = lambda name: vecs[name][layer][None, :]
    p, vt = _inproj(xf, row("mix_norm_g"), prep["w_main"], prep["col_scale"], prep["col_bias"],
                    prep["wf"], prep["bf"][layer], prep["lt_incl"], prep["place"],
                    prep["wvt"], prep["vbias"], seq, layer)
    hc = _conv_branch(p, prep["conv_w"][layer], row("conv_b"), row("conv_ln_g"), row("conv_ln_b"), seq)
    hs = _sg_branch(p, row("sg_ln_g"), row("sg_ln_b"), prep["w_sp"][layer], prep["sg_bias"][layer])
    ha = _attention(p, vt, batch, seq)
    return _merge(xf, hc, ha, hs, p, row("b_gate"), prep["wc"], prep["wa"], prep["ws"], prep["wm"], layer)


def kernel(x, mix_norm_g, w_in, b_forget, b_gate, conv_w, conv_b, conv_ln_g, conv_ln_b, w_conv_out,
           w_att_out, sg_ln_g, sg_ln_b, w_spatial, b_spatial, w_sg_out, w_mix_out, ffn_norm_g,
           ffn_w1, ffn_w3, ffn_w2, router_w, moe_w1, moe_w3, moe_w2, final_norm_g):
    batch, seq, d = x.shape
    depth = w_in.shape[0]
    if depth % 2 == 1:
        raise NotImplementedError("the final norm is fused into the last (expert) layer")
    xf = x.reshape(batch * seq, d)

    prep = _prepare_mixer(w_in, b_forget, conv_w, w_spatial, b_spatial, w_conv_out, w_att_out,
                          w_sg_out, w_mix_out)
    vecs = dict(mix_norm_g=mix_norm_g, conv_b=conv_b, conv_ln_g=conv_ln_g, conv_ln_b=conv_ln_b,
                sg_ln_g=sg_ln_g, sg_ln_b=sg_ln_b, b_gate=b_gate)
    f1, f3, f2 = ffn_w1, ffn_w3, ffn_w2
    m1, m3, m2 = moe_w1, moe_w3, moe_w2
    wr = jnp.pad(router_w, ((0, 0), (0, 0), (0, LANES - N_EXPERTS)))
    lt = jnp.tril(jnp.ones((TM_BRANCH, TM_BRANCH), F32), -1).astype(BF16)
    gfin = final_norm_g[None, :]

    for layer in range(depth):
        xf = _token_mixer(xf, prep, vecs, layer, batch, seq)
        g_ffn = ffn_norm_g[layer][None, :]
        i = layer // 2
        if layer % 2 == 0:
            xf = _ffn(xf, g_ffn, f1, f3, f2, i)
        else:
            xf = _moe_layer(xf, g_ffn, wr[i], lt, m1, m3, m2, i, gfin,
                            final_norm=(layer == depth - 1))
    return xf.reshape(batch, seq, d)
```

```python
import functools

import jax
import jax.numpy as jnp
from jax import lax
from jax.experimental import pallas as pl
from jax.experimental.pallas import tpu as pltpu

F32 = jnp.float32
BF16 = jnp.bfloat16

D_MODEL = 1024
CONV_CH = 512
CONV_K = 31
ATT_HEADS = 8
HEAD_DIM = 64
ATT_WIDTH = ATT_HEADS * HEAD_DIM
SG_GROUPS = 8
SG_WIDTH = 512
SG_CHUNK = 128
N_BRANCH = 3
OFF_CONV = 0
OFF_Q = OFF_CONV + 2 * CONV_CH
OFF_K = OFF_Q + ATT_WIDTH
OFF_V = OFF_K + ATT_WIDTH
OFF_F = OFF_V + ATT_WIDTH
OFF_SG = OFF_F + ATT_HEADS
OFF_GATE = OFF_SG + 2 * SG_WIDTH
D_FF = 2816
N_EXPERTS = 8
D_FF_EXPERT = 3584
EPS = 1e-6

LANES = 128
SUBLANES = 8
MIB = 1024 * 1024

HEAD_PAD = LANES
P_K = 0
P_CONV = P_K + ATT_HEADS * HEAD_PAD
P_SG = P_CONV + 2 * CONV_CH
P_GATE = P_SG + 2 * SG_WIDTH
P_Q = P_GATE + N_BRANCH * D_MODEL
P_COLS = P_Q + ATT_HEADS * HEAD_PAD
N_SPLIT = 3
V_ROWS = HEAD_PAD
LOG2E = 1.4426950408889634

TM_PROJ = 1024
TN_PROJ = 1024
TM_BRANCH = 512
CONV_HIST = 32
CONV_RB = 64
TQ = 512
TK = 512
ATT_UNROLL = 28
TM_FFN = 1024
TF_FFN = 256
MOE_BLK = 512
TF_MOE = 512
TM_ROW = 1024


def _params(vmem_mib, n_axes):
    return pltpu.CompilerParams(dimension_semantics=("arbitrary",) * n_axes,
                                vmem_limit_bytes=vmem_mib * MIB)


def _rms(x, g):
    return x * lax.rsqrt(jnp.mean(x * x, axis=-1, keepdims=True) + EPS) * g


def _layer_norm(x, g, b):
    mu = jnp.mean(x, axis=-1, keepdims=True)
    d = x - mu
    var = jnp.mean(d * d, axis=-1, keepdims=True)
    return d * lax.rsqrt(var + EPS) * g + b


def _sigmoid(x):
    return 1.0 / (1.0 + jnp.exp(-x))


def _split3(x):
    hi = x.astype(BF16)
    r1 = x - hi.astype(F32)
    mid = r1.astype(BF16)
    lo = (r1 - mid.astype(F32)).astype(BF16)
    return hi, mid, lo


def _inproj_kernel(x_ref, g_ref, w_ref, sc_ref, bi_ref, wf_ref, bf_ref, lt_ref, pl_ref, wvt_ref,
                   vb_ref, p_ref, vt_ref, xn_ref, c3_ref, carry_ref, wres_ref, *, tiles_per_batch):
    i = pl.program_id(0)
    j = pl.program_id(1)
    tm = x_ref.shape[0]
    tk = vt_ref.shape[-1]

    @pl.when(i == 0)
    def _():
        wres_ref[j] = w_ref[...]

    @pl.when(jnp.logical_and(j == 0, i % tiles_per_batch == 0))
    def _():
        carry_ref[...] = jnp.zeros_like(carry_ref)

    @pl.when(j == 0)
    def _():
        xn = _rms(x_ref[...], g_ref[...]).astype(BF16)
        xn_ref[...] = xn
        f = jnp.dot(xn, wf_ref[...], preferred_element_type=F32) + bf_ref[...]
        ls = jnp.minimum(f, 0.0) - jnp.log1p(jnp.exp(-jnp.abs(f)))
        lt = lt_ref[...]
        terms = _split3(ls)
        offset = carry_ref[...]
        for blk in range(tm // LANES):
            rows = slice(blk * LANES, (blk + 1) * LANES)
            c = sum(jnp.dot(lt, term[rows], preferred_element_type=F32) for term in terms) + offset
            offset = c[LANES - 1:LANES, :]
            for r, term in enumerate(_split3(c * LOG2E)):
                c3_ref[rows, r * LANES:(r + 1) * LANES] = term
        carry_ref[...] = offset
        vt = lax.dot_general(wvt_ref[...], xn, (((1,), (1,)), ((), ())),
                             preferred_element_type=F32) + vb_ref[...]
        for ch in range(tm // tk):
            vt_ref[ch] = vt[:, ch * tk:(ch + 1) * tk].astype(BF16)
        decay = jnp.dot(c3_ref[...], pl_ref[...], preferred_element_type=F32)
        p_ref[...] = (jnp.dot(xn, wres_ref[0], preferred_element_type=F32) + decay).astype(BF16)

    @pl.when(j > 0)
    def _():
        acc = jnp.dot(xn_ref[...], wres_ref[j], preferred_element_type=F32)
        p_ref[...] = (acc * sc_ref[...] + bi_ref[...]).astype(BF16)


def _inproj(x, g, w, col_scale, col_bias, wf, bf, lt, place, wvt, vbias, seq, layer):
    n = x.shape[0]
    tm, tn = TM_PROJ, TN_PROJ
    assert P_K == 0 and tn == ATT_HEADS * HEAD_PAD
    kern = functools.partial(_inproj_kernel, tiles_per_batch=seq // tm)
    vt_rows = ATT_HEADS * V_ROWS
    n_col = P_COLS // tn
    const = lambda shape: pl.BlockSpec(shape, lambda i, j: (0,) * len(shape))
    return pl.pallas_call(
        kern,
        grid=(n // tm, n_col),
        in_specs=[
            pl.BlockSpec((tm, D_MODEL), lambda i, j: (i, 0)),
            const((1, D_MODEL)),
            pl.BlockSpec((None, D_MODEL, tn), lambda i, j: (layer, 0, jnp.where(i == 0, j, n_col - 1))),
            pl.BlockSpec((1, tn), lambda i, j: (0, j)),
            pl.BlockSpec((1, tn), lambda i, j: (0, j)),
            pl.BlockSpec((None, D_MODEL, LANES), lambda i, j: (layer, 0, 0)),
            const((1, LANES)),
            const((LANES, LANES)),
            const((N_SPLIT * LANES, tn)),
            pl.BlockSpec((None, vt_rows, D_MODEL), lambda i, j: (layer, 0, 0)),
            const((vt_rows, 1)),
        ],
        out_specs=[
            pl.BlockSpec((tm, tn), lambda i, j: (i, j)),
            pl.BlockSpec((tm // TK, vt_rows, TK), lambda i, j: (i, 0, 0)),
        ],
        out_shape=[
            jax.ShapeDtypeStruct((n, P_COLS), BF16),
            jax.ShapeDtypeStruct((n // TK, vt_rows, TK), BF16),
        ],
        scratch_shapes=[pltpu.VMEM((tm, D_MODEL), BF16), pltpu.VMEM((tm, N_SPLIT * LANES), BF16),
                        pltpu.VMEM((1, LANES), F32), pltpu.VMEM((n_col, D_MODEL, tn), BF16)],
        compiler_params=_params(56, 2),
        name="inproj",
    )(x, g, w, col_scale, col_bias, wf, bf, lt, place, wvt, vbias)


def _conv_kernel(a1_ref, a2_ref, w_ref, cb_ref, g_ref, b_ref, o_ref, hext_ref, hsh_ref,
                 *, tiles_per_batch):
    i = pl.program_id(0)
    tm = a1_ref.shape[0]
    span = hsh_ref.shape[1]

    @pl.when(i % tiles_per_batch == 0)
    def _():
        hext_ref[0:CONV_HIST, :] = jnp.zeros((CONV_HIST, CONV_CH), F32)

    @pl.when(i % tiles_per_batch != 0)
    def _():
        hext_ref[0:CONV_HIST, :] = hext_ref[tm:tm + CONV_HIST, :]

    hext_ref[CONV_HIST:CONV_HIST + tm, :] = (
        a1_ref[...].astype(F32) * _sigmoid(a2_ref[...].astype(F32)))
    for s in range(1, SUBLANES):
        hsh_ref[s] = hext_ref[s:s + span, :]

    g = g_ref[...]
    b = b_ref[...]
    base = CONV_HIST - (CONV_K - 1)
    for r in range(0, tm, CONV_RB):
        acc = jnp.broadcast_to(cb_ref[...], (CONV_RB, CONV_CH))
        for j in range(CONV_K):
            s = (base + j) % SUBLANES
            a = r + base + j - s
            rows = hext_ref[a:a + CONV_RB, :] if s == 0 else hsh_ref[s, a:a + CONV_RB, :]
            acc = acc + w_ref[j:j + 1, :] * rows
        y = _layer_norm(acc, g, b)
        o_ref[r:r + CONV_RB, :] = (y * _sigmoid(y)).astype(BF16)


def _conv_branch(p, w, cb, g, b, seq):
    n = p.shape[0]
    tm = TM_BRANCH
    kern = functools.partial(_conv_kernel, tiles_per_batch=seq // tm)
    c0 = P_CONV // CONV_CH
    vec = pl.BlockSpec((1, CONV_CH), lambda i: (0, 0))
    return pl.pallas_call(
        kern,
        grid=(n // tm,),
        in_specs=[
            pl.BlockSpec((tm, CONV_CH), lambda i: (i, c0)),
            pl.BlockSpec((tm, CONV_CH), lambda i: (i, c0 + 1)),
            pl.BlockSpec((CONV_K + 1, CONV_CH), lambda i: (0, 0)),
            vec, vec, vec,
        ],
        out_specs=pl.BlockSpec((tm, CONV_CH), lambda i: (i, 0)),
        out_shape=jax.ShapeDtypeStruct((n, CONV_CH), BF16),
        scratch_shapes=[pltpu.VMEM((tm + CONV_HIST, CONV_CH), F32),
                        pltpu.VMEM((SUBLANES, tm + CONV_HIST - SUBLANES, CONV_CH), F32)],
        compiler_params=_params(32, 1),
        name="conv_branch",
    )(p, p, w, cb, g, b)


def _sg_kernel(u_ref, v_ref, g_ref, b_ref, w_ref, bias_ref, o_ref):
    tm = u_ref.shape[0]

    def gelu(z):
        return 0.5 * z * (1.0 + lax.erf(z * 0.7071067811865476))

    zu = gelu(u_ref[...].astype(F32))
    vn = _layer_norm(gelu(v_ref[...].astype(F32)), g_ref[...], b_ref[...]).astype(BF16)
    lane = lax.broadcasted_iota(jnp.int32, (SG_CHUNK, LANES), 1)
    first_group = lane < (SG_WIDTH // SG_GROUPS)
    for c in range(tm // SG_CHUNK):
        rows = slice(c * SG_CHUNK, (c + 1) * SG_CHUNK)
        for pr in range(SG_WIDTH // LANES):
            cols = slice(pr * LANES, (pr + 1) * LANES)
            vp = vn[rows, cols]
            m0 = jnp.dot(w_ref[2 * pr], vp, preferred_element_type=F32)
            m1 = jnp.dot(w_ref[2 * pr + 1], vp, preferred_element_type=F32)
            mixed = jnp.where(first_group, m0, m1) + bias_ref[:, cols]
            o_ref[rows, cols] = (zu[rows, cols] * mixed).astype(BF16)


def _sg_branch(p, g, b, w_tril, bias_full):
    n = p.shape[0]
    tm = TM_BRANCH
    c0 = P_SG // SG_WIDTH
    vec = pl.BlockSpec((1, SG_WIDTH), lambda i: (0, 0))
    return pl.pallas_call(
        _sg_kernel,
        grid=(n // tm,),
        in_specs=[
            pl.BlockSpec((tm, SG_WIDTH), lambda i: (i, c0)),
            pl.BlockSpec((tm, SG_WIDTH), lambda i: (i, c0 + 1)),
            vec, vec,
            pl.BlockSpec((SG_GROUPS, SG_CHUNK, SG_CHUNK), lambda i: (0, 0, 0)),
            pl.BlockSpec((SG_CHUNK, SG_WIDTH), lambda i: (0, 0)),
        ],
        out_specs=pl.BlockSpec((tm, SG_WIDTH), lambda i: (i, 0)),
        out_shape=jax.ShapeDtypeStruct((n, SG_WIDTH), BF16),
        compiler_params=_params(32, 1),
        name="sg_branch",
    )(p, p, g, b, w_tril, bias_full)


def _attn_kernel(ti_ref, tj_ref, q_ref, k_ref, vt_ref, o_ref, sa_ref, sb_ref, m_ref, acc_ref, tri_ref,
                 *, n_below, n_diag):
    tq = sa_ref.shape[-1]
    tk = vt_ref.shape[-1]
    head_cols = [slice(hh * HEAD_PAD, (hh + 1) * HEAD_PAD) for hh in range(2)]

    def scores(t, dst_ref):
        q0 = pl.multiple_of(ti_ref[t] * tq, tq)
        k0 = pl.multiple_of(tj_ref[t] * tk, tk)
        for hh, cols in enumerate(head_cols):
            dst_ref[hh] = lax.dot_general(k_ref[pl.ds(k0, tk), cols], q_ref[pl.ds(q0, tq), cols],
                                          (((1,), (1,)), ((), ())), preferred_element_type=F32)

    def consume(t, src_ref, masked):
        i = ti_ref[t]
        j = tj_ref[t]
        probs = []
        for hh in range(2):
            s = src_ref[hh]
            if masked:
                s = s + tri_ref[...]
            m = m_ref[i, hh]
            m_new = jnp.maximum(m, jnp.max(s, axis=0, keepdims=True))
            m_ref[i, hh] = m_new
            probs.append((jnp.exp2(m - m_new), jnp.exp2(s - m_new).astype(BF16)))
        for hh, (alpha, pexp) in enumerate(probs):
            pv = jnp.dot(vt_ref[j, hh * V_ROWS:(hh + 1) * V_ROWS, :], pexp, preferred_element_type=F32)
            acc_ref[i, hh] = alpha * acc_ref[i, hh] + pv

    def run(base, count, masked):
        bufs = (sa_ref, sb_ref)
        scores(base, bufs[0])

        def trip(u, carry):
            t = base + ATT_UNROLL * u
            for k in range(ATT_UNROLL):
                scores(t + k + 1, bufs[(k + 1) % 2])
                consume(t + k, bufs[k % 2], masked)
            return carry

        lax.fori_loop(0, count // ATT_UNROLL, trip, 0)
        done = count - count % ATT_UNROLL
        for k in range(done, count):
            if k + 1 < count:
                scores(base + k + 1, bufs[(k + 1) % 2])
            consume(base + k, bufs[k % 2], masked)

    tri_ref[...] = jnp.where(lax.broadcasted_iota(jnp.int32, (tk, tq), 0)
                             <= lax.broadcasted_iota(jnp.int32, (tk, tq), 1), 0.0, -jnp.inf)
    m_ref[...] = jnp.full(m_ref.shape, -jnp.inf, F32)
    acc_ref[...] = jnp.zeros(acc_ref.shape, F32)
    if n_below:
        run(0, n_below, False)
    run(n_below + 1, n_diag, True)

    eye = (lax.broadcasted_iota(jnp.int32, (tq, tq), 0)
           == lax.broadcasted_iota(jnp.int32, (tq, tq), 1)).astype(BF16)
    for i in range(q_ref.shape[0] // tq):
        even = acc_ref[i, 0]
        odd = acc_ref[i, 1]
        merged = jnp.concatenate([even[:HEAD_DIM] / even[HEAD_DIM:HEAD_DIM + 1, :],
                                  odd[V_ROWS - HEAD_DIM:] / odd[0:1, :]], axis=0)
        o_ref[i * tq:(i + 1) * tq, :] = lax.dot_general(
            eye, merged.astype(BF16), (((1,), (1,)), ((), ())),
            preferred_element_type=F32).astype(BF16)


def _attention(p, vt, batch, seq):
    n = p.shape[0]
    pairs = ATT_HEADS // 2
    assert TQ == TK
    nq = seq // TQ
    below = [(i, j) for i in range(nq) for j in range(i)]
    diag = [(i, i) for i in range(nq)]
    tiles = below + below[-1:] + diag + diag[-1:] if below else [(0, 0)] + diag + diag[-1:]
    ti = jnp.asarray([t[0] for t in tiles], jnp.int32)
    tj = jnp.asarray([t[1] for t in tiles], jnp.int32)
    pw = 2 * HEAD_PAD
    grid_spec = pltpu.PrefetchScalarGridSpec(
        num_scalar_prefetch=2,
        grid=(batch, pairs),
        in_specs=[
            pl.BlockSpec((seq, pw), lambda b, h, ti, tj: (b, P_Q // pw + h)),
            pl.BlockSpec((seq, pw), lambda b, h, ti, tj: (b, P_K // pw + h)),
            pl.BlockSpec((seq // TK, 2 * V_ROWS, TK), lambda b, h, ti, tj: (b, h, 0)),
        ],
        out_specs=pl.BlockSpec((seq, LANES), lambda b, h, ti, tj: (b, h)),
        scratch_shapes=[pltpu.VMEM((2, TK, TQ), F32), pltpu.VMEM((2, TK, TQ), F32),
                        pltpu.VMEM((nq, 2, 1, TQ), F32), pltpu.VMEM((nq, 2, V_ROWS, TQ), F32),
                        pltpu.VMEM((TK, TQ), F32)],
    )
    return pl.pallas_call(
        functools.partial(_attn_kernel, n_below=len(below), n_diag=len(diag)),
        grid_spec=grid_spec,
        out_shape=jax.ShapeDtypeStruct((n, ATT_WIDTH), BF16),
        compiler_params=_params(40, 2),
        name="fox_attention",
    )(ti, tj, p, p, vt)


def _merge_kernel(x_ref, hc_ref, ha_ref, hs_ref, g0_ref, g1_ref, g2_ref, bg_ref,
                  wc_ref, wa_ref, ws_ref, wm_ref, o_ref):
    merged = None
    for k, (h_ref, w_ref, gl_ref) in enumerate(((hc_ref, wc_ref, g0_ref), (ha_ref, wa_ref, g1_ref),
                                                (hs_ref, ws_ref, g2_ref))):
        y = jnp.dot(h_ref[...], w_ref[...], preferred_element_type=F32)
        gate = _sigmoid(gl_ref[...].astype(F32) + bg_ref[:, k * D_MODEL:(k + 1) * D_MODEL])
        merged = gate * y if merged is None else merged + gate * y
    o_ref[...] = x_ref[...] + jnp.dot(merged.astype(BF16), wm_ref[...], preferred_element_type=F32)


def _merge(x, hc, ha, hs, p, bg, wc, wa, ws, wm, layer):
    n = x.shape[0]
    tm = TM_BRANCH
    g0 = P_GATE // D_MODEL
    half = lambda: pl.BlockSpec((tm, CONV_CH), lambda i: (i, 0))
    wspec = lambda k: pl.BlockSpec((None, k, D_MODEL), lambda i: (layer, 0, 0))
    return pl.pallas_call(
        _merge_kernel,
        grid=(n // tm,),
        in_specs=[
            pl.BlockSpec((tm, D_MODEL), lambda i: (i, 0)),
            half(), half(), half(),
            pl.BlockSpec((tm, D_MODEL), lambda i: (i, g0)),
            pl.BlockSpec((tm, D_MODEL), lambda i: (i, g0 + 1)),
            pl.BlockSpec((tm, D_MODEL), lambda i: (i, g0 + 2)),
            pl.BlockSpec((1, N_BRANCH * D_MODEL), lambda i: (0, 0)),
            wspec(CONV_CH), wspec(ATT_WIDTH), wspec(SG_WIDTH), wspec(D_MODEL),
        ],
        out_specs=pl.BlockSpec((tm, D_MODEL), lambda i: (i, 0)),
        out_shape=jax.ShapeDtypeStruct((n, D_MODEL), F32),
        compiler_params=_params(48, 1),
        name="merge",
    )(x, hc, ha, hs, p, p, p, bg, wc, wa, ws, wm)


def _ffn_kernel(x_ref, g_ref, w1_ref, w3_ref, w2_ref, o_ref, hn_ref, wb1_ref, wb3_ref, wb2_ref):
    f = pl.program_id(1)

    @pl.when(pl.program_id(0) == 0)
    def _():
        wb1_ref[f] = w1_ref[...].astype(BF16)
        wb3_ref[f] = w3_ref[...].astype(BF16)
        wb2_ref[f] = w2_ref[...].astype(BF16)

    @pl.when(f == 0)
    def _():
        hn_ref[...] = _rms(x_ref[...], g_ref[...]).astype(BF16)
        o_ref[...] = x_ref[...]

    hn = hn_ref[...]
    a = jnp.dot(hn, wb1_ref[f], preferred_element_type=F32)
    b = jnp.dot(hn, wb3_ref[f], preferred_element_type=F32)
    t = (a * _sigmoid(a) * b).astype(BF16)
    o_ref[...] += jnp.dot(t, wb2_ref[f], preferred_element_type=F32)


def _ffn(x, g, w1, w3, w2, layer):
    n = x.shape[0]
    tm, tf = TM_FFN, TF_FFN
    nf = D_FF // tf

    def f_idx(i, f):
        return jnp.where(i == 0, f, nf - 1)

    return pl.pallas_call(
        _ffn_kernel,
        grid=(n // tm, nf),
        in_specs=[
            pl.BlockSpec((tm, D_MODEL), lambda i, f: (i, 0)),
            pl.BlockSpec((1, D_MODEL), lambda i, f: (0, 0)),
            pl.BlockSpec((None, D_MODEL, tf), lambda i, f: (layer, 0, f_idx(i, f))),
            pl.BlockSpec((None, D_MODEL, tf), lambda i, f: (layer, 0, f_idx(i, f))),
            pl.BlockSpec((None, tf, D_MODEL), lambda i, f: (layer, f_idx(i, f), 0)),
        ],
        out_specs=pl.BlockSpec((tm, D_MODEL), lambda i, f: (i, 0)),
        out_shape=jax.ShapeDtypeStruct((n, D_MODEL), F32),
        scratch_shapes=[pltpu.VMEM((tm, D_MODEL), BF16),
                        pltpu.VMEM((nf, D_MODEL, tf), BF16), pltpu.VMEM((nf, D_MODEL, tf), BF16),
                        pltpu.VMEM((nf, tf, D_MODEL), BF16)],
        compiler_params=_params(56, 2),
        name="ffn_dense",
    )(x, g, w1, w3, w2)


def _router_kernel(x_ref, g_ref, wr_ref, lt_ref, h_ref, meta_ref, cnt_ref, carry_ref):
    i = pl.program_id(0)
    tm = x_ref.shape[0]

    @pl.when(i == 0)
    def _():
        carry_ref[...] = jnp.zeros_like(carry_ref)

    h = _rms(x_ref[...], g_ref[...])
    h_ref[...] = h
    h_hi = h.astype(BF16)
    h_lo = (h - h_hi.astype(F32)).astype(BF16)
    w = wr_ref[...]
    w_hi = w.astype(BF16)
    w_lo = (w - w_hi.astype(F32)).astype(BF16)
    logits = (jnp.dot(h_hi, w_hi, preferred_element_type=F32)
              + jnp.dot(h_hi, w_lo, preferred_element_type=F32)
              + jnp.dot(h_lo, w_hi, preferred_element_type=F32))
    lane = lax.broadcasted_iota(jnp.int32, (tm, LANES), 1)
    lanef = lane.astype(F32)
    lg = jnp.where(lane < N_EXPERTS, logits, -jnp.inf)
    v1 = jnp.max(lg, axis=1, keepdims=True)
    i1 = jnp.min(jnp.where(lg == v1, lanef, float(LANES)), axis=1, keepdims=True)
    lg2 = jnp.where(lanef == i1, -jnp.inf, lg)
    v2 = jnp.max(lg2, axis=1, keepdims=True)
    i2 = jnp.min(jnp.where(lg2 == v2, lanef, float(LANES)), axis=1, keepdims=True)
    e = jnp.exp(v2 - v1)
    g1 = 1.0 / (1.0 + e)
    g2 = e / (1.0 + e)
    oh1 = lanef == i1
    oh2 = lanef == i2
    cnt = (oh1.astype(F32) + oh2.astype(F32))
    before = jnp.dot(lt_ref[...], cnt.astype(BF16), preferred_element_type=F32) + carry_ref[...]
    r1 = jnp.sum(jnp.where(oh1, before, 0.0), axis=1, keepdims=True)
    r2 = jnp.sum(jnp.where(oh2, before, 0.0), axis=1, keepdims=True)
    total = carry_ref[...] + jnp.sum(cnt, axis=0, keepdims=True)
    carry_ref[...] = total
    cnt_ref[...] = jnp.broadcast_to(total, cnt_ref.shape)
    meta = jnp.zeros((tm, LANES), F32)
    for k, val in enumerate((i1, i2, g1, g2, r1, r2)):
        meta = jnp.where(lane == k, val, meta)
    meta_ref[...] = meta


def _router(x, g, wr, lt):
    n = x.shape[0]
    tm = TM_BRANCH
    return pl.pallas_call(
        _router_kernel,
        grid=(n // tm,),
        in_specs=[
            pl.BlockSpec((tm, D_MODEL), lambda i: (i, 0)),
            pl.BlockSpec((1, D_MODEL), lambda i: (0, 0)),
            pl.BlockSpec((D_MODEL, LANES), lambda i: (0, 0)),
            pl.BlockSpec((tm, tm), lambda i: (0, 0)),
        ],
        out_specs=[
            pl.BlockSpec((tm, D_MODEL), lambda i: (i, 0)),
            pl.BlockSpec((tm, LANES), lambda i: (i, 0)),
            pl.BlockSpec((8, LANES), lambda i: (0, 0)),
        ],
        out_shape=[
            jax.ShapeDtypeStruct((n, D_MODEL), F32),
            jax.ShapeDtypeStruct((n, LANES), F32),
            jax.ShapeDtypeStruct((8, LANES), F32),
        ],
        scratch_shapes=[pltpu.VMEM((1, LANES), F32)],
        compiler_params=_params(32, 1),
        name="router",
    )(x, g, wr, lt)


def _row_copy(src_ref, src_row, dst_ref, dst_row, sem):
    return pltpu.make_async_copy(src_ref.at[pl.ds(src_row, 1), :], dst_ref.at[pl.ds(dst_row, 1), :], sem)


def _dispatch_kernel(bounds_ref, dest_ref, h_ref, xs_ref, zero_ref, sem, zero_sem):
    tm = h_ref.shape[0]
    n_blk = xs_ref.shape[0] // MOE_BLK

    @pl.when(pl.program_id(0) == 0)
    def _():
        zero_ref[...] = jnp.zeros_like(zero_ref)

        def zero_block(blk):
            start = pl.multiple_of(blk * MOE_BLK, MOE_BLK)
            copy = pltpu.make_async_copy(zero_ref, xs_ref.at[pl.ds(start, MOE_BLK), :], zero_sem)
            copy.start()
            copy.wait()

        for e in range(N_EXPERTS):
            @pl.when(bounds_ref[e + 1] > bounds_ref[e])
            def _():
                zero_block(bounds_ref[e + 1] // MOE_BLK - 1)

        def unused(blk, carry):
            zero_block(blk)
            return carry

        lax.fori_loop(bounds_ref[N_EXPERTS] // MOE_BLK, n_blk, unused, 0)

    def issue(r, carry):
        for k in range(2):
            _row_copy(h_ref, r, xs_ref, dest_ref[2 * r + k], sem).start(priority=k)
        return carry

    lax.fori_loop(0, tm, issue, 0, unroll=8)
    for _ in range(2):
        pltpu.make_async_copy(h_ref, xs_ref.at[pl.ds(0, tm), :], sem).wait()


def _dispatch(bounds, dest_flat, h, n_rows):
    n = h.shape[0]
    tm = TM_ROW
    return pl.pallas_call(
        _dispatch_kernel,
        grid=(n // tm,),
        in_specs=[
            pl.BlockSpec(memory_space=pltpu.SMEM),
            pl.BlockSpec((2 * tm,), lambda i: (i,), memory_space=pltpu.SMEM),
            pl.BlockSpec((tm, D_MODEL), lambda i: (i, 0)),
        ],
        out_specs=pl.BlockSpec(memory_space=pl.ANY),
        out_shape=jax.ShapeDtypeStruct((n_rows, D_MODEL), F32),
        scratch_shapes=[pltpu.VMEM((MOE_BLK, D_MODEL), F32), pltpu.SemaphoreType.DMA(()),
                        pltpu.SemaphoreType.DMA(())],
        compiler_params=_params(32, 1),
        name="moe_dispatch",
    )(bounds, dest_flat, h)


def _first_block_of_expert(b, be_ref):
    return jnp.logical_or(b == 0, be_ref[b] != be_ref[jnp.maximum(b - 1, 0)])


def _moe_kernel(be_ref, nu_ref, x_ref, w1_ref, w3_ref, w2_ref, y_ref, xb_ref, wb1_ref, wb3_ref, wb2_ref):
    b = pl.program_id(0)
    f = pl.program_id(1)

    @pl.when(b < nu_ref[0])
    def _():
        @pl.when(_first_block_of_expert(b, be_ref))
        def _():
            wb1_ref[f] = w1_ref[...].astype(BF16)
            wb3_ref[f] = w3_ref[...].astype(BF16)
            wb2_ref[f] = w2_ref[...].astype(BF16)

        @pl.when(f == 0)
        def _():
            xb_ref[...] = x_ref[...].astype(BF16)
            y_ref[...] = jnp.zeros_like(y_ref)

        xb = xb_ref[...]
        a = jnp.dot(xb, wb1_ref[f], preferred_element_type=F32)
        c = jnp.dot(xb, wb3_ref[f], preferred_element_type=F32)
        t = (a * _sigmoid(a) * c).astype(BF16)
        y_ref[...] += jnp.dot(t, wb2_ref[f], preferred_element_type=F32)

    @pl.when(jnp.logical_and(b >= nu_ref[0], f == pl.num_programs(1) - 1))
    def _():
        y_ref[...] = jnp.zeros_like(y_ref)


def _moe_experts(blk_e, n_used, xs, w1, w3, w2, layer):
    rows = xs.shape[0]
    n_blk = rows // MOE_BLK
    nf = D_FF_EXPERT // TF_MOE

    def row_idx(b, f, be, nu):
        return (jnp.minimum(b, nu[0] - 1), 0)

    def f_idx(b, f, be, nu):
        fetch = jnp.logical_and(b < nu[0], _first_block_of_expert(b, be))
        return jnp.where(fetch, f, nf - 1)

    grid_spec = pltpu.PrefetchScalarGridSpec(
        num_scalar_prefetch=2,
        grid=(n_blk, nf),
        in_specs=[
            pl.BlockSpec((MOE_BLK, D_MODEL), row_idx),
            pl.BlockSpec((None, None, D_MODEL, TF_MOE),
                         lambda b, f, be, nu: (layer, be[b], 0, f_idx(b, f, be, nu))),
            pl.BlockSpec((None, None, D_MODEL, TF_MOE),
                         lambda b, f, be, nu: (layer, be[b], 0, f_idx(b, f, be, nu))),
            pl.BlockSpec((None, None, TF_MOE, D_MODEL),
                         lambda b, f, be, nu: (layer, be[b], f_idx(b, f, be, nu), 0)),
        ],
        out_specs=pl.BlockSpec((MOE_BLK, D_MODEL), lambda b, f, be, nu: (b, 0)),
        scratch_shapes=[pltpu.VMEM((MOE_BLK, D_MODEL), BF16),
                        pltpu.VMEM((nf, D_MODEL, TF_MOE), BF16), pltpu.VMEM((nf, D_MODEL, TF_MOE), BF16),
                        pltpu.VMEM((nf, TF_MOE, D_MODEL), BF16)],
    )
    return pl.pallas_call(
        _moe_kernel,
        grid_spec=grid_spec,
        out_shape=jax.ShapeDtypeStruct((rows, D_MODEL), F32),
        compiler_params=_params(56, 2),
        name="moe_experts",
    )(blk_e, n_used, xs, w1, w3, w2)


def _combine_kernel(dest_ref, x_ref, meta_ref, gfin_ref, y_ref, o_ref, ybuf_ref, sem, *, final_norm):
    tm = x_ref.shape[0]

    def issue(r, carry):
        for k in range(2):
            _row_copy(y_ref, dest_ref[2 * r + k], ybuf_ref.at[k], r, sem).start(priority=k)
        return carry

    lax.fori_loop(0, tm, issue, 0, unroll=8)
    for k in range(2):
        pltpu.make_async_copy(y_ref.at[pl.ds(0, tm), :], ybuf_ref.at[k], sem).wait()

    g1 = meta_ref[:, 2:3]
    g2 = meta_ref[:, 3:4]
    out = x_ref[...] + (g1 * ybuf_ref[0] + g2 * ybuf_ref[1])
    if final_norm:
        out = _rms(out, gfin_ref[...])
    o_ref[...] = out


def _combine(dest_flat, x, meta, gfin, y, final_norm):
    n = x.shape[0]
    tm = TM_ROW
    kern = functools.partial(_combine_kernel, final_norm=final_norm)
    return pl.pallas_call(
        kern,
        grid=(n // tm,),
        in_specs=[
            pl.BlockSpec((2 * tm,), lambda i: (i,), memory_space=pltpu.SMEM),
            pl.BlockSpec((tm, D_MODEL), lambda i: (i, 0)),
            pl.BlockSpec((tm, LANES), lambda i: (i, 0)),
            pl.BlockSpec((1, D_MODEL), lambda i: (0, 0)),
            pl.BlockSpec(memory_space=pl.ANY),
        ],
        out_specs=pl.BlockSpec((tm, D_MODEL), lambda i: (i, 0)),
        out_shape=jax.ShapeDtypeStruct((n, D_MODEL), F32),
        scratch_shapes=[pltpu.VMEM((2, tm, D_MODEL), F32), pltpu.SemaphoreType.DMA(())],
        compiler_params=_params(32, 1),
        name="moe_combine",
    )(dest_flat, x, meta, gfin, y)


def _moe_layer(x, g, wr, lt, w1, w3, w2, layer, gfin, final_norm):
    n = x.shape[0]
    h, meta, cnt = _router(x, g, wr, lt)
    expert = meta[:, 0:2].astype(jnp.int32)
    rank = meta[:, 4:6].astype(jnp.int32)
    counts = cnt[0, :N_EXPERTS].astype(jnp.int32)
    padded = (counts + MOE_BLK - 1) // MOE_BLK * MOE_BLK
    pad_end = jnp.cumsum(padded)
    pad_start = pad_end - padded
    dest = (pad_start[expert] + rank).reshape(-1)
    n_blk = (2 * n) // MOE_BLK + N_EXPERTS
    blk_start = jnp.arange(n_blk, dtype=jnp.int32) * MOE_BLK
    blk_e = jnp.minimum(jnp.sum(blk_start[:, None] >= pad_end[None, :], axis=1), N_EXPERTS - 1)
    n_used = (pad_end[-1:] // MOE_BLK).astype(jnp.int32)
    bounds = jnp.concatenate([jnp.zeros((1,), jnp.int32), pad_end.astype(jnp.int32)])
    xs = _dispatch(bounds, dest, h, n_blk * MOE_BLK)
    y = _moe_experts(blk_e.astype(jnp.int32), n_used, xs, w1, w3, w2, layer)
    return _combine(dest, x, meta, gfin, y, final_norm)


def _prepare_mixer(w_in, b_forget, conv_w, w_spatial, b_spatial, w_conv_out, w_att_out, w_sg_out,
                   w_mix_out):
    depth, d, _ = w_in.shape

    def pad_heads(w):
        w = w.reshape(depth, d, ATT_HEADS, HEAD_DIM)
        w = jnp.pad(w, ((0, 0), (0, 0), (0, 0), (0, HEAD_PAD - HEAD_DIM)))
        return w.reshape(depth, d, ATT_HEADS * HEAD_PAD)

    prep = {}
    prep["w_main"] = jnp.concatenate(
        [pad_heads(w_in[:, :, OFF_K:OFF_V]), w_in[:, :, OFF_CONV:OFF_Q], w_in[:, :, OFF_SG:OFF_GATE],
         w_in[:, :, OFF_GATE:], pad_heads(w_in[:, :, OFF_Q:OFF_K])], axis=2).astype(BF16)
    col = jnp.arange(P_COLS)
    is_q = col >= P_Q
    spare = (col % HEAD_PAD >= HEAD_DIM) & (col % HEAD_PAD < HEAD_DIM + N_SPLIT)
    prep["col_scale"] = jnp.where(is_q, LOG2E * HEAD_DIM ** -0.5, 1.0).astype(F32)[None, :]
    prep["col_bias"] = jnp.where(is_q & spare, 1.0, 0.0).astype(F32)[None, :]
    kcol = jnp.arange(ATT_HEADS * HEAD_PAD)
    place = -((kcol[None, None, :] // HEAD_PAD == jnp.arange(LANES)[None, :, None])
              & (kcol[None, None, :] % HEAD_PAD == HEAD_DIM + jnp.arange(N_SPLIT)[:, None, None])
              ).astype(BF16)
    prep["place"] = place.reshape(N_SPLIT * LANES, ATT_HEADS * HEAD_PAD)
    prep["wf"] = jnp.pad(w_in[:, :, OFF_F:OFF_SG], ((0, 0), (0, 0), (0, LANES - ATT_HEADS))).astype(BF16)
    prep["bf"] = jnp.pad(b_forget, ((0, 0), (0, LANES - ATT_HEADS)))[:, None, :]
    wv_t = jnp.swapaxes(w_in[:, :, OFF_V:OFF_F], 1, 2).reshape(depth, ATT_HEADS // 2, 2, HEAD_DIM, d)
    zeros_h = jnp.zeros_like(wv_t[:, :, 0, :V_ROWS - HEAD_DIM])
    wvt = jnp.concatenate([wv_t[:, :, 0], zeros_h, zeros_h, wv_t[:, :, 1]], axis=2)
    prep["wvt"] = wvt.reshape(depth, ATT_HEADS * V_ROWS, d).astype(BF16)
    vrow = jnp.arange(ATT_HEADS * V_ROWS) % (2 * V_ROWS)
    prep["vbias"] = ((vrow == HEAD_DIM) | (vrow == V_ROWS)).astype(F32)[:, None]
    prep["lt_incl"] = jnp.tril(jnp.ones((LANES, LANES), F32)).astype(BF16)
    prep["conv_w"] = jnp.pad(conv_w, ((0, 0), (0, 1), (0, 0)))
    causal = jnp.tril(jnp.ones((SG_CHUNK, SG_CHUNK), bool))
    prep["w_sp"] = jnp.where(causal[None, None], w_spatial, 0).astype(BF16)
    prep["sg_bias"] = jnp.repeat(jnp.swapaxes(b_spatial, 1, 2), SG_WIDTH // SG_GROUPS, axis=2)
    for name, w in (("wc", w_conv_out), ("wa", w_att_out), ("ws", w_sg_out), ("wm", w_mix_out)):
        prep[name] = w.astype(BF16)
    return prep


def _token_mixer(xf, prep, vecs, layer, batch, seq):
    row = lambda name: vecs[name][layer][None, :]
    p, vt = _inproj(xf, row("mix_norm_g"), prep["w_main"], prep["col_scale"], prep["col_bias"],
                    prep["wf"], prep["bf"][layer], prep["lt_incl"], prep["place"],
                    prep["wvt"], prep["vbias"], seq, layer)
    hc = _conv_branch(p, prep["conv_w"][layer], row("conv_b"), row("conv_ln_g"), row("conv_ln_b"), seq)
    hs = _sg_branch(p, row("sg_ln_g"), row("sg_ln_b"), prep["w_sp"][layer], prep["sg_bias"][layer])
    ha = _attention(p, vt, batch, seq)
    return _merge(xf, hc, ha, hs, p, row("b_gate"), prep["wc"], prep["wa"], prep["ws"], prep["wm"], layer)


def kernel(x, mix_norm_g, w_in, b_forget, b_gate, conv_w, conv_b, conv_ln_g, conv_ln_b, w_conv_out,
           w_att_out, sg_ln_g, sg_ln_b, w_spatial, b_spatial, w_sg_out, w_mix_out, ffn_norm_g,
           ffn_w1, ffn_w3, ffn_w2, router_w, moe_w1, moe_w3, moe_w2, final_norm_g):
    batch, seq, d = x.shape
    depth = w_in.shape[0]
    if depth % 2 == 1:
        raise NotImplementedError("the final norm is fused into the last (expert) layer")
    xf = x.reshape(batch * seq, d)

    prep = _prepare_mixer(w_in, b_forget, conv_w, w_spatial, b_spatial, w_conv_out, w_att_out,
                          w_sg_out, w_mix_out)
    vecs = dict(mix_norm_g=mix_norm_g, conv_b=conv_b, conv_ln_g=conv_ln_g, conv_ln_b=conv_ln_b,
                sg_ln_g=sg_ln_g, sg_ln_b=sg_ln_b, b_gate=b_gate)
    f1, f3, f2 = ffn_w1, ffn_w3, ffn_w2
    m1, m3, m2 = moe_w1, moe_w3, moe_w2
    wr = jnp.pad(router_w, ((0, 0), (0, 0), (0, LANES - N_EXPERTS)))
    lt = jnp.tril(jnp.ones((TM_BRANCH, TM_BRANCH), F32), -1).astype(BF16)
    gfin = final_norm_g[None, :]

    for layer in range(depth):
        xf = _token_mixer(xf, prep, vecs, layer, batch, seq)
        g_ffn = ffn_norm_g[layer][None, :]
        i = layer // 2
        if layer % 2 == 0:
            xf = _ffn(xf, g_ffn, f1, f3, f2, i)
        else:
            xf = _moe_layer(xf, g_ffn, wr[i], lt, m1, m3, m2, i, gfin,
                            final_norm=(layer == depth - 1))
    return xf.reshape(batch, seq, d)
```

```python
import functools

import jax
import jax.numpy as jnp
from jax import lax
from jax.experimental import pallas as pl
from jax.experimental.pallas import tpu as pltpu

F32 = jnp.float32
BF16 = jnp.bfloat16

D_MODEL = 1024
CONV_CH = 512
CONV_K = 31
ATT_HEADS = 8
HEAD_DIM = 64
ATT_WIDTH = ATT_HEADS * HEAD_DIM
SG_GROUPS = 8
SG_WIDTH = 512
SG_CHUNK = 128
N_BRANCH = 3
OFF_CONV = 0
OFF_Q = OFF_CONV + 2 * CONV_CH
OFF_K = OFF_Q + ATT_WIDTH
OFF_V = OFF_K + ATT_WIDTH
OFF_F = OFF_V + ATT_WIDTH
OFF_SG = OFF_F + ATT_HEADS
OFF_GATE = OFF_SG + 2 * SG_WIDTH
D_FF = 2816
N_EXPERTS = 8
D_FF_EXPERT = 3584
EPS = 1e-6

LANES = 128
SUBLANES = 8
MIB = 1024 * 1024

HEAD_PAD = LANES
P_K = 0
P_CONV = P_K + ATT_HEADS * HEAD_PAD
P_SG = P_CONV + 2 * CONV_CH
P_GATE = P_SG + 2 * SG_WIDTH
P_Q = P_GATE + N_BRANCH * D_MODEL
P_COLS = P_Q + ATT_HEADS * HEAD_PAD
N_SPLIT = 3
V_ROWS = HEAD_PAD
LOG2E = 1.4426950408889634

TM_PROJ = 1024
TN_PROJ = 1024
TM_BRANCH = 512
CONV_HIST = 32
CONV_RB = 64
TQ = 512
TK = 512
ATT_UNROLL = 14
TM_FFN = 1024
TF_FFN = 256
MOE_BLK = 512
TF_MOE = 512
TM_ROW = 1024


def _params(vmem_mib, n_axes):
    return pltpu.CompilerParams(dimension_semantics=("arbitrary",) * n_axes,
                                vmem_limit_bytes=vmem_mib * MIB)


def _rms(x, g):
    return x * lax.rsqrt(jnp.mean(x * x, axis=-1, keepdims=True) + EPS) * g


def _layer_norm(x, g, b):
    mu = jnp.mean(x, axis=-1, keepdims=True)
    d = x - mu
    var = jnp.mean(d * d, axis=-1, keepdims=True)
    return d * lax.rsqrt(var + EPS) * g + b


def _sigmoid(x):
    return 1.0 / (1.0 + jnp.exp(-x))


def _split3(x):
    hi = x.astype(BF16)
    r1 = x - hi.astype(F32)
    mid = r1.astype(BF16)
    lo = (r1 - mid.astype(F32)).astype(BF16)
    return hi, mid, lo


def _inproj_kernel(x_ref, g_ref, w_ref, sc_ref, bi_ref, wf_ref, bf_ref, lt_ref, pl_ref, wvt_ref,
                   vb_ref, p_ref, vt_ref, xn_ref, c3_ref, carry_ref, wres_ref, *, tiles_per_batch):
    i = pl.program_id(0)
    j = pl.program_id(1)
    tm = x_ref.shape[0]
    tk = vt_ref.shape[-1]

    @pl.when(i == 0)
    def _():
        wres_ref[j] = w_ref[...]

    @pl.when(jnp.logical_and(j == 0, i % tiles_per_batch == 0))
    def _():
        carry_ref[...] = jnp.zeros_like(carry_ref)

    @pl.when(j == 0)
    def _():
        xn = _rms(x_ref[...], g_ref[...]).astype(BF16)
        xn_ref[...] = xn
        f = jnp.dot(xn, wf_ref[...], preferred_element_type=F32) + bf_ref[...]
        ls = jnp.minimum(f, 0.0) - jnp.log1p(jnp.exp(-jnp.abs(f)))
        lt = lt_ref[...]
        terms = _split3(ls)
        offset = carry_ref[...]
        for blk in range(tm // LANES):
            rows = slice(blk * LANES, (blk + 1) * LANES)
            c = sum(jnp.dot(lt, term[rows], preferred_element_type=F32) for term in terms) + offset
            offset = c[LANES - 1:LANES, :]
            for r, term in enumerate(_split3(c * LOG2E)):
                c3_ref[rows, r * LANES:(r + 1) * LANES] = term
        carry_ref[...] = offset
        vt = lax.dot_general(wvt_ref[...], xn, (((1,), (1,)), ((), ())),
                             preferred_element_type=F32) + vb_ref[...]
        for ch in range(tm // tk):
            vt_ref[ch] = vt[:, ch * tk:(ch + 1) * tk].astype(BF16)
        decay = jnp.dot(c3_ref[...], pl_ref[...], preferred_element_type=F32)
        p_ref[...] = (jnp.dot(xn, wres_ref[0], preferred_element_type=F32) + decay).astype(BF16)

    @pl.when(j > 0)
    def _():
        acc = jnp.dot(xn_ref[...], wres_ref[j], preferred_element_type=F32)
        p_ref[...] = (acc * sc_ref[...] + bi_ref[...]).astype(BF16)


def _inproj(x, g, w, col_scale, col_bias, wf, bf, lt, place, wvt, vbias, seq, layer):
    n = x.shape[0]
    tm, tn = TM_PROJ, TN_PROJ
    assert P_K == 0 and tn == ATT_HEADS * HEAD_PAD
    kern = functools.partial(_inproj_kernel, tiles_per_batch=seq // tm)
    vt_rows = ATT_HEADS * V_ROWS
    n_col = P_COLS // tn
    const = lambda shape: pl.BlockSpec(shape, lambda i, j: (0,) * len(shape))
    return pl.pallas_call(
        kern,
        grid=(n // tm, n_col),
        in_specs=[
            pl.BlockSpec((tm, D_MODEL), lambda i, j: (i, 0)),
            const((1, D_MODEL)),
            pl.BlockSpec((None, D_MODEL, tn), lambda i, j: (layer, 0, jnp.where(i == 0, j, n_col - 1))),
            pl.BlockSpec((1, tn), lambda i, j: (0, j)),
            pl.BlockSpec((1, tn), lambda i, j: (0, j)),
            pl.BlockSpec((None, D_MODEL, LANES), lambda i, j: (layer, 0, 0)),
            const((1, LANES)),
            const((LANES, LANES)),
            const((N_SPLIT * LANES, tn)),
            pl.BlockSpec((None, vt_rows, D_MODEL), lambda i, j: (layer, 0, 0)),
            const((vt_rows, 1)),
        ],
        out_specs=[
            pl.BlockSpec((tm, tn), lambda i, j: (i, j)),
            pl.BlockSpec((tm // TK, vt_rows, TK), lambda i, j: (i, 0, 0)),
        ],
        out_shape=[
            jax.ShapeDtypeStruct((n, P_COLS), BF16),
            jax.ShapeDtypeStruct((n // TK, vt_rows, TK), BF16),
        ],
        scratch_shapes=[pltpu.VMEM((tm, D_MODEL), BF16), pltpu.VMEM((tm, N_SPLIT * LANES), BF16),
                        pltpu.VMEM((1, LANES), F32), pltpu.VMEM((n_col, D_MODEL, tn), BF16)],
        compiler_params=_params(56, 2),
        name="inproj",
    )(x, g, w, col_scale, col_bias, wf, bf, lt, place, wvt, vbias)


def _conv_kernel(a1_ref, a2_ref, w_ref, cb_ref, g_ref, b_ref, o_ref, hext_ref, hsh_ref,
                 *, tiles_per_batch):
    i = pl.program_id(0)
    tm = a1_ref.shape[0]
    span = hsh_ref.shape[1]

    @pl.when(i % tiles_per_batch == 0)
    def _():
        hext_ref[0:CONV_HIST, :] = jnp.zeros((CONV_HIST, CONV_CH), F32)

    @pl.when(i % tiles_per_batch != 0)
    def _():
        hext_ref[0:CONV_HIST, :] = hext_ref[tm:tm + CONV_HIST, :]

    hext_ref[CONV_HIST:CONV_HIST + tm, :] = (
        a1_ref[...].astype(F32) * _sigmoid(a2_ref[...].astype(F32)))
    for s in range(1, SUBLANES):
        hsh_ref[s] = hext_ref[s:s + span, :]

    g = g_ref[...]
    b = b_ref[...]
    base = CONV_HIST - (CONV_K - 1)
    for r in range(0, tm, CONV_RB):
        acc = jnp.broadcast_to(cb_ref[...], (CONV_RB, CONV_CH))
        for j in range(CONV_K):
            s = (base + j) % SUBLANES
            a = r + base + j - s
            rows = hext_ref[a:a + CONV_RB, :] if s == 0 else hsh_ref[s, a:a + CONV_RB, :]
            acc = acc + w_ref[j:j + 1, :] * rows
        y = _layer_norm(acc, g, b)
        o_ref[r:r + CONV_RB, :] = (y * _sigmoid(y)).astype(BF16)


def _conv_branch(p, w, cb, g, b, seq):
    n = p.shape[0]
    tm = TM_BRANCH
    kern = functools.partial(_conv_kernel, tiles_per_batch=seq // tm)
    c0 = P_CONV // CONV_CH
    vec = pl.BlockSpec((1, CONV_CH), lambda i: (0, 0))
    return pl.pallas_call(
        kern,
        grid=(n // tm,),
        in_specs=[
            pl.BlockSpec((tm, CONV_CH), lambda i: (i, c0)),
            pl.BlockSpec((tm, CONV_CH), lambda i: (i, c0 + 1)),
            pl.BlockSpec((CONV_K + 1, CONV_CH), lambda i: (0, 0)),
            vec, vec, vec,
        ],
        out_specs=pl.BlockSpec((tm, CONV_CH), lambda i: (i, 0)),
        out_shape=jax.ShapeDtypeStruct((n, CONV_CH), BF16),
        scratch_shapes=[pltpu.VMEM((tm + CONV_HIST, CONV_CH), F32),
                        pltpu.VMEM((SUBLANES, tm + CONV_HIST - SUBLANES, CONV_CH), F32)],
        compiler_params=_params(32, 1),
        name="conv_branch",
    )(p, p, w, cb, g, b)


def _sg_kernel(u_ref, v_ref, g_ref, b_ref, w_ref, bias_ref, o_ref):
    tm = u_ref.shape[0]

    def gelu(z):
        return 0.5 * z * (1.0 + lax.erf(z * 0.7071067811865476))

    zu = gelu(u_ref[...].astype(F32))
    vn = _layer_norm(gelu(v_ref[...].astype(F32)), g_ref[...], b_ref[...]).astype(BF16)
    lane = lax.broadcasted_iota(jnp.int32, (SG_CHUNK, LANES), 1)
    first_group = lane < (SG_WIDTH // SG_GROUPS)
    for c in range(tm // SG_CHUNK):
        rows = slice(c * SG_CHUNK, (c + 1) * SG_CHUNK)
        for pr in range(SG_WIDTH // LANES):
            cols = slice(pr * LANES, (pr + 1) * LANES)
            vp = vn[rows, cols]
            m0 = jnp.dot(w_ref[2 * pr], vp, preferred_element_type=F32)
            m1 = jnp.dot(w_ref[2 * pr + 1], vp, preferred_element_type=F32)
            mixed = jnp.where(first_group, m0, m1) + bias_ref[:, cols]
            o_ref[rows, cols] = (zu[rows, cols] * mixed).astype(BF16)


def _sg_branch(p, g, b, w_tril, bias_full):
    n = p.shape[0]
    tm = TM_BRANCH
    c0 = P_SG // SG_WIDTH
    vec = pl.BlockSpec((1, SG_WIDTH), lambda i: (0, 0))
    return pl.pallas_call(
        _sg_kernel,
        grid=(n // tm,),
        in_specs=[
            pl.BlockSpec((tm, SG_WIDTH), lambda i: (i, c0)),
            pl.BlockSpec((tm, SG_WIDTH), lambda i: (i, c0 + 1)),
            vec, vec,
            pl.BlockSpec((SG_GROUPS, SG_CHUNK, SG_CHUNK), lambda i: (0, 0, 0)),
            pl.BlockSpec((SG_CHUNK, SG_WIDTH), lambda i: (0, 0)),
        ],
        out_specs=pl.BlockSpec((tm, SG_WIDTH), lambda i: (i, 0)),
        out_shape=jax.ShapeDtypeStruct((n, SG_WIDTH), BF16),
        compiler_params=_params(32, 1),
        name="sg_branch",
    )(p, p, g, b, w_tril, bias_full)


def _attn_kernel(ti_ref, tj_ref, q_ref, k_ref, vt_ref, o_ref, sa_ref, sb_ref, m_ref, acc_ref, tri_ref,
                 *, n_below, n_diag):
    tq = sa_ref.shape[-1]
    tk = vt_ref.shape[-1]
    head_cols = [slice(hh * HEAD_PAD, (hh + 1) * HEAD_PAD) for hh in range(2)]

    def scores(t, dst_ref):
        q0 = pl.multiple_of(ti_ref[t] * tq, tq)
        k0 = pl.multiple_of(tj_ref[t] * tk, tk)
        for hh, cols in enumerate(head_cols):
            dst_ref[hh] = lax.dot_general(k_ref[pl.ds(k0, tk), cols], q_ref[pl.ds(q0, tq), cols],
                                          (((1,), (1,)), ((), ())), preferred_element_type=F32)

    def consume(t, src_ref, masked):
        i = ti_ref[t]
        j = tj_ref[t]
        probs = []
        for hh in range(2):
            s = src_ref[hh]
            if masked:
                s = s + tri_ref[...]
            m = m_ref[i, hh]
            m_new = jnp.maximum(m, jnp.max(s, axis=0, keepdims=True))
            m_ref[i, hh] = m_new
            probs.append((jnp.exp2(m - m_new), jnp.exp2(s - m_new).astype(BF16)))
        for hh, (alpha, pexp) in enumerate(probs):
            pv = jnp.dot(vt_ref[j, hh * V_ROWS:(hh + 1) * V_ROWS, :], pexp, preferred_element_type=F32)
            acc_ref[i, hh] = alpha * acc_ref[i, hh] + pv

    def run(base, count, masked):
        bufs = (sa_ref, sb_ref)
        scores(base, bufs[0])

        def trip(u, carry):
            t = base + ATT_UNROLL * u
            for k in range(ATT_UNROLL):
                scores(t + k + 1, bufs[(k + 1) % 2])
                consume(t + k, bufs[k % 2], masked)
            return carry

        lax.fori_loop(0, count // ATT_UNROLL, trip, 0)
        done = count - count % ATT_UNROLL
        for k in range(done, count):
            if k + 1 < count:
                scores(base + k + 1, bufs[(k + 1) % 2])
            consume(base + k, bufs[k % 2], masked)

    tri_ref[...] = jnp.where(lax.broadcasted_iota(jnp.int32, (tk, tq), 0)
                             <= lax.broadcasted_iota(jnp.int32, (tk, tq), 1), 0.0, -jnp.inf)
    m_ref[...] = jnp.full(m_ref.shape, -jnp.inf, F32)
    acc_ref[...] = jnp.zeros(acc_ref.shape, F32)
    if n_below:
        run(0, n_below, False)
    run(n_below + 1, n_diag, True)

    eye = (lax.broadcasted_iota(jnp.int32, (tq, tq), 0)
           == lax.broadcasted_iota(jnp.int32, (tq, tq), 1)).astype(BF16)
    for i in range(q_ref.shape[0] // tq):
        even = acc_ref[i, 0]
        odd = acc_ref[i, 1]
        merged = jnp.concatenate([even[:HEAD_DIM] / even[HEAD_DIM:HEAD_DIM + 1, :],
                                  odd[V_ROWS - HEAD_DIM:] / odd[0:1, :]], axis=0)
        o_ref[i * tq:(i + 1) * tq, :] = lax.dot_general(
            eye, merged.astype(BF16), (((1,), (1,)), ((), ())),
            preferred_element_type=F32).astype(BF16)


def _attention(p, vt, batch, seq):
    n = p.shape[0]
    pairs = ATT_HEADS // 2
    assert TQ == TK
    nq = seq // TQ
    below = [(i, j) for i in range(nq) for j in range(i)]
    diag = [(i, i) for i in range(nq)]
    tiles = below + below[-1:] + diag + diag[-1:] if below else [(0, 0)] + diag + diag[-1:]
    ti = jnp.asarray([t[0] for t in tiles], jnp.int32)
    tj = jnp.asarray([t[1] for t in tiles], jnp.int32)
    pw = 2 * HEAD_PAD
    grid_spec = pltpu.PrefetchScalarGridSpec(
        num_scalar_prefetch=2,
        grid=(batch, pairs),
        in_specs=[
            pl.BlockSpec((seq, pw), lambda b, h, ti, tj: (b, P_Q // pw + h)),
            pl.BlockSpec((seq, pw), lambda b, h, ti, tj: (b, P_K // pw + h)),
            pl.BlockSpec((seq // TK, 2 * V_ROWS, TK), lambda b, h, ti, tj: (b, h, 0)),
        ],
        out_specs=pl.BlockSpec((seq, LANES), lambda b, h, ti, tj: (b, h)),
        scratch_shapes=[pltpu.VMEM((2, TK, TQ), F32), pltpu.VMEM((2, TK, TQ), F32),
                        pltpu.VMEM((nq, 2, 1, TQ), F32), pltpu.VMEM((nq, 2, V_ROWS, TQ), F32),
                        pltpu.VMEM((TK, TQ), F32)],
    )
    return pl.pallas_call(
        functools.partial(_attn_kernel, n_below=len(below), n_diag=len(diag)),
        grid_spec=grid_spec,
        out_shape=jax.ShapeDtypeStruct((n, ATT_WIDTH), BF16),
        compiler_params=_params(40, 2),
        name="fox_attention",
    )(ti, tj, p, p, vt)


def _merge_kernel(x_ref, hc_ref, ha_ref, hs_ref, g0_ref, g1_ref, g2_ref, bg_ref,
                  wc_ref, wa_ref, ws_ref, wm_ref, o_ref):
    merged = None
    for k, (h_ref, w_ref, gl_ref) in enumerate(((hc_ref, wc_ref, g0_ref), (ha_ref, wa_ref, g1_ref),
                                                (hs_ref, ws_ref, g2_ref))):
        y = jnp.dot(h_ref[...], w_ref[...], preferred_element_type=F32)
        gate = _sigmoid(gl_ref[...].astype(F32) + bg_ref[:, k * D_MODEL:(k + 1) * D_MODEL])
        merged = gate * y if merged is None else merged + gate * y
    o_ref[...] = x_ref[...] + jnp.dot(merged.astype(BF16), wm_ref[...], preferred_element_type=F32)


def _merge(x, hc, ha, hs, p, bg, wc, wa, ws, wm, layer):
    n = x.shape[0]
    tm = TM_BRANCH
    g0 = P_GATE // D_MODEL
    half = lambda: pl.BlockSpec((tm, CONV_CH), lambda i: (i, 0))
    wspec = lambda k: pl.BlockSpec((None, k, D_MODEL), lambda i: (layer, 0, 0))
    return pl.pallas_call(
        _merge_kernel,
        grid=(n // tm,),
        in_specs=[
            pl.BlockSpec((tm, D_MODEL), lambda i: (i, 0)),
            half(), half(), half(),
            pl.BlockSpec((tm, D_MODEL), lambda i: (i, g0)),
            pl.BlockSpec((tm, D_MODEL), lambda i: (i, g0 + 1)),
            pl.BlockSpec((tm, D_MODEL), lambda i: (i, g0 + 2)),
            pl.BlockSpec((1, N_BRANCH * D_MODEL), lambda i: (0, 0)),
            wspec(CONV_CH), wspec(ATT_WIDTH), wspec(SG_WIDTH), wspec(D_MODEL),
        ],
        out_specs=pl.BlockSpec((tm, D_MODEL), lambda i: (i, 0)),
        out_shape=jax.ShapeDtypeStruct((n, D_MODEL), F32),
        compiler_params=_params(48, 1),
        name="merge",
    )(x, hc, ha, hs, p, p, p, bg, wc, wa, ws, wm)


def _ffn_kernel(x_ref, g_ref, w1_ref, w3_ref, w2_ref, o_ref, hn_ref, wb1_ref, wb3_ref, wb2_ref):
    f = pl.program_id(1)

    @pl.when(pl.program_id(0) == 0)
    def _():
        wb1_ref[f] = w1_ref[...].astype(BF16)
        wb3_ref[f] = w3_ref[...].astype(BF16)
        wb2_ref[f] = w2_ref[...].astype(BF16)

    @pl.when(f == 0)
    def _():
        hn_ref[...] = _rms(x_ref[...], g_ref[...]).astype(BF16)
        o_ref[...] = x_ref[...]

    hn = hn_ref[...]
    a = jnp.dot(hn, wb1_ref[f], preferred_element_type=F32)
    b = jnp.dot(hn, wb3_ref[f], preferred_element_type=F32)
    t = (a * _sigmoid(a) * b).astype(BF16)
    o_ref[...] += jnp.dot(t, wb2_ref[f], preferred_element_type=F32)


def _ffn(x, g, w1, w3, w2, layer):
    n = x.shape[0]
    tm, tf = TM_FFN, TF_FFN
    nf = D_FF // tf

    def f_idx(i, f):
        return jnp.where(i == 0, f, nf - 1)

    return pl.pallas_call(
        _ffn_kernel,
        grid=(n // tm, nf),
        in_specs=[
            pl.BlockSpec((tm, D_MODEL), lambda i, f: (i, 0)),
            pl.BlockSpec((1, D_MODEL), lambda i, f: (0, 0)),
            pl.BlockSpec((None, D_MODEL, tf), lambda i, f: (layer, 0, f_idx(i, f))),
            pl.BlockSpec((None, D_MODEL, tf), lambda i, f: (layer, 0, f_idx(i, f))),
            pl.BlockSpec((None, tf, D_MODEL), lambda i, f: (layer, f_idx(i, f), 0)),
        ],
        out_specs=pl.BlockSpec((tm, D_MODEL), lambda i, f: (i, 0)),
        out_shape=jax.ShapeDtypeStruct((n, D_MODEL), F32),
        scratch_shapes=[pltpu.VMEM((tm, D_MODEL), BF16),
                        pltpu.VMEM((nf, D_MODEL, tf), BF16), pltpu.VMEM((nf, D_MODEL, tf), BF16),
                        pltpu.VMEM((nf, tf, D_MODEL), BF16)],
        compiler_params=_params(56, 2),
        name="ffn_dense",
    )(x, g, w1, w3, w2)


def _router_kernel(x_ref, g_ref, wr_ref, lt_ref, h_ref, meta_ref, cnt_ref, carry_ref):
    i = pl.program_id(0)
    tm = x_ref.shape[0]

    @pl.when(i == 0)
    def _():
        carry_ref[...] = jnp.zeros_like(carry_ref)

    h = _rms(x_ref[...], g_ref[...])
    h_ref[...] = h
    h_hi = h.astype(BF16)
    h_lo = (h - h_hi.astype(F32)).astype(BF16)
    w = wr_ref[...]
    w_hi = w.astype(BF16)
    w_lo = (w - w_hi.astype(F32)).astype(BF16)
    logits = (jnp.dot(h_hi, w_hi, preferred_element_type=F32)
              + jnp.dot(h_hi, w_lo, preferred_element_type=F32)
              + jnp.dot(h_lo, w_hi, preferred_element_type=F32))
    lane = lax.broadcasted_iota(jnp.int32, (tm, LANES), 1)
    lanef = lane.astype(F32)
    lg = jnp.where(lane < N_EXPERTS, logits, -jnp.inf)
    v1 = jnp.max(lg, axis=1, keepdims=True)
    i1 = jnp.min(jnp.where(lg == v1, lanef, float(LANES)), axis=1, keepdims=True)
    lg2 = jnp.where(lanef == i1, -jnp.inf, lg)
    v2 = jnp.max(lg2, axis=1, keepdims=True)
    i2 = jnp.min(jnp.where(lg2 == v2, lanef, float(LANES)), axis=1, keepdims=True)
    e = jnp.exp(v2 - v1)
    g1 = 1.0 / (1.0 + e)
    g2 = e / (1.0 + e)
    oh1 = lanef == i1
    oh2 = lanef == i2
    cnt = (oh1.astype(F32) + oh2.astype(F32))
    before = jnp.dot(lt_ref[...], cnt.astype(BF16), preferred_element_type=F32) + carry_ref[...]
    r1 = jnp.sum(jnp.where(oh1, before, 0.0), axis=1, keepdims=True)
    r2 = jnp.sum(jnp.where(oh2, before, 0.0), axis=1, keepdims=True)
    total = carry_ref[...] + jnp.sum(cnt, axis=0, keepdims=True)
    carry_ref[...] = total
    cnt_ref[...] = jnp.broadcast_to(total, cnt_ref.shape)
    meta = jnp.zeros((tm, LANES), F32)
    for k, val in enumerate((i1, i2, g1, g2, r1, r2)):
        meta = jnp.where(lane == k, val, meta)
    meta_ref[...] = meta


def _router(x, g, wr, lt):
    n = x.shape[0]
    tm = TM_BRANCH
    return pl.pallas_call(
        _router_kernel,
        grid=(n // tm,),
        in_specs=[
            pl.BlockSpec((tm, D_MODEL), lambda i: (i, 0)),
            pl.BlockSpec((1, D_MODEL), lambda i: (0, 0)),
            pl.BlockSpec((D_MODEL, LANES), lambda i: (0, 0)),
            pl.BlockSpec((tm, tm), lambda i: (0, 0)),
        ],
        out_specs=[
            pl.BlockSpec((tm, D_MODEL), lambda i: (i, 0)),
            pl.BlockSpec((tm, LANES), lambda i: (i, 0)),
            pl.BlockSpec((8, LANES), lambda i: (0, 0)),
        ],
        out_shape=[
            jax.ShapeDtypeStruct((n, D_MODEL), F32),
            jax.ShapeDtypeStruct((n, LANES), F32),
            jax.ShapeDtypeStruct((8, LANES), F32),
        ],
        scratch_shapes=[pltpu.VMEM((1, LANES), F32)],
        compiler_params=_params(32, 1),
        name="router",
    )(x, g, wr, lt)


def _row_copy(src_ref, src_row, dst_ref, dst_row, sem):
    return pltpu.make_async_copy(src_ref.at[pl.ds(src_row, 1), :], dst_ref.at[pl.ds(dst_row, 1), :], sem)


def _dispatch_kernel(bounds_ref, dest_ref, h_ref, xs_ref, zero_ref, sem, zero_sem):
    tm = h_ref.shape[0]
    n_blk = xs_ref.shape[0] // MOE_BLK

    @pl.when(pl.program_id(0) == 0)
    def _():
        zero_ref[...] = jnp.zeros_like(zero_ref)

        def zero_block(blk):
            start = pl.multiple_of(blk * MOE_BLK, MOE_BLK)
            copy = pltpu.make_async_copy(zero_ref, xs_ref.at[pl.ds(start, MOE_BLK), :], zero_sem)
            copy.start()
            copy.wait()

        for e in range(N_EXPERTS):
            @pl.when(bounds_ref[e + 1] > bounds_ref[e])
            def _():
                zero_block(bounds_ref[e + 1] // MOE_BLK - 1)

        def unused(blk, carry):
            zero_block(blk)
            return carry

        lax.fori_loop(bounds_ref[N_EXPERTS] // MOE_BLK, n_blk, unused, 0)

    def issue(r, carry):
        for k in range(2):
            _row_copy(h_ref, r, xs_ref, dest_ref[2 * r + k], sem).start(priority=k)
        return carry

    lax.fori_loop(0, tm, issue, 0, unroll=8)
    for _ in range(2):
        pltpu.make_async_copy(h_ref, xs_ref.at[pl.ds(0, tm), :], sem).wait()


def _dispatch(bounds, dest_flat, h, n_rows):
    n = h.shape[0]
    tm = TM_ROW
    return pl.pallas_call(
        _dispatch_kernel,
        grid=(n // tm,),
        in_specs=[
            pl.BlockSpec(memory_space=pltpu.SMEM),
            pl.BlockSpec((2 * tm,), lambda i: (i,), memory_space=pltpu.SMEM),
            pl.BlockSpec((tm, D_MODEL), lambda i: (i, 0)),
        ],
        out_specs=pl.BlockSpec(memory_space=pl.ANY),
        out_shape=jax.ShapeDtypeStruct((n_rows, D_MODEL), F32),
        scratch_shapes=[pltpu.VMEM((MOE_BLK, D_MODEL), F32), pltpu.SemaphoreType.DMA(()),
                        pltpu.SemaphoreType.DMA(())],
        compiler_params=_params(32, 1),
        name="moe_dispatch",
    )(bounds, dest_flat, h)


def _first_block_of_expert(b, be_ref):
    return jnp.logical_or(b == 0, be_ref[b] != be_ref[jnp.maximum(b - 1, 0)])


def _moe_kernel(be_ref, nu_ref, x_ref, w1_ref, w3_ref, w2_ref, y_ref, xb_ref, wb1_ref, wb3_ref, wb2_ref):
    b = pl.program_id(0)
    f = pl.program_id(1)

    @pl.when(b < nu_ref[0])
    def _():
        @pl.when(_first_block_of_expert(b, be_ref))
        def _():
            wb1_ref[f] = w1_ref[...].astype(BF16)
            wb3_ref[f] = w3_ref[...].astype(BF16)
            wb2_ref[f] = w2_ref[...].astype(BF16)

        @pl.when(f == 0)
        def _():
            xb_ref[...] = x_ref[...].astype(BF16)
            y_ref[...] = jnp.zeros_like(y_ref)

        xb = xb_ref[...]
        a = jnp.dot(xb, wb1_ref[f], preferred_element_type=F32)
        c = jnp.dot(xb, wb3_ref[f], preferred_element_type=F32)
        t = (a * _sigmoid(a) * c).astype(BF16)
        y_ref[...] += jnp.dot(t, wb2_ref[f], preferred_element_type=F32)

    @pl.when(jnp.logical_and(b >= nu_ref[0], f == pl.num_programs(1) - 1))
    def _():
        y_ref[...] = jnp.zeros_like(y_ref)


def _moe_experts(blk_e, n_used, xs, w1, w3, w2, layer):
    rows = xs.shape[0]
    n_blk = rows // MOE_BLK
    nf = D_FF_EXPERT // TF_MOE

    def row_idx(b, f, be, nu):
        return (jnp.minimum(b, nu[0] - 1), 0)

    def f_idx(b, f, be, nu):
        fetch = jnp.logical_and(b < nu[0], _first_block_of_expert(b, be))
        return jnp.where(fetch, f, nf - 1)

    grid_spec = pltpu.PrefetchScalarGridSpec(
        num_scalar_prefetch=2,
        grid=(n_blk, nf),
        in_specs=[
            pl.BlockSpec((MOE_BLK, D_MODEL), row_idx),
            pl.BlockSpec((None, None, D_MODEL, TF_MOE),
                         lambda b, f, be, nu: (layer, be[b], 0, f_idx(b, f, be, nu))),
            pl.BlockSpec((None, None, D_MODEL, TF_MOE),
                         lambda b, f, be, nu: (layer, be[b], 0, f_idx(b, f, be, nu))),
            pl.BlockSpec((None, None, TF_MOE, D_MODEL),
                         lambda b, f, be, nu: (layer, be[b], f_idx(b, f, be, nu), 0)),
        ],
        out_specs=pl.BlockSpec((MOE_BLK, D_MODEL), lambda b, f, be, nu: (b, 0)),
        scratch_shapes=[pltpu.VMEM((MOE_BLK, D_MODEL), BF16),
                        pltpu.VMEM((nf, D_MODEL, TF_MOE), BF16), pltpu.VMEM((nf, D_MODEL, TF_MOE), BF16),
                        pltpu.VMEM((nf, TF_MOE, D_MODEL), BF16)],
    )
    return pl.pallas_call(
        _moe_kernel,
        grid_spec=grid_spec,
        out_shape=jax.ShapeDtypeStruct((rows, D_MODEL), F32),
        compiler_params=_params(56, 2),
        name="moe_experts",
    )(blk_e, n_used, xs, w1, w3, w2)


def _combine_kernel(dest_ref, x_ref, meta_ref, gfin_ref, y_ref, o_ref, ybuf_ref, sem, *, final_norm):
    tm = x_ref.shape[0]

    def issue(r, carry):
        for k in range(2):
            _row_copy(y_ref, dest_ref[2 * r + k], ybuf_ref.at[k], r, sem).start(priority=k)
        return carry

    lax.fori_loop(0, tm, issue, 0, unroll=8)
    for k in range(2):
        pltpu.make_async_copy(y_ref.at[pl.ds(0, tm), :], ybuf_ref.at[k], sem).wait()

    g1 = meta_ref[:, 2:3]
    g2 = meta_ref[:, 3:4]
    out = x_ref[...] + (g1 * ybuf_ref[0] + g2 * ybuf_ref[1])
    if final_norm:
        out = _rms(out, gfin_ref[...])
    o_ref[...] = out


def _combine(dest_flat, x, meta, gfin, y, final_norm):
    n = x.shape[0]
    tm = TM_ROW
    kern = functools.partial(_combine_kernel, final_norm=final_norm)
    return pl.pallas_call(
        kern,
        grid=(n // tm,),
        in_specs=[
            pl.BlockSpec((2 * tm,), lambda i: (i,), memory_space=pltpu.SMEM),
            pl.BlockSpec((tm, D_MODEL), lambda i: (i, 0)),
            pl.BlockSpec((tm, LANES), lambda i: (i, 0)),
            pl.BlockSpec((1, D_MODEL), lambda i: (0, 0)),
            pl.BlockSpec(memory_space=pl.ANY),
        ],
        out_specs=pl.BlockSpec((tm, D_MODEL), lambda i: (i, 0)),
        out_shape=jax.ShapeDtypeStruct((n, D_MODEL), F32),
        scratch_shapes=[pltpu.VMEM((2, tm, D_MODEL), F32), pltpu.SemaphoreType.DMA(())],
        compiler_params=_params(32, 1),
        name="moe_combine",
    )(dest_flat, x, meta, gfin, y)


def _moe_layer(x, g, wr, lt, w1, w3, w2, layer, gfin, final_norm):
    n = x.shape[0]
    h, meta, cnt = _router(x, g, wr, lt)
    expert = meta[:, 0:2].astype(jnp.int32)
    rank = meta[:, 4:6].astype(jnp.int32)
    counts = cnt[0, :N_EXPERTS].astype(jnp.int32)
    padded = (counts + MOE_BLK - 1) // MOE_BLK * MOE_BLK
    pad_end = jnp.cumsum(padded)
    pad_start = pad_end - padded
    dest = (pad_start[expert] + rank).reshape(-1)
    n_blk = (2 * n) // MOE_BLK + N_EXPERTS
    blk_start = jnp.arange(n_blk, dtype=jnp.int32) * MOE_BLK
    blk_e = jnp.minimum(jnp.sum(blk_start[:, None] >= pad_end[None, :], axis=1), N_EXPERTS - 1)
    n_used = (pad_end[-1:] // MOE_BLK).astype(jnp.int32)
    bounds = jnp.concatenate([jnp.zeros((1,), jnp.int32), pad_end.astype(jnp.int32)])
    xs = _dispatch(bounds, dest, h, n_blk * MOE_BLK)
    y = _moe_experts(blk_e.astype(jnp.int32), n_used, xs, w1, w3, w2, layer)
    return _combine(dest, x, meta, gfin, y, final_norm)


def _prepare_mixer(w_in, b_forget, conv_w, w_spatial, b_spatial, w_conv_out, w_att_out, w_sg_out,
                   w_mix_out):
    depth, d, _ = w_in.shape

    def pad_heads(w):
        w = w.reshape(depth, d, ATT_HEADS, HEAD_DIM)
        w = jnp.pad(w, ((0, 0), (0, 0), (0, 0), (0, HEAD_PAD - HEAD_DIM)))
        return w.reshape(depth, d, ATT_HEADS * HEAD_PAD)

    prep = {}
    prep["w_main"] = jnp.concatenate(
        [pad_heads(w_in[:, :, OFF_K:OFF_V]), w_in[:, :, OFF_CONV:OFF_Q], w_in[:, :, OFF_SG:OFF_GATE],
         w_in[:, :, OFF_GATE:], pad_heads(w_in[:, :, OFF_Q:OFF_K])], axis=2).astype(BF16)
    col = jnp.arange(P_COLS)
    is_q = col >= P_Q
    spare = (col % HEAD_PAD >= HEAD_DIM) & (col % HEAD_PAD < HEAD_DIM + N_SPLIT)
    prep["col_scale"] = jnp.where(is_q, LOG2E * HEAD_DIM ** -0.5, 1.0).astype(F32)[None, :]
    prep["col_bias"] = jnp.where(is_q & spare, 1.0, 0.0).astype(F32)[None, :]
    kcol = jnp.arange(ATT_HEADS * HEAD_PAD)
    place = -((kcol[None, None, :] // HEAD_PAD == jnp.arange(LANES)[None, :, None])
              & (kcol[None, None, :] % HEAD_PAD == HEAD_DIM + jnp.arange(N_SPLIT)[:, None, None])
              ).astype(BF16)
    prep["place"] = place.reshape(N_SPLIT * LANES, ATT_HEADS * HEAD_PAD)
    prep["wf"] = jnp.pad(w_in[:, :, OFF_F:OFF_SG], ((0, 0), (0, 0), (0, LANES - ATT_HEADS))).astype(BF16)
    prep["bf"] = jnp.pad(b_forget, ((0, 0), (0, LANES - ATT_HEADS)))[:, None, :]
    wv_t = jnp.swapaxes(w_in[:, :, OFF_V:OFF_F], 1, 2).reshape(depth, ATT_HEADS // 2, 2, HEAD_DIM, d)
    zeros_h = jnp.zeros_like(wv_t[:, :, 0, :V_ROWS - HEAD_DIM])
    wvt = jnp.concatenate([wv_t[:, :, 0], zeros_h, zeros_h, wv_t[:, :, 1]], axis=2)
    prep["wvt"] = wvt.reshape(depth, ATT_HEADS * V_ROWS, d).astype(BF16)
    vrow = jnp.arange(ATT_HEADS * V_ROWS) % (2 * V_ROWS)
    prep["vbias"] = ((vrow == HEAD_DIM) | (vrow == V_ROWS)).astype(F32)[:, None]
    prep["lt_incl"] = jnp.tril(jnp.ones((LANES, LANES), F32)).astype(BF16)
    prep["conv_w"] = jnp.pad(conv_w, ((0, 0), (0, 1), (0, 0)))
    causal = jnp.tril(jnp.ones((SG_CHUNK, SG_CHUNK), bool))
    prep["w_sp"] = jnp.where(causal[None, None], w_spatial, 0).astype(BF16)
    prep["sg_bias"] = jnp.repeat(jnp.swapaxes(b_spatial, 1, 2), SG_WIDTH // SG_GROUPS, axis=2)
    for name, w in (("wc", w_conv_out), ("wa", w_att_out), ("ws", w_sg_out), ("wm", w_mix_out)):
        prep[name] = w.astype(BF16)
    return prep


def _token_mixer(xf, prep, vecs, layer, batch, seq):
    row = lambda name: vecs[name][layer][None, :]
    p, vt = _inproj(xf, row("mix_norm_g"), prep["w_main"], prep["col_scale"], prep["col_bias"],
                    prep["wf"], prep["bf"][layer], prep["lt_incl"], prep["place"],
                    prep["wvt"], prep["vbias"], seq, layer)
    hc = _conv_branch(p, prep["conv_w"][layer], row("conv_b"), row("conv_ln_g"), row("conv_ln_b"), seq)
    hs = _sg_branch(p, row("sg_ln_g"), row("sg_ln_b"), prep["w_sp"][layer], prep["sg_bias"][layer])
    ha = _attention(p, vt, batch, seq)
    return _merge(xf, hc, ha, hs, p, row("b_gate"), prep["wc"], prep["wa"], prep["ws"], prep["wm"], layer)


def kernel(x, mix_norm_g, w_in, b_forget, b_gate, conv_w, conv_b, conv_ln_g, conv_ln_b, w_conv_out,
           w_att_out, sg_ln_g, sg_ln_b, w_spatial, b_spatial, w_sg_out, w_mix_out, ffn_norm_g,
           ffn_w1, ffn_w3, ffn_w2, router_w, moe_w1, moe_w3, moe_w2, final_norm_g):
    batch, seq, d = x.shape
    depth = w_in.shape[0]
    if depth % 2 == 1:
        raise NotImplementedError("the final norm is fused into the last (expert) layer")
    xf = x.reshape(batch * seq, d)

    prep = _prepare_mixer(w_in, b_forget, conv_w, w_spatial, b_spatial, w_conv_out, w_att_out,
                          w_sg_out, w_mix_out)
    vecs = dict(mix_norm_g=mix_norm_g, conv_b=conv_b, conv_ln_g=conv_ln_g, conv_ln_b=conv_ln_b,
                sg_ln_g=sg_ln_g, sg_ln_b=sg_ln_b, b_gate=b_gate)
    f1, f3, f2 = ffn_w1, ffn_w3, ffn_w2
    m1, m3, m2 = moe_w1, moe_w3, moe_w2
    wr = jnp.pad(router_w, ((0, 0), (0, 0), (0, LANES - N_EXPERTS)))
    lt = jnp.tril(jnp.ones((TM_BRANCH, TM_BRANCH), F32), -1).astype(BF16)
    gfin = final_norm_g[None, :]

    for layer in range(depth):
        xf = _token_mixer(xf, prep, vecs, layer, batch, seq)
        g_ffn = ffn_norm_g[layer][None, :]
        i = layer // 2
        if layer % 2 == 0:
            xf = _ffn(xf, g_ffn, f1, f3, f2, i)
        else:
            xf = _moe_layer(xf, g_ffn, wr[i], lt, m1, m3, m2, i, gfin,
                            final_norm=(layer == depth - 1))
    return xf.reshape(batch, seq, d)
```

```python
import functools

import jax
import jax.numpy as jnp
from jax import lax
from jax.experimental import pallas as pl
from jax.experimental.pallas import tpu as pltpu

F32 = jnp.float32
BF16 = jnp.bfloat16

D_MODEL = 1024
CONV_CH = 512
CONV_K = 31
ATT_HEADS = 8
HEAD_DIM = 64
ATT_WIDTH = ATT_HEADS * HEAD_DIM
SG_GROUPS = 8
SG_WIDTH = 512
SG_CHUNK = 128
N_BRANCH = 3
OFF_CONV = 0
OFF_Q = OFF_CONV + 2 * CONV_CH
OFF_K = OFF_Q + ATT_WIDTH
OFF_V = OFF_K + ATT_WIDTH
OFF_F = OFF_V + ATT_WIDTH
OFF_SG = OFF_F + ATT_HEADS
OFF_GATE = OFF_SG + 2 * SG_WIDTH
D_FF = 2816
N_EXPERTS = 8
D_FF_EXPERT = 3584
EPS = 1e-6

LANES = 128
SUBLANES = 8
MIB = 1024 * 1024

HEAD_PAD = LANES
P_K = 0
P_CONV = P_K + ATT_HEADS * HEAD_PAD
P_SG = P_CONV + 2 * CONV_CH
P_GATE = P_SG + 2 * SG_WIDTH
P_Q = P_GATE + N_BRANCH * D_MODEL
P_COLS = P_Q + ATT_HEADS * HEAD_PAD
N_SPLIT = 3
V_ROWS = HEAD_PAD
LOG2E = 1.4426950408889634

TM_PROJ = 1024
TN_PROJ = 1024
TM_BRANCH = 512
CONV_HIST = 32
CONV_RB = 64
TQ = 512
TK = 512
ATT_UNROLL = 14
TM_FFN = 1024
TF_FFN = 256
MOE_BLK = 512
TF_MOE = 512
TM_ROW = 1024
ROW_ISSUE_UNROLL = 16


def _params(vmem_mib, n_axes):
    return pltpu.CompilerParams(dimension_semantics=("arbitrary",) * n_axes,
                                vmem_limit_bytes=vmem_mib * MIB)


def _rms(x, g):
    return x * lax.rsqrt(jnp.mean(x * x, axis=-1, keepdims=True) + EPS) * g


def _layer_norm(x, g, b):
    mu = jnp.mean(x, axis=-1, keepdims=True)
    d = x - mu
    var = jnp.mean(d * d, axis=-1, keepdims=True)
    return d * lax.rsqrt(var + EPS) * g + b


def _sigmoid(x):
    return 1.0 / (1.0 + jnp.exp(-x))


def _split3(x):
    hi = x.astype(BF16)
    r1 = x - hi.astype(F32)
    mid = r1.astype(BF16)
    lo = (r1 - mid.astype(F32)).astype(BF16)
    return hi, mid, lo


def _inproj_kernel(x_ref, g_ref, w_ref, sc_ref, bi_ref, wf_ref, bf_ref, lt_ref, pl_ref, wvt_ref,
                   vb_ref, p_ref, vt_ref, xn_ref, c3_ref, carry_ref, wres_ref, *, tiles_per_batch):
    i = pl.program_id(0)
    j = pl.program_id(1)
    tm = x_ref.shape[0]
    tk = vt_ref.shape[-1]

    @pl.when(i == 0)
    def _():
        wres_ref[j] = w_ref[...]

    @pl.when(jnp.logical_and(j == 0, i % tiles_per_batch == 0))
    def _():
        carry_ref[...] = jnp.zeros_like(carry_ref)

    @pl.when(j == 0)
    def _():
        xn = _rms(x_ref[...], g_ref[...]).astype(BF16)
        xn_ref[...] = xn
        f = jnp.dot(xn, wf_ref[...], preferred_element_type=F32) + bf_ref[...]
        ls = jnp.minimum(f, 0.0) - jnp.log1p(jnp.exp(-jnp.abs(f)))
        lt = lt_ref[...]
        terms = _split3(ls)
        offset = carry_ref[...]
        for blk in range(tm // LANES):
            rows = slice(blk * LANES, (blk + 1) * LANES)
            c = sum(jnp.dot(lt, term[rows], preferred_element_type=F32) for term in terms) + offset
            offset = c[LANES - 1:LANES, :]
            for r, term in enumerate(_split3(c * LOG2E)):
                c3_ref[rows, r * LANES:(r + 1) * LANES] = term
        carry_ref[...] = offset
        vt = lax.dot_general(wvt_ref[...], xn, (((1,), (1,)), ((), ())),
                             preferred_element_type=F32) + vb_ref[...]
        for ch in range(tm // tk):
            vt_ref[ch] = vt[:, ch * tk:(ch + 1) * tk].astype(BF16)
        decay = jnp.dot(c3_ref[...], pl_ref[...], preferred_element_type=F32)
        p_ref[...] = (jnp.dot(xn, wres_ref[0], preferred_element_type=F32) + decay).astype(BF16)

    @pl.when(j > 0)
    def _():
        acc = jnp.dot(xn_ref[...], wres_ref[j], preferred_element_type=F32)
        p_ref[...] = (acc * sc_ref[...] + bi_ref[...]).astype(BF16)


def _inproj(x, g, w, col_scale, col_bias, wf, bf, lt, place, wvt, vbias, seq, layer):
    n = x.shape[0]
    tm, tn = TM_PROJ, TN_PROJ
    assert P_K == 0 and tn == ATT_HEADS * HEAD_PAD
    kern = functools.partial(_inproj_kernel, tiles_per_batch=seq // tm)
    vt_rows = ATT_HEADS * V_ROWS
    n_col = P_COLS // tn
    const = lambda shape: pl.BlockSpec(shape, lambda i, j: (0,) * len(shape))
    return pl.pallas_call(
        kern,
        grid=(n // tm, n_col),
        in_specs=[
            pl.BlockSpec((tm, D_MODEL), lambda i, j: (i, 0)),
            const((1, D_MODEL)),
            pl.BlockSpec((None, D_MODEL, tn), lambda i, j: (layer, 0, jnp.where(i == 0, j, n_col - 1))),
            pl.BlockSpec((1, tn), lambda i, j: (0, j)),
            pl.BlockSpec((1, tn), lambda i, j: (0, j)),
            pl.BlockSpec((None, D_MODEL, LANES), lambda i, j: (layer, 0, 0)),
            const((1, LANES)),
            const((LANES, LANES)),
            const((N_SPLIT * LANES, tn)),
            pl.BlockSpec((None, vt_rows, D_MODEL), lambda i, j: (layer, 0, 0)),
            const((vt_rows, 1)),
        ],
        out_specs=[
            pl.BlockSpec((tm, tn), lambda i, j: (i, j)),
            pl.BlockSpec((tm // TK, vt_rows, TK), lambda i, j: (i, 0, 0)),
        ],
        out_shape=[
            jax.ShapeDtypeStruct((n, P_COLS), BF16),
            jax.ShapeDtypeStruct((n // TK, vt_rows, TK), BF16),
        ],
        scratch_shapes=[pltpu.VMEM((tm, D_MODEL), BF16), pltpu.VMEM((tm, N_SPLIT * LANES), BF16),
                        pltpu.VMEM((1, LANES), F32), pltpu.VMEM((n_col, D_MODEL, tn), BF16)],
        compiler_params=_params(56, 2),
        name="inproj",
    )(x, g, w, col_scale, col_bias, wf, bf, lt, place, wvt, vbias)


def _conv_kernel(a1_ref, a2_ref, w_ref, cb_ref, g_ref, b_ref, o_ref, hext_ref, hsh_ref,
                 *, tiles_per_batch):
    i = pl.program_id(0)
    tm = a1_ref.shape[0]
    span = hsh_ref.shape[1]

    @pl.when(i % tiles_per_batch == 0)
    def _():
        hext_ref[0:CONV_HIST, :] = jnp.zeros((CONV_HIST, CONV_CH), F32)

    @pl.when(i % tiles_per_batch != 0)
    def _():
        hext_ref[0:CONV_HIST, :] = hext_ref[tm:tm + CONV_HIST, :]

    hext_ref[CONV_HIST:CONV_HIST + tm, :] = (
        a1_ref[...].astype(F32) * _sigmoid(a2_ref[...].astype(F32)))
    for s in range(1, SUBLANES):
        hsh_ref[s] = hext_ref[s:s + span, :]

    g = g_ref[...]
    b = b_ref[...]
    base = CONV_HIST - (CONV_K - 1)
    for r in range(0, tm, CONV_RB):
        acc = jnp.broadcast_to(cb_ref[...], (CONV_RB, CONV_CH))
        for j in range(CONV_K):
            s = (base + j) % SUBLANES
            a = r + base + j - s
            rows = hext_ref[a:a + CONV_RB, :] if s == 0 else hsh_ref[s, a:a + CONV_RB, :]
            acc = acc + w_ref[j:j + 1, :] * rows
        y = _layer_norm(acc, g, b)
        o_ref[r:r + CONV_RB, :] = (y * _sigmoid(y)).astype(BF16)


def _conv_branch(p, w, cb, g, b, seq):
    n = p.shape[0]
    tm = TM_BRANCH
    kern = functools.partial(_conv_kernel, tiles_per_batch=seq // tm)
    c0 = P_CONV // CONV_CH
    vec = pl.BlockSpec((1, CONV_CH), lambda i: (0, 0))
    return pl.pallas_call(
        kern,
        grid=(n // tm,),
        in_specs=[
            pl.BlockSpec((tm, CONV_CH), lambda i: (i, c0)),
            pl.BlockSpec((tm, CONV_CH), lambda i: (i, c0 + 1)),
            pl.BlockSpec((CONV_K + 1, CONV_CH), lambda i: (0, 0)),
            vec, vec, vec,
        ],
        out_specs=pl.BlockSpec((tm, CONV_CH), lambda i: (i, 0)),
        out_shape=jax.ShapeDtypeStruct((n, CONV_CH), BF16),
        scratch_shapes=[pltpu.VMEM((tm + CONV_HIST, CONV_CH), F32),
                        pltpu.VMEM((SUBLANES, tm + CONV_HIST - SUBLANES, CONV_CH), F32)],
        compiler_params=_params(32, 1),
        name="conv_branch",
    )(p, p, w, cb, g, b)


def _sg_kernel(u_ref, v_ref, g_ref, b_ref, w_ref, bias_ref, o_ref):
    tm = u_ref.shape[0]

    def gelu(z):
        return 0.5 * z * (1.0 + lax.erf(z * 0.7071067811865476))

    zu = gelu(u_ref[...].astype(F32))
    vn = _layer_norm(gelu(v_ref[...].astype(F32)), g_ref[...], b_ref[...]).astype(BF16)
    lane = lax.broadcasted_iota(jnp.int32, (SG_CHUNK, LANES), 1)
    first_group = lane < (SG_WIDTH // SG_GROUPS)
    for c in range(tm // SG_CHUNK):
        rows = slice(c * SG_CHUNK, (c + 1) * SG_CHUNK)
        for pr in range(SG_WIDTH // LANES):
            cols = slice(pr * LANES, (pr + 1) * LANES)
            vp = vn[rows, cols]
            m0 = jnp.dot(w_ref[2 * pr], vp, preferred_element_type=F32)
            m1 = jnp.dot(w_ref[2 * pr + 1], vp, preferred_element_type=F32)
            mixed = jnp.where(first_group, m0, m1) + bias_ref[:, cols]
            o_ref[rows, cols] = (zu[rows, cols] * mixed).astype(BF16)


def _sg_branch(p, g, b, w_tril, bias_full):
    n = p.shape[0]
    tm = TM_BRANCH
    c0 = P_SG // SG_WIDTH
    vec = pl.BlockSpec((1, SG_WIDTH), lambda i: (0, 0))
    return pl.pallas_call(
        _sg_kernel,
        grid=(n // tm,),
        in_specs=[
            pl.BlockSpec((tm, SG_WIDTH), lambda i: (i, c0)),
            pl.BlockSpec((tm, SG_WIDTH), lambda i: (i, c0 + 1)),
            vec, vec,
            pl.BlockSpec((SG_GROUPS, SG_CHUNK, SG_CHUNK), lambda i: (0, 0, 0)),
            pl.BlockSpec((SG_CHUNK, SG_WIDTH), lambda i: (0, 0)),
        ],
        out_specs=pl.BlockSpec((tm, SG_WIDTH), lambda i: (i, 0)),
        out_shape=jax.ShapeDtypeStruct((n, SG_WIDTH), BF16),
        compiler_params=_params(32, 1),
        name="sg_branch",
    )(p, p, g, b, w_tril, bias_full)


def _attn_kernel(ti_ref, tj_ref, q_ref, k_ref, vt_ref, o_ref, sa_ref, sb_ref, m_ref, acc_ref, tri_ref,
                 *, n_below, n_diag):
    tq = sa_ref.shape[-1]
    tk = vt_ref.shape[-1]
    head_cols = [slice(hh * HEAD_PAD, (hh + 1) * HEAD_PAD) for hh in range(2)]

    def scores(t, dst_ref):
        q0 = pl.multiple_of(ti_ref[t] * tq, tq)
        k0 = pl.multiple_of(tj_ref[t] * tk, tk)
        for hh, cols in enumerate(head_cols):
            dst_ref[hh] = lax.dot_general(k_ref[pl.ds(k0, tk), cols], q_ref[pl.ds(q0, tq), cols],
                                          (((1,), (1,)), ((), ())), preferred_element_type=F32)

    def consume(t, src_ref, masked):
        i = ti_ref[t]
        j = tj_ref[t]
        probs = []
        for hh in range(2):
            s = src_ref[hh]
            if masked:
                s = s + tri_ref[...]
            m = m_ref[i, hh]
            m_new = jnp.maximum(m, jnp.max(s, axis=0, keepdims=True))
            m_ref[i, hh] = m_new
            probs.append((jnp.exp2(m - m_new), jnp.exp2(s - m_new).astype(BF16)))
        for hh, (alpha, pexp) in enumerate(probs):
            pv = jnp.dot(vt_ref[j, hh * V_ROWS:(hh + 1) * V_ROWS, :], pexp, preferred_element_type=F32)
            acc_ref[i, hh] = alpha * acc_ref[i, hh] + pv

    def run(base, count, masked):
        bufs = (sa_ref, sb_ref)
        scores(base, bufs[0])

        def trip(u, carry):
            t = base + ATT_UNROLL * u
            for k in range(ATT_UNROLL):
                scores(t + k + 1, bufs[(k + 1) % 2])
                consume(t + k, bufs[k % 2], masked)
            return carry

        lax.fori_loop(0, count // ATT_UNROLL, trip, 0)
        done = count - count % ATT_UNROLL
        for k in range(done, count):
            if k + 1 < count:
                scores(base + k + 1, bufs[(k + 1) % 2])
            consume(base + k, bufs[k % 2], masked)

    tri_ref[...] = jnp.where(lax.broadcasted_iota(jnp.int32, (tk, tq), 0)
                             <= lax.broadcasted_iota(jnp.int32, (tk, tq), 1), 0.0, -jnp.inf)
    m_ref[...] = jnp.full(m_ref.shape, -jnp.inf, F32)
    acc_ref[...] = jnp.zeros(acc_ref.shape, F32)
    if n_below:
        run(0, n_below, False)
    run(n_below + 1, n_diag, True)

    eye = (lax.broadcasted_iota(jnp.int32, (tq, tq), 0)
           == lax.broadcasted_iota(jnp.int32, (tq, tq), 1)).astype(BF16)
    for i in range(q_ref.shape[0] // tq):
        even = acc_ref[i, 0]
        odd = acc_ref[i, 1]
        merged = jnp.concatenate([even[:HEAD_DIM] / even[HEAD_DIM:HEAD_DIM + 1, :],
                                  odd[V_ROWS - HEAD_DIM:] / odd[0:1, :]], axis=0)
        o_ref[i * tq:(i + 1) * tq, :] = lax.dot_general(
            eye, merged.astype(BF16), (((1,), (1,)), ((), ())),
            preferred_element_type=F32).astype(BF16)


def _attention(p, vt, batch, seq):
    n = p.shape[0]
    pairs = ATT_HEADS // 2
    assert TQ == TK
    nq = seq // TQ
    below = [(i, j) for i in range(nq) for j in range(i)]
    diag = [(i, i) for i in range(nq)]
    tiles = below + below[-1:] + diag + diag[-1:] if below else [(0, 0)] + diag + diag[-1:]
    ti = jnp.asarray([t[0] for t in tiles], jnp.int32)
    tj = jnp.asarray([t[1] for t in tiles], jnp.int32)
    pw = 2 * HEAD_PAD
    grid_spec = pltpu.PrefetchScalarGridSpec(
        num_scalar_prefetch=2,
        grid=(batch, pairs),
        in_specs=[
            pl.BlockSpec((seq, pw), lambda b, h, ti, tj: (b, P_Q // pw + h)),
            pl.BlockSpec((seq, pw), lambda b, h, ti, tj: (b, P_K // pw + h)),
            pl.BlockSpec((seq // TK, 2 * V_ROWS, TK), lambda b, h, ti, tj: (b, h, 0)),
        ],
        out_specs=pl.BlockSpec((seq, LANES), lambda b, h, ti, tj: (b, h)),
        scratch_shapes=[pltpu.VMEM((2, TK, TQ), F32), pltpu.VMEM((2, TK, TQ), F32),
                        pltpu.VMEM((nq, 2, 1, TQ), F32), pltpu.VMEM((nq, 2, V_ROWS, TQ), F32),
                        pltpu.VMEM((TK, TQ), F32)],
    )
    return pl.pallas_call(
        functools.partial(_attn_kernel, n_below=len(below), n_diag=len(diag)),
        grid_spec=grid_spec,
        out_shape=jax.ShapeDtypeStruct((n, ATT_WIDTH), BF16),
        compiler_params=_params(40, 2),
        name="fox_attention",
    )(ti, tj, p, p, vt)


def _merge_kernel(x_ref, hc_ref, ha_ref, hs_ref, g0_ref, g1_ref, g2_ref, bg_ref,
                  wc_ref, wa_ref, ws_ref, wm_ref, o_ref):
    merged = None
    for k, (h_ref, w_ref, gl_ref) in enumerate(((hc_ref, wc_ref, g0_ref), (ha_ref, wa_ref, g1_ref),
                                                (hs_ref, ws_ref, g2_ref))):
        y = jnp.dot(h_ref[...], w_ref[...], preferred_element_type=F32)
        gate = _sigmoid(gl_ref[...].astype(F32) + bg_ref[:, k * D_MODEL:(k + 1) * D_MODEL])
        merged = gate * y if merged is None else merged + gate * y
    o_ref[...] = x_ref[...] + jnp.dot(merged.astype(BF16), wm_ref[...], preferred_element_type=F32)


def _merge(x, hc, ha, hs, p, bg, wc, wa, ws, wm, layer):
    n = x.shape[0]
    tm = TM_BRANCH
    g0 = P_GATE // D_MODEL
    half = lambda: pl.BlockSpec((tm, CONV_CH), lambda i: (i, 0))
    wspec = lambda k: pl.BlockSpec((None, k, D_MODEL), lambda i: (layer, 0, 0))
    return pl.pallas_call(
        _merge_kernel,
        grid=(n // tm,),
        in_specs=[
            pl.BlockSpec((tm, D_MODEL), lambda i: (i, 0)),
            half(), half(), half(),
            pl.BlockSpec((tm, D_MODEL), lambda i: (i, g0)),
            pl.BlockSpec((tm, D_MODEL), lambda i: (i, g0 + 1)),
            pl.BlockSpec((tm, D_MODEL), lambda i: (i, g0 + 2)),
            pl.BlockSpec((1, N_BRANCH * D_MODEL), lambda i: (0, 0)),
            wspec(CONV_CH), wspec(ATT_WIDTH), wspec(SG_WIDTH), wspec(D_MODEL),
        ],
        out_specs=pl.BlockSpec((tm, D_MODEL), lambda i: (i, 0)),
        out_shape=jax.ShapeDtypeStruct((n, D_MODEL), F32),
        compiler_params=_params(48, 1),
        name="merge",
    )(x, hc, ha, hs, p, p, p, bg, wc, wa, ws, wm)


def _ffn_kernel(x_ref, g_ref, w1_ref, w3_ref, w2_ref, o_ref, hn_ref, wb1_ref, wb3_ref, wb2_ref):
    f = pl.program_id(1)

    @pl.when(pl.program_id(0) == 0)
    def _():
        wb1_ref[f] = w1_ref[...].astype(BF16)
        wb3_ref[f] = w3_ref[...].astype(BF16)
        wb2_ref[f] = w2_ref[...].astype(BF16)

    @pl.when(f == 0)
    def _():
        hn_ref[...] = _rms(x_ref[...], g_ref[...]).astype(BF16)
        o_ref[...] = x_ref[...]

    hn = hn_ref[...]
    a = jnp.dot(hn, wb1_ref[f], preferred_element_type=F32)
    b = jnp.dot(hn, wb3_ref[f], preferred_element_type=F32)
    t = (a * _sigmoid(a) * b).astype(BF16)
    o_ref[...] += jnp.dot(t, wb2_ref[f], preferred_element_type=F32)


def _ffn(x, g, w1, w3, w2, layer):
    n = x.shape[0]
    tm, tf = TM_FFN, TF_FFN
    nf = D_FF // tf

    def f_idx(i, f):
        return jnp.where(i == 0, f, nf - 1)

    return pl.pallas_call(
        _ffn_kernel,
        grid=(n // tm, nf),
        in_specs=[
            pl.BlockSpec((tm, D_MODEL), lambda i, f: (i, 0)),
            pl.BlockSpec((1, D_MODEL), lambda i, f: (0, 0)),
            pl.BlockSpec((None, D_MODEL, tf), lambda i, f: (layer, 0, f_idx(i, f))),
            pl.BlockSpec((None, D_MODEL, tf), lambda i, f: (layer, 0, f_idx(i, f))),
            pl.BlockSpec((None, tf, D_MODEL), lambda i, f: (layer, f_idx(i, f), 0)),
        ],
        out_specs=pl.BlockSpec((tm, D_MODEL), lambda i, f: (i, 0)),
        out_shape=jax.ShapeDtypeStruct((n, D_MODEL), F32),
        scratch_shapes=[pltpu.VMEM((tm, D_MODEL), BF16),
                        pltpu.VMEM((nf, D_MODEL, tf), BF16), pltpu.VMEM((nf, D_MODEL, tf), BF16),
                        pltpu.VMEM((nf, tf, D_MODEL), BF16)],
        compiler_params=_params(56, 2),
        name="ffn_dense",
    )(x, g, w1, w3, w2)


def _router_kernel(x_ref, g_ref, wr_ref, lt_ref, h_ref, meta_ref, cnt_ref, carry_ref):
    i = pl.program_id(0)
    tm = x_ref.shape[0]

    @pl.when(i == 0)
    def _():
        carry_ref[...] = jnp.zeros_like(carry_ref)

    h = _rms(x_ref[...], g_ref[...])
    h_ref[...] = h
    h_hi = h.astype(BF16)
    h_lo = (h - h_hi.astype(F32)).astype(BF16)
    w = wr_ref[...]
    w_hi = w.astype(BF16)
    w_lo = (w - w_hi.astype(F32)).astype(BF16)
    logits = (jnp.dot(h_hi, w_hi, preferred_element_type=F32)
              + jnp.dot(h_hi, w_lo, preferred_element_type=F32)
              + jnp.dot(h_lo, w_hi, preferred_element_type=F32))
    lane = lax.broadcasted_iota(jnp.int32, (tm, LANES), 1)
    lanef = lane.astype(F32)
    lg = jnp.where(lane < N_EXPERTS, logits, -jnp.inf)
    v1 = jnp.max(lg, axis=1, keepdims=True)
    i1 = jnp.min(jnp.where(lg == v1, lanef, float(LANES)), axis=1, keepdims=True)
    lg2 = jnp.where(lanef == i1, -jnp.inf, lg)
    v2 = jnp.max(lg2, axis=1, keepdims=True)
    i2 = jnp.min(jnp.where(lg2 == v2, lanef, float(LANES)), axis=1, keepdims=True)
    e = jnp.exp(v2 - v1)
    g1 = 1.0 / (1.0 + e)
    g2 = e / (1.0 + e)
    oh1 = lanef == i1
    oh2 = lanef == i2
    cnt = (oh1.astype(F32) + oh2.astype(F32))
    before = jnp.dot(lt_ref[...], cnt.astype(BF16), preferred_element_type=F32) + carry_ref[...]
    r1 = jnp.sum(jnp.where(oh1, before, 0.0), axis=1, keepdims=True)
    r2 = jnp.sum(jnp.where(oh2, before, 0.0), axis=1, keepdims=True)
    total = carry_ref[...] + jnp.sum(cnt, axis=0, keepdims=True)
    carry_ref[...] = total
    cnt_ref[...] = jnp.broadcast_to(total, cnt_ref.shape)
    meta = jnp.zeros((tm, LANES), F32)
    for k, val in enumerate((i1, i2, g1, g2, r1, r2)):
        meta = jnp.where(lane == k, val, meta)
    meta_ref[...] = meta


def _router(x, g, wr, lt):
    n = x.shape[0]
    tm = TM_BRANCH
    return pl.pallas_call(
        _router_kernel,
        grid=(n // tm,),
        in_specs=[
            pl.BlockSpec((tm, D_MODEL), lambda i: (i, 0)),
            pl.BlockSpec((1, D_MODEL), lambda i: (0, 0)),
            pl.BlockSpec((D_MODEL, LANES), lambda i: (0, 0)),
            pl.BlockSpec((tm, tm), lambda i: (0, 0)),
        ],
        out_specs=[
            pl.BlockSpec((tm, D_MODEL), lambda i: (i, 0)),
            pl.BlockSpec((tm, LANES), lambda i: (i, 0)),
            pl.BlockSpec((8, LANES), lambda i: (0, 0)),
        ],
        out_shape=[
            jax.ShapeDtypeStruct((n, D_MODEL), F32),
            jax.ShapeDtypeStruct((n, LANES), F32),
            jax.ShapeDtypeStruct((8, LANES), F32),
        ],
        scratch_shapes=[pltpu.VMEM((1, LANES), F32)],
        compiler_params=_params(32, 1),
        name="router",
    )(x, g, wr, lt)


def _row_copy(src_ref, src_row, dst_ref, dst_row, sem):
    return pltpu.make_async_copy(src_ref.at[pl.ds(src_row, 1), :], dst_ref.at[pl.ds(dst_row, 1), :], sem)


def _dispatch_kernel(bounds_ref, dest_ref, h_ref, xs_ref, zero_ref, sem, zero_sem):
    tm = h_ref.shape[0]
    n_blk = xs_ref.shape[0] // MOE_BLK

    @pl.when(pl.program_id(0) == 0)
    def _():
        zero_ref[...] = jnp.zeros_like(zero_ref)

        def zero_block(blk):
            start = pl.multiple_of(blk * MOE_BLK, MOE_BLK)
            copy = pltpu.make_async_copy(zero_ref, xs_ref.at[pl.ds(start, MOE_BLK), :], zero_sem)
            copy.start()
            copy.wait()

        for e in range(N_EXPERTS):
            @pl.when(bounds_ref[e + 1] > bounds_ref[e])
            def _():
                zero_block(bounds_ref[e + 1] // MOE_BLK - 1)

        def unused(blk, carry):
            zero_block(blk)
            return carry

        lax.fori_loop(bounds_ref[N_EXPERTS] // MOE_BLK, n_blk, unused, 0)

    def issue(r, carry):
        for k in range(2):
            _row_copy(h_ref, r, xs_ref, dest_ref[2 * r + k], sem).start(priority=k)
        return carry

    lax.fori_loop(0, tm, issue, 0, unroll=ROW_ISSUE_UNROLL)
    for _ in range(2):
        pltpu.make_async_copy(h_ref, xs_ref.at[pl.ds(0, tm), :], sem).wait()


def _dispatch(bounds, dest_flat, h, n_rows):
    n = h.shape[0]
    tm = TM_ROW
    return pl.pallas_call(
        _dispatch_kernel,
        grid=(n // tm,),
        in_specs=[
            pl.BlockSpec(memory_space=pltpu.SMEM),
            pl.BlockSpec((2 * tm,), lambda i: (i,), memory_space=pltpu.SMEM),
            pl.BlockSpec((tm, D_MODEL), lambda i: (i, 0)),
        ],
        out_specs=pl.BlockSpec(memory_space=pl.ANY),
        out_shape=jax.ShapeDtypeStruct((n_rows, D_MODEL), F32),
        scratch_shapes=[pltpu.VMEM((MOE_BLK, D_MODEL), F32), pltpu.SemaphoreType.DMA(()),
                        pltpu.SemaphoreType.DMA(())],
        compiler_params=_params(32, 1),
        name="moe_dispatch",
    )(bounds, dest_flat, h)


def _first_block_of_expert(b, be_ref):
    return jnp.logical_or(b == 0, be_ref[b] != be_ref[jnp.maximum(b - 1, 0)])


def _moe_kernel(be_ref, nu_ref, x_ref, w1_ref, w3_ref, w2_ref, y_ref, xb_ref, wb1_ref, wb3_ref, wb2_ref):
    b = pl.program_id(0)
    f = pl.program_id(1)

    @pl.when(b < nu_ref[0])
    def _():
        @pl.when(_first_block_of_expert(b, be_ref))
        def _():
            wb1_ref[f] = w1_ref[...].astype(BF16)
            wb3_ref[f] = w3_ref[...].astype(BF16)
            wb2_ref[f] = w2_ref[...].astype(BF16)

        @pl.when(f == 0)
        def _():
            xb_ref[...] = x_ref[...].astype(BF16)
            y_ref[...] = jnp.zeros_like(y_ref)

        xb = xb_ref[...]
        a = jnp.dot(xb, wb1_ref[f], preferred_element_type=F32)
        c = jnp.dot(xb, wb3_ref[f], preferred_element_type=F32)
        t = (a * _sigmoid(a) * c).astype(BF16)
        y_ref[...] += jnp.dot(t, wb2_ref[f], preferred_element_type=F32)

    @pl.when(jnp.logical_and(b >= nu_ref[0], f == pl.num_programs(1) - 1))
    def _():
        y_ref[...] = jnp.zeros_like(y_ref)


def _moe_experts(blk_e, n_used, xs, w1, w3, w2, layer):
    rows = xs.shape[0]
    n_blk = rows // MOE_BLK
    nf = D_FF_EXPERT // TF_MOE

    def row_idx(b, f, be, nu):
        return (jnp.minimum(b, nu[0] - 1), 0)

    def f_idx(b, f, be, nu):
        fetch = jnp.logical_and(b < nu[0], _first_block_of_expert(b, be))
        return jnp.where(fetch, f, nf - 1)

    grid_spec = pltpu.PrefetchScalarGridSpec(
        num_scalar_prefetch=2,
        grid=(n_blk, nf),
        in_specs=[
            pl.BlockSpec((MOE_BLK, D_MODEL), row_idx),
            pl.BlockSpec((None, None, D_MODEL, TF_MOE),
                         lambda b, f, be, nu: (layer, be[b], 0, f_idx(b, f, be, nu))),
            pl.BlockSpec((None, None, D_MODEL, TF_MOE),
                         lambda b, f, be, nu: (layer, be[b], 0, f_idx(b, f, be, nu))),
            pl.BlockSpec((None, None, TF_MOE, D_MODEL),
                         lambda b, f, be, nu: (layer, be[b], f_idx(b, f, be, nu), 0)),
        ],
        out_specs=pl.BlockSpec((MOE_BLK, D_MODEL), lambda b, f, be, nu: (b, 0)),
        scratch_shapes=[pltpu.VMEM((MOE_BLK, D_MODEL), BF16),
                        pltpu.VMEM((nf, D_MODEL, TF_MOE), BF16), pltpu.VMEM((nf, D_MODEL, TF_MOE), BF16),
                        pltpu.VMEM((nf, TF_MOE, D_MODEL), BF16)],
    )
    return pl.pallas_call(
        _moe_kernel,
        grid_spec=grid_spec,
        out_shape=jax.ShapeDtypeStruct((rows, D_MODEL), F32),
        compiler_params=_params(56, 2),
        name="moe_experts",
    )(blk_e, n_used, xs, w1, w3, w2)


def _combine_kernel(dest_ref, x_ref, meta_ref, gfin_ref, y_ref, o_ref, ybuf_ref, sem, *, final_norm):
    tm = x_ref.shape[0]

    def issue(r, carry):
        for k in range(2):
            _row_copy(y_ref, dest_ref[2 * r + k], ybuf_ref.at[k], r, sem).start(priority=k)
        return carry

    lax.fori_loop(0, tm, issue, 0, unroll=ROW_ISSUE_UNROLL)
    for k in range(2):
        pltpu.make_async_copy(y_ref.at[pl.ds(0, tm), :], ybuf_ref.at[k], sem).wait()

    g1 = meta_ref[:, 2:3]
    g2 = meta_ref[:, 3:4]
    out = x_ref[...] + (g1 * ybuf_ref[0] + g2 * ybuf_ref[1])
    if final_norm:
        out = _rms(out, gfin_ref[...])
    o_ref[...] = out


def _combine(dest_flat, x, meta, gfin, y, final_norm):
    n = x.shape[0]
    tm = TM_ROW
    kern = functools.partial(_combine_kernel, final_norm=final_norm)
    return pl.pallas_call(
        kern,
        grid=(n // tm,),
        in_specs=[
            pl.BlockSpec((2 * tm,), lambda i: (i,), memory_space=pltpu.SMEM),
            pl.BlockSpec((tm, D_MODEL), lambda i: (i, 0)),
            pl.BlockSpec((tm, LANES), lambda i: (i, 0)),
            pl.BlockSpec((1, D_MODEL), lambda i: (0, 0)),
            pl.BlockSpec(memory_space=pl.ANY),
        ],
        out_specs=pl.BlockSpec((tm, D_MODEL), lambda i: (i, 0)),
        out_shape=jax.ShapeDtypeStruct((n, D_MODEL), F32),
        scratch_shapes=[pltpu.VMEM((2, tm, D_MODEL), F32), pltpu.SemaphoreType.DMA(())],
        compiler_params=_params(32, 1),
        name="moe_combine",
    )(dest_flat, x, meta, gfin, y)


def _moe_layer(x, g, wr, lt, w1, w3, w2, layer, gfin, final_norm):
    n = x.shape[0]
    h, meta, cnt = _router(x, g, wr, lt)
    expert = meta[:, 0:2].astype(jnp.int32)
    rank = meta[:, 4:6].astype(jnp.int32)
    counts = cnt[0, :N_EXPERTS].astype(jnp.int32)
    padded = (counts + MOE_BLK - 1) // MOE_BLK * MOE_BLK
    pad_end = jnp.cumsum(padded)
    pad_start = pad_end - padded
    dest = (pad_start[expert] + rank).reshape(-1)
    n_blk = (2 * n) // MOE_BLK + N_EXPERTS
    blk_start = jnp.arange(n_blk, dtype=jnp.int32) * MOE_BLK
    blk_e = jnp.minimum(jnp.sum(blk_start[:, None] >= pad_end[None, :], axis=1), N_EXPERTS - 1)
    n_used = (pad_end[-1:] // MOE_BLK).astype(jnp.int32)
    bounds = jnp.concatenate([jnp.zeros((1,), jnp.int32), pad_end.astype(jnp.int32)])
    xs = _dispatch(bounds, dest, h, n_blk * MOE_BLK)
    y = _moe_experts(blk_e.astype(jnp.int32), n_used, xs, w1, w3, w2, layer)
    return _combine(dest, x, meta, gfin, y, final_norm)


def _prepare_mixer(w_in, b_forget, conv_w, w_spatial, b_spatial, w_conv_out, w_att_out, w_sg_out,
                   w_mix_out):
    depth, d, _ = w_in.shape

    def pad_heads(w):
        w = w.reshape(depth, d, ATT_HEADS, HEAD_DIM)
        w = jnp.pad(w, ((0, 0), (0, 0), (0, 0), (0, HEAD_PAD - HEAD_DIM)))
        return w.reshape(depth, d, ATT_HEADS * HEAD_PAD)

    prep = {}
    prep["w_main"] = jnp.concatenate(
        [pad_heads(w_in[:, :, OFF_K:OFF_V]), w_in[:, :, OFF_CONV:OFF_Q], w_in[:, :, OFF_SG:OFF_GATE],
         w_in[:, :, OFF_GATE:], pad_heads(w_in[:, :, OFF_Q:OFF_K])], axis=2).astype(BF16)
    col = jnp.arange(P_COLS)
    is_q = col >= P_Q
    spare = (col % HEAD_PAD >= HEAD_DIM) & (col % HEAD_PAD < HEAD_DIM + N_SPLIT)
    prep["col_scale"] = jnp.where(is_q, LOG2E * HEAD_DIM ** -0.5, 1.0).astype(F32)[None, :]
    prep["col_bias"] = jnp.where(is_q & spare, 1.0, 0.0).astype(F32)[None, :]
    kcol = jnp.arange(ATT_HEADS * HEAD_PAD)
    place = -((kcol[None, None, :] // HEAD_PAD == jnp.arange(LANES)[None, :, None])
              & (kcol[None, None, :] % HEAD_PAD == HEAD_DIM + jnp.arange(N_SPLIT)[:, None, None])
              ).astype(BF16)
    prep["place"] = place.reshape(N_SPLIT * LANES, ATT_HEADS * HEAD_PAD)
    prep["wf"] = jnp.pad(w_in[:, :, OFF_F:OFF_SG], ((0, 0), (0, 0), (0, LANES - ATT_HEADS))).astype(BF16)
    prep["bf"] = jnp.pad(b_forget, ((0, 0), (0, LANES - ATT_HEADS)))[:, None, :]
    wv_t = jnp.swapaxes(w_in[:, :, OFF_V:OFF_F], 1, 2).reshape(depth, ATT_HEADS // 2, 2, HEAD_DIM, d)
    zeros_h = jnp.zeros_like(wv_t[:, :, 0, :V_ROWS - HEAD_DIM])
    wvt = jnp.concatenate([wv_t[:, :, 0], zeros_h, zeros_h, wv_t[:, :, 1]], axis=2)
    prep["wvt"] = wvt.reshape(depth, ATT_HEADS * V_ROWS, d).astype(BF16)
    vrow = jnp.arange(ATT_HEADS * V_ROWS) % (2 * V_ROWS)
    prep["vbias"] = ((vrow == HEAD_DIM) | (vrow == V_ROWS)).astype(F32)[:, None]
    prep["lt_incl"] = jnp.tril(jnp.ones((LANES, LANES), F32)).astype(BF16)
    prep["conv_w"] = jnp.pad(conv_w, ((0, 0), (0, 1), (0, 0)))
    causal = jnp.tril(jnp.ones((SG_CHUNK, SG_CHUNK), bool))
    prep["w_sp"] = jnp.where(causal[None, None], w_spatial, 0).astype(BF16)
    prep["sg_bias"] = jnp.repeat(jnp.swapaxes(b_spatial, 1, 2), SG_WIDTH // SG_GROUPS, axis=2)
    for name, w in (("wc", w_conv_out), ("wa", w_att_out), ("ws", w_sg_out), ("wm", w_mix_out)):
        prep[name] = w.astype(BF16)
    return prep


def _token_mixer(xf, prep, vecs, layer, batch, seq):
    row = lambda name: vecs[name][layer][None, :]
    p, vt = _inproj(xf, row("mix_norm_g"), prep["w_main"], prep["col_scale"], prep["col_bias"],
                    prep["wf"], prep["bf"][layer], prep["lt_incl"], prep["place"],
                    prep["wvt"], prep["vbias"], seq, layer)
    hc = _conv_branch(p, prep["conv_w"][layer], row("conv_b"), row("conv_ln_g"), row("conv_ln_b"), seq)
    hs = _sg_branch(p, row("sg_ln_g"), row("sg_ln_b"), prep["w_sp"][layer], prep["sg_bias"][layer])
    ha = _attention(p, vt, batch, seq)
    return _merge(xf, hc, ha, hs, p, row("b_gate"), prep["wc"], prep["wa"], prep["ws"], prep["wm"], layer)


def kernel(x, mix_norm_g, w_in, b_forget, b_gate, conv_w, conv_b, conv_ln_g, conv_ln_b, w_conv_out,
           w_att_out, sg_ln_g, sg_ln_b, w_spatial, b_spatial, w_sg_out, w_mix_out, ffn_norm_g,
           ffn_w1, ffn_w3, ffn_w2, router_w, moe_w1, moe_w3, moe_w2, final_norm_g):
    batch, seq, d = x.shape
    depth = w_in.shape[0]
    if depth % 2 == 1:
        raise NotImplementedError("the final norm is fused into the last (expert) layer")
    xf = x.reshape(batch * seq, d)

    prep = _prepare_mixer(w_in, b_forget, conv_w, w_spatial, b_spatial, w_conv_out, w_att_out,
                          w_sg_out, w_mix_out)
    vecs = dict(mix_norm_g=mix_norm_g, conv_b=conv_b, conv_ln_g=conv_ln_g, conv_ln_b=conv_ln_b,
                sg_ln_g=sg_ln_g, sg_ln_b=sg_ln_b, b_gate=b_gate)
    f1, f3, f2 = ffn_w1, ffn_w3, ffn_w2
    m1, m3, m2 = moe_w1, moe_w3, moe_w2
    wr = jnp.pad(router_w, ((0, 0), (0, 0), (0, LANES - N_EXPERTS)))
    lt = jnp.tril(jnp.ones((TM_BRANCH, TM_BRANCH), F32), -1).astype(BF16)
    gfin = final_norm_g[None, :]

    for layer in range(depth):
        xf = _token_mixer(xf, prep, vecs, layer, batch, seq)
        g_ffn = ffn_norm_g[layer][None, :]
        i = layer // 2
        if layer % 2 == 0:
            xf = _ffn(xf, g_ffn, f1, f3, f2, i)
        else:
            xf = _moe_layer(xf, g_ffn, wr[i], lt, m1, m3, m2, i, gfin,
                            final_norm=(layer == depth - 1))
    return xf.reshape(batch, seq, d)
```

```python
import functools

import jax
import jax.numpy as jnp
from jax import lax
from jax.experimental import pallas as pl
from jax.experimental.pallas import tpu as pltpu

F32 = jnp.float32
BF16 = jnp.bfloat16

D_MODEL = 1024
CONV_CH = 512
CONV_K = 31
ATT_HEADS = 8
HEAD_DIM = 64
ATT_WIDTH = ATT_HEADS * HEAD_DIM
SG_GROUPS = 8
SG_WIDTH = 512
SG_CHUNK = 128
N_BRANCH = 3
OFF_CONV = 0
OFF_Q = OFF_CONV + 2 * CONV_CH
OFF_K = OFF_Q + ATT_WIDTH
OFF_V = OFF_K + ATT_WIDTH
OFF_F = OFF_V + ATT_WIDTH
OFF_SG = OFF_F + ATT_HEADS
OFF_GATE = OFF_SG + 2 * SG_WIDTH
D_FF = 2816
N_EXPERTS = 8
D_FF_EXPERT = 3584
EPS = 1e-6

LANES = 128
SUBLANES = 8
MIB = 1024 * 1024

HEAD_PAD = LANES
P_K = 0
P_CONV = P_K + ATT_HEADS * HEAD_PAD
P_SG = P_CONV + 2 * CONV_CH
P_GATE = P_SG + 2 * SG_WIDTH
P_Q = P_GATE + N_BRANCH * D_MODEL
P_COLS = P_Q + ATT_HEADS * HEAD_PAD
N_SPLIT = 3
V_ROWS = HEAD_PAD
LOG2E = 1.4426950408889634

TM_PROJ = 1024
TN_PROJ = 1024
TM_BRANCH = 512
CONV_HIST = 32
CONV_RB = 64
TQ = 512
TK = 512
ATT_UNROLL = 14
TM_FFN = 1024
TF_FFN = 256
MOE_BLK = 512
TF_MOE = 512
TM_ROW = 1024
ROW_ISSUE_UNROLL = 32


def _params(vmem_mib, n_axes):
    return pltpu.CompilerParams(dimension_semantics=("arbitrary",) * n_axes,
                                vmem_limit_bytes=vmem_mib * MIB)


def _rms(x, g):
    return x * lax.rsqrt(jnp.mean(x * x, axis=-1, keepdims=True) + EPS) * g


def _layer_norm(x, g, b):
    mu = jnp.mean(x, axis=-1, keepdims=True)
    d = x - mu
    var = jnp.mean(d * d, axis=-1, keepdims=True)
    return d * lax.rsqrt(var + EPS) * g + b


def _sigmoid(x):
    return 1.0 / (1.0 + jnp.exp(-x))


def _split3(x):
    hi = x.astype(BF16)
    r1 = x - hi.astype(F32)
    mid = r1.astype(BF16)
    lo = (r1 - mid.astype(F32)).astype(BF16)
    return hi, mid, lo


def _inproj_kernel(x_ref, g_ref, w_ref, sc_ref, bi_ref, wf_ref, bf_ref, lt_ref, pl_ref, wvt_ref,
                   vb_ref, p_ref, vt_ref, xn_ref, c3_ref, carry_ref, wres_ref, *, tiles_per_batch):
    i = pl.program_id(0)
    j = pl.program_id(1)
    tm = x_ref.shape[0]
    tk = vt_ref.shape[-1]

    @pl.when(i == 0)
    def _():
        wres_ref[j] = w_ref[...]

    @pl.when(jnp.logical_and(j == 0, i % tiles_per_batch == 0))
    def _():
        carry_ref[...] = jnp.zeros_like(carry_ref)

    @pl.when(j == 0)
    def _():
        xn = _rms(x_ref[...], g_ref[...]).astype(BF16)
        xn_ref[...] = xn
        f = jnp.dot(xn, wf_ref[...], preferred_element_type=F32) + bf_ref[...]
        ls = jnp.minimum(f, 0.0) - jnp.log1p(jnp.exp(-jnp.abs(f)))
        lt = lt_ref[...]
        terms = _split3(ls)
        offset = carry_ref[...]
        for blk in range(tm // LANES):
            rows = slice(blk * LANES, (blk + 1) * LANES)
            c = sum(jnp.dot(lt, term[rows], preferred_element_type=F32) for term in terms) + offset
            offset = c[LANES - 1:LANES, :]
            for r, term in enumerate(_split3(c * LOG2E)):
                c3_ref[rows, r * LANES:(r + 1) * LANES] = term
        carry_ref[...] = offset
        vt = lax.dot_general(wvt_ref[...], xn, (((1,), (1,)), ((), ())),
                             preferred_element_type=F32) + vb_ref[...]
        for ch in range(tm // tk):
            vt_ref[ch] = vt[:, ch * tk:(ch + 1) * tk].astype(BF16)
        decay = jnp.dot(c3_ref[...], pl_ref[...], preferred_element_type=F32)
        p_ref[...] = (jnp.dot(xn, wres_ref[0], preferred_element_type=F32) + decay).astype(BF16)

    @pl.when(j > 0)
    def _():
        acc = jnp.dot(xn_ref[...], wres_ref[j], preferred_element_type=F32)
        p_ref[...] = (acc * sc_ref[...] + bi_ref[...]).astype(BF16)


def _inproj(x, g, w, col_scale, col_bias, wf, bf, lt, place, wvt, vbias, seq, layer):
    n = x.shape[0]
    tm, tn = TM_PROJ, TN_PROJ
    assert P_K == 0 and tn == ATT_HEADS * HEAD_PAD
    kern = functools.partial(_inproj_kernel, tiles_per_batch=seq // tm)
    vt_rows = ATT_HEADS * V_ROWS
    n_col = P_COLS // tn
    const = lambda shape: pl.BlockSpec(shape, lambda i, j: (0,) * len(shape))
    return pl.pallas_call(
        kern,
        grid=(n // tm, n_col),
        in_specs=[
            pl.BlockSpec((tm, D_MODEL), lambda i, j: (i, 0)),
            const((1, D_MODEL)),
            pl.BlockSpec((None, D_MODEL, tn), lambda i, j: (layer, 0, jnp.where(i == 0, j, n_col - 1))),
            pl.BlockSpec((1, tn), lambda i, j: (0, j)),
            pl.BlockSpec((1, tn), lambda i, j: (0, j)),
            pl.BlockSpec((None, D_MODEL, LANES), lambda i, j: (layer, 0, 0)),
            const((1, LANES)),
            const((LANES, LANES)),
            const((N_SPLIT * LANES, tn)),
            pl.BlockSpec((None, vt_rows, D_MODEL), lambda i, j: (layer, 0, 0)),
            const((vt_rows, 1)),
        ],
        out_specs=[
            pl.BlockSpec((tm, tn), lambda i, j: (i, j)),
            pl.BlockSpec((tm // TK, vt_rows, TK), lambda i, j: (i, 0, 0)),
        ],
        out_shape=[
            jax.ShapeDtypeStruct((n, P_COLS), BF16),
            jax.ShapeDtypeStruct((n // TK, vt_rows, TK), BF16),
        ],
        scratch_shapes=[pltpu.VMEM((tm, D_MODEL), BF16), pltpu.VMEM((tm, N_SPLIT * LANES), BF16),
                        pltpu.VMEM((1, LANES), F32), pltpu.VMEM((n_col, D_MODEL, tn), BF16)],
        compiler_params=_params(56, 2),
        name="inproj",
    )(x, g, w, col_scale, col_bias, wf, bf, lt, place, wvt, vbias)


def _conv_kernel(a1_ref, a2_ref, w_ref, cb_ref, g_ref, b_ref, o_ref, hext_ref, hsh_ref,
                 *, tiles_per_batch):
    i = pl.program_id(0)
    tm = a1_ref.shape[0]
    span = hsh_ref.shape[1]

    @pl.when(i % tiles_per_batch == 0)
    def _():
        hext_ref[0:CONV_HIST, :] = jnp.zeros((CONV_HIST, CONV_CH), F32)

    @pl.when(i % tiles_per_batch != 0)
    def _():
        hext_ref[0:CONV_HIST, :] = hext_ref[tm:tm + CONV_HIST, :]

    hext_ref[CONV_HIST:CONV_HIST + tm, :] = (
        a1_ref[...].astype(F32) * _sigmoid(a2_ref[...].astype(F32)))
    for s in range(1, SUBLANES):
        hsh_ref[s] = hext_ref[s:s + span, :]

    g = g_ref[...]
    b = b_ref[...]
    base = CONV_HIST - (CONV_K - 1)
    for r in range(0, tm, CONV_RB):
        acc = jnp.broadcast_to(cb_ref[...], (CONV_RB, CONV_CH))
        for j in range(CONV_K):
            s = (base + j) % SUBLANES
            a = r + base + j - s
            rows = hext_ref[a:a + CONV_RB, :] if s == 0 else hsh_ref[s, a:a + CONV_RB, :]
            acc = acc + w_ref[j:j + 1, :] * rows
        y = _layer_norm(acc, g, b)
        o_ref[r:r + CONV_RB, :] = (y * _sigmoid(y)).astype(BF16)


def _conv_branch(p, w, cb, g, b, seq):
    n = p.shape[0]
    tm = TM_BRANCH
    kern = functools.partial(_conv_kernel, tiles_per_batch=seq // tm)
    c0 = P_CONV // CONV_CH
    vec = pl.BlockSpec((1, CONV_CH), lambda i: (0, 0))
    return pl.pallas_call(
        kern,
        grid=(n // tm,),
        in_specs=[
            pl.BlockSpec((tm, CONV_CH), lambda i: (i, c0)),
            pl.BlockSpec((tm, CONV_CH), lambda i: (i, c0 + 1)),
            pl.BlockSpec((CONV_K + 1, CONV_CH), lambda i: (0, 0)),
            vec, vec, vec,
        ],
        out_specs=pl.BlockSpec((tm, CONV_CH), lambda i: (i, 0)),
        out_shape=jax.ShapeDtypeStruct((n, CONV_CH), BF16),
        scratch_shapes=[pltpu.VMEM((tm + CONV_HIST, CONV_CH), F32),
                        pltpu.VMEM((SUBLANES, tm + CONV_HIST - SUBLANES, CONV_CH), F32)],
        compiler_params=_params(32, 1),
        name="conv_branch",
    )(p, p, w, cb, g, b)


def _sg_kernel(u_ref, v_ref, g_ref, b_ref, w_ref, bias_ref, o_ref):
    tm = u_ref.shape[0]

    def gelu(z):
        return 0.5 * z * (1.0 + lax.erf(z * 0.7071067811865476))

    zu = gelu(u_ref[...].astype(F32))
    vn = _layer_norm(gelu(v_ref[...].astype(F32)), g_ref[...], b_ref[...]).astype(BF16)
    lane = lax.broadcasted_iota(jnp.int32, (SG_CHUNK, LANES), 1)
    first_group = lane < (SG_WIDTH // SG_GROUPS)
    for c in range(tm // SG_CHUNK):
        rows = slice(c * SG_CHUNK, (c + 1) * SG_CHUNK)
        for pr in range(SG_WIDTH // LANES):
            cols = slice(pr * LANES, (pr + 1) * LANES)
            vp = vn[rows, cols]
            m0 = jnp.dot(w_ref[2 * pr], vp, preferred_element_type=F32)
            m1 = jnp.dot(w_ref[2 * pr + 1], vp, preferred_element_type=F32)
            mixed = jnp.where(first_group, m0, m1) + bias_ref[:, cols]
            o_ref[rows, cols] = (zu[rows, cols] * mixed).astype(BF16)


def _sg_branch(p, g, b, w_tril, bias_full):
    n = p.shape[0]
    tm = TM_BRANCH
    c0 = P_SG // SG_WIDTH
    vec = pl.BlockSpec((1, SG_WIDTH), lambda i: (0, 0))
    return pl.pallas_call(
        _sg_kernel,
        grid=(n // tm,),
        in_specs=[
            pl.BlockSpec((tm, SG_WIDTH), lambda i: (i, c0)),
            pl.BlockSpec((tm, SG_WIDTH), lambda i: (i, c0 + 1)),
            vec, vec,
            pl.BlockSpec((SG_GROUPS, SG_CHUNK, SG_CHUNK), lambda i: (0, 0, 0)),
            pl.BlockSpec((SG_CHUNK, SG_WIDTH), lambda i: (0, 0)),
        ],
        out_specs=pl.BlockSpec((tm, SG_WIDTH), lambda i: (i, 0)),
        out_shape=jax.ShapeDtypeStruct((n, SG_WIDTH), BF16),
        compiler_params=_params(32, 1),
        name="sg_branch",
    )(p, p, g, b, w_tril, bias_full)


def _attn_kernel(ti_ref, tj_ref, q_ref, k_ref, vt_ref, o_ref, sa_ref, sb_ref, m_ref, acc_ref, tri_ref,
                 *, n_below, n_diag):
    tq = sa_ref.shape[-1]
    tk = vt_ref.shape[-1]
    head_cols = [slice(hh * HEAD_PAD, (hh + 1) * HEAD_PAD) for hh in range(2)]

    def scores(t, dst_ref):
        q0 = pl.multiple_of(ti_ref[t] * tq, tq)
        k0 = pl.multiple_of(tj_ref[t] * tk, tk)
        for hh, cols in enumerate(head_cols):
            dst_ref[hh] = lax.dot_general(k_ref[pl.ds(k0, tk), cols], q_ref[pl.ds(q0, tq), cols],
                                          (((1,), (1,)), ((), ())), preferred_element_type=F32)

    def consume(t, src_ref, masked):
        i = ti_ref[t]
        j = tj_ref[t]
        probs = []
        for hh in range(2):
            s = src_ref[hh]
            if masked:
                s = s + tri_ref[...]
            m = m_ref[i, hh]
            m_new = jnp.maximum(m, jnp.max(s, axis=0, keepdims=True))
            m_ref[i, hh] = m_new
            probs.append((jnp.exp2(m - m_new), jnp.exp2(s - m_new).astype(BF16)))
        for hh, (alpha, pexp) in enumerate(probs):
            pv = jnp.dot(vt_ref[j, hh * V_ROWS:(hh + 1) * V_ROWS, :], pexp, preferred_element_type=F32)
            acc_ref[i, hh] = alpha * acc_ref[i, hh] + pv

    def run(base, count, masked):
        bufs = (sa_ref, sb_ref)
        scores(base, bufs[0])

        def trip(u, carry):
            t = base + ATT_UNROLL * u
            for k in range(ATT_UNROLL):
                scores(t + k + 1, bufs[(k + 1) % 2])
                consume(t + k, bufs[k % 2], masked)
            return carry

        lax.fori_loop(0, count // ATT_UNROLL, trip, 0)
        done = count - count % ATT_UNROLL
        for k in range(done, count):
            if k + 1 < count:
                scores(base + k + 1, bufs[(k + 1) % 2])
            consume(base + k, bufs[k % 2], masked)

    tri_ref[...] = jnp.where(lax.broadcasted_iota(jnp.int32, (tk, tq), 0)
                             <= lax.broadcasted_iota(jnp.int32, (tk, tq), 1), 0.0, -jnp.inf)
    m_ref[...] = jnp.full(m_ref.shape, -jnp.inf, F32)
    acc_ref[...] = jnp.zeros(acc_ref.shape, F32)
    if n_below:
        run(0, n_below, False)
    run(n_below + 1, n_diag, True)

    eye = (lax.broadcasted_iota(jnp.int32, (tq, tq), 0)
           == lax.broadcasted_iota(jnp.int32, (tq, tq), 1)).astype(BF16)
    for i in range(q_ref.shape[0] // tq):
        even = acc_ref[i, 0]
        odd = acc_ref[i, 1]
        merged = jnp.concatenate([even[:HEAD_DIM] / even[HEAD_DIM:HEAD_DIM + 1, :],
                                  odd[V_ROWS - HEAD_DIM:] / odd[0:1, :]], axis=0)
        o_ref[i * tq:(i + 1) * tq, :] = lax.dot_general(
            eye, merged.astype(BF16), (((1,), (1,)), ((), ())),
            preferred_element_type=F32).astype(BF16)


def _attention(p, vt, batch, seq):
    n = p.shape[0]
    pairs = ATT_HEADS // 2
    assert TQ == TK
    nq = seq // TQ
    below = [(i, j) for i in range(nq) for j in range(i)]
    diag = [(i, i) for i in range(nq)]
    tiles = below + below[-1:] + diag + diag[-1:] if below else [(0, 0)] + diag + diag[-1:]
    ti = jnp.asarray([t[0] for t in tiles], jnp.int32)
    tj = jnp.asarray([t[1] for t in tiles], jnp.int32)
    pw = 2 * HEAD_PAD
    grid_spec = pltpu.PrefetchScalarGridSpec(
        num_scalar_prefetch=2,
        grid=(batch, pairs),
        in_specs=[
            pl.BlockSpec((seq, pw), lambda b, h, ti, tj: (b, P_Q // pw + h)),
            pl.BlockSpec((seq, pw), lambda b, h, ti, tj: (b, P_K // pw + h)),
            pl.BlockSpec((seq // TK, 2 * V_ROWS, TK), lambda b, h, ti, tj: (b, h, 0)),
        ],
        out_specs=pl.BlockSpec((seq, LANES), lambda b, h, ti, tj: (b, h)),
        scratch_shapes=[pltpu.VMEM((2, TK, TQ), F32), pltpu.VMEM((2, TK, TQ), F32),
                        pltpu.VMEM((nq, 2, 1, TQ), F32), pltpu.VMEM((nq, 2, V_ROWS, TQ), F32),
                        pltpu.VMEM((TK, TQ), F32)],
    )
    return pl.pallas_call(
        functools.partial(_attn_kernel, n_below=len(below), n_diag=len(diag)),
        grid_spec=grid_spec,
        out_shape=jax.ShapeDtypeStruct((n, ATT_WIDTH), BF16),
        compiler_params=_params(40, 2),
        name="fox_attention",
    )(ti, tj, p, p, vt)


def _merge_kernel(x_ref, hc_ref, ha_ref, hs_ref, g0_ref, g1_ref, g2_ref, bg_ref,
                  wc_ref, wa_ref, ws_ref, wm_ref, o_ref):
    merged = None
    for k, (h_ref, w_ref, gl_ref) in enumerate(((hc_ref, wc_ref, g0_ref), (ha_ref, wa_ref, g1_ref),
                                                (hs_ref, ws_ref, g2_ref))):
        y = jnp.dot(h_ref[...], w_ref[...], preferred_element_type=F32)
        gate = _sigmoid(gl_ref[...].astype(F32) + bg_ref[:, k * D_MODEL:(k + 1) * D_MODEL])
        merged = gate * y if merged is None else merged + gate * y
    o_ref[...] = x_ref[...] + jnp.dot(merged.astype(BF16), wm_ref[...], preferred_element_type=F32)


def _merge(x, hc, ha, hs, p, bg, wc, wa, ws, wm, layer):
    n = x.shape[0]
    tm = TM_BRANCH
    g0 = P_GATE // D_MODEL
    half = lambda: pl.BlockSpec((tm, CONV_CH), lambda i: (i, 0))
    wspec = lambda k: pl.BlockSpec((None, k, D_MODEL), lambda i: (layer, 0, 0))
    return pl.pallas_call(
        _merge_kernel,
        grid=(n // tm,),
        in_specs=[
            pl.BlockSpec((tm, D_MODEL), lambda i: (i, 0)),
            half(), half(), half(),
            pl.BlockSpec((tm, D_MODEL), lambda i: (i, g0)),
            pl.BlockSpec((tm, D_MODEL), lambda i: (i, g0 + 1)),
            pl.BlockSpec((tm, D_MODEL), lambda i: (i, g0 + 2)),
            pl.BlockSpec((1, N_BRANCH * D_MODEL), lambda i: (0, 0)),
            wspec(CONV_CH), wspec(ATT_WIDTH), wspec(SG_WIDTH), wspec(D_MODEL),
        ],
        out_specs=pl.BlockSpec((tm, D_MODEL), lambda i: (i, 0)),
        out_shape=jax.ShapeDtypeStruct((n, D_MODEL), F32),
        compiler_params=_params(48, 1),
        name="merge",
    )(x, hc, ha, hs, p, p, p, bg, wc, wa, ws, wm)


def _ffn_kernel(x_ref, g_ref, w1_ref, w3_ref, w2_ref, o_ref, hn_ref, wb1_ref, wb3_ref, wb2_ref):
    f = pl.program_id(1)

    @pl.when(pl.program_id(0) == 0)
    def _():
        wb1_ref[f] = w1_ref[...].astype(BF16)
        wb3_ref[f] = w3_ref[...].astype(BF16)
        wb2_ref[f] = w2_ref[...].astype(BF16)

    @pl.when(f == 0)
    def _():
        hn_ref[...] = _rms(x_ref[...], g_ref[...]).astype(BF16)
        o_ref[...] = x_ref[...]

    hn = hn_ref[...]
    a = jnp.dot(hn, wb1_ref[f], preferred_element_type=F32)
    b = jnp.dot(hn, wb3_ref[f], preferred_element_type=F32)
    t = (a * _sigmoid(a) * b).astype(BF16)
    o_ref[...] += jnp.dot(t, wb2_ref[f], preferred_element_type=F32)


def _ffn(x, g, w1, w3, w2, layer):
    n = x.shape[0]
    tm, tf = TM_FFN, TF_FFN
    nf = D_FF // tf

    def f_idx(i, f):
        return jnp.where(i == 0, f, nf - 1)

    return pl.pallas_call(
        _ffn_kernel,
        grid=(n // tm, nf),
        in_specs=[
            pl.BlockSpec((tm, D_MODEL), lambda i, f: (i, 0)),
            pl.BlockSpec((1, D_MODEL), lambda i, f: (0, 0)),
            pl.BlockSpec((None, D_MODEL, tf), lambda i, f: (layer, 0, f_idx(i, f))),
            pl.BlockSpec((None, D_MODEL, tf), lambda i, f: (layer, 0, f_idx(i, f))),
            pl.BlockSpec((None, tf, D_MODEL), lambda i, f: (layer, f_idx(i, f), 0)),
        ],
        out_specs=pl.BlockSpec((tm, D_MODEL), lambda i, f: (i, 0)),
        out_shape=jax.ShapeDtypeStruct((n, D_MODEL), F32),
        scratch_shapes=[pltpu.VMEM((tm, D_MODEL), BF16),
                        pltpu.VMEM((nf, D_MODEL, tf), BF16), pltpu.VMEM((nf, D_MODEL, tf), BF16),
                        pltpu.VMEM((nf, tf, D_MODEL), BF16)],
        compiler_params=_params(56, 2),
        name="ffn_dense",
    )(x, g, w1, w3, w2)


def _router_kernel(x_ref, g_ref, wr_ref, lt_ref, h_ref, meta_ref, cnt_ref, carry_ref):
    i = pl.program_id(0)
    tm = x_ref.shape[0]

    @pl.when(i == 0)
    def _():
        carry_ref[...] = jnp.zeros_like(carry_ref)

    h = _rms(x_ref[...], g_ref[...])
    h_ref[...] = h
    h_hi = h.astype(BF16)
    h_lo = (h - h_hi.astype(F32)).astype(BF16)
    w = wr_ref[...]
    w_hi = w.astype(BF16)
    w_lo = (w - w_hi.astype(F32)).astype(BF16)
    logits = (jnp.dot(h_hi, w_hi, preferred_element_type=F32)
              + jnp.dot(h_hi, w_lo, preferred_element_type=F32)
              + jnp.dot(h_lo, w_hi, preferred_element_type=F32))
    lane = lax.broadcasted_iota(jnp.int32, (tm, LANES), 1)
    lanef = lane.astype(F32)
    lg = jnp.where(lane < N_EXPERTS, logits, -jnp.inf)
    v1 = jnp.max(lg, axis=1, keepdims=True)
    i1 = jnp.min(jnp.where(lg == v1, lanef, float(LANES)), axis=1, keepdims=True)
    lg2 = jnp.where(lanef == i1, -jnp.inf, lg)
    v2 = jnp.max(lg2, axis=1, keepdims=True)
    i2 = jnp.min(jnp.where(lg2 == v2, lanef, float(LANES)), axis=1, keepdims=True)
    e = jnp.exp(v2 - v1)
    g1 = 1.0 / (1.0 + e)
    g2 = e / (1.0 + e)
    oh1 = lanef == i1
    oh2 = lanef == i2
    cnt = (oh1.astype(F32) + oh2.astype(F32))
    before = jnp.dot(lt_ref[...], cnt.astype(BF16), preferred_element_type=F32) + carry_ref[...]
    r1 = jnp.sum(jnp.where(oh1, before, 0.0), axis=1, keepdims=True)
    r2 = jnp.sum(jnp.where(oh2, before, 0.0), axis=1, keepdims=True)
    total = carry_ref[...] + jnp.sum(cnt, axis=0, keepdims=True)
    carry_ref[...] = total
    cnt_ref[...] = jnp.broadcast_to(total, cnt_ref.shape)
    meta = jnp.zeros((tm, LANES), F32)
    for k, val in enumerate((i1, i2, g1, g2, r1, r2)):
        meta = jnp.where(lane == k, val, meta)
    meta_ref[...] = meta


def _router(x, g, wr, lt):
    n = x.shape[0]
    tm = TM_BRANCH
    return pl.pallas_call(
        _router_kernel,
        grid=(n // tm,),
        in_specs=[
            pl.BlockSpec((tm, D_MODEL), lambda i: (i, 0)),
            pl.BlockSpec((1, D_MODEL), lambda i: (0, 0)),
            pl.BlockSpec((D_MODEL, LANES), lambda i: (0, 0)),
            pl.BlockSpec((tm, tm), lambda i: (0, 0)),
        ],
        out_specs=[
            pl.BlockSpec((tm, D_MODEL), lambda i: (i, 0)),
            pl.BlockSpec((tm, LANES), lambda i: (i, 0)),
            pl.BlockSpec((8, LANES), lambda i: (0, 0)),
        ],
        out_shape=[
            jax.ShapeDtypeStruct((n, D_MODEL), F32),
            jax.ShapeDtypeStruct((n, LANES), F32),
            jax.ShapeDtypeStruct((8, LANES), F32),
        ],
        scratch_shapes=[pltpu.VMEM((1, LANES), F32)],
        compiler_params=_params(32, 1),
        name="router",
    )(x, g, wr, lt)


def _row_copy(src_ref, src_row, dst_ref, dst_row, sem):
    return pltpu.make_async_copy(src_ref.at[pl.ds(src_row, 1), :], dst_ref.at[pl.ds(dst_row, 1), :], sem)


def _dispatch_kernel(bounds_ref, dest_ref, h_ref, xs_ref, zero_ref, sem, zero_sem):
    tm = h_ref.shape[0]
    n_blk = xs_ref.shape[0] // MOE_BLK

    @pl.when(pl.program_id(0) == 0)
    def _():
        zero_ref[...] = jnp.zeros_like(zero_ref)

        def zero_block(blk):
            start = pl.multiple_of(blk * MOE_BLK, MOE_BLK)
            copy = pltpu.make_async_copy(zero_ref, xs_ref.at[pl.ds(start, MOE_BLK), :], zero_sem)
            copy.start()
            copy.wait()

        for e in range(N_EXPERTS):
            @pl.when(bounds_ref[e + 1] > bounds_ref[e])
            def _():
                zero_block(bounds_ref[e + 1] // MOE_BLK - 1)

        def unused(blk, carry):
            zero_block(blk)
            return carry

        lax.fori_loop(bounds_ref[N_EXPERTS] // MOE_BLK, n_blk, unused, 0)

    def issue(r, carry):
        for k in range(2):
            _row_copy(h_ref, r, xs_ref, dest_ref[2 * r + k], sem).start(priority=k)
        return carry

    lax.fori_loop(0, tm, issue, 0, unroll=ROW_ISSUE_UNROLL)
    for _ in range(2):
        pltpu.make_async_copy(h_ref, xs_ref.at[pl.ds(0, tm), :], sem).wait()


def _dispatch(bounds, dest_flat, h, n_rows):
    n = h.shape[0]
    tm = TM_ROW
    return pl.pallas_call(
        _dispatch_kernel,
        grid=(n // tm,),
        in_specs=[
            pl.BlockSpec(memory_space=pltpu.SMEM),
            pl.BlockSpec((2 * tm,), lambda i: (i,), memory_space=pltpu.SMEM),
            pl.BlockSpec((tm, D_MODEL), lambda i: (i, 0)),
        ],
        out_specs=pl.BlockSpec(memory_space=pl.ANY),
        out_shape=jax.ShapeDtypeStruct((n_rows, D_MODEL), F32),
        scratch_shapes=[pltpu.VMEM((MOE_BLK, D_MODEL), F32), pltpu.SemaphoreType.DMA(()),
                        pltpu.SemaphoreType.DMA(())],
        compiler_params=_params(32, 1),
        name="moe_dispatch",
    )(bounds, dest_flat, h)


def _first_block_of_expert(b, be_ref):
    return jnp.logical_or(b == 0, be_ref[b] != be_ref[jnp.maximum(b - 1, 0)])


def _moe_kernel(be_ref, nu_ref, x_ref, w1_ref, w3_ref, w2_ref, y_ref, xb_ref, wb1_ref, wb3_ref, wb2_ref):
    b = pl.program_id(0)
    f = pl.program_id(1)

    @pl.when(b < nu_ref[0])
    def _():
        @pl.when(_first_block_of_expert(b, be_ref))
        def _():
            wb1_ref[f] = w1_ref[...].astype(BF16)
            wb3_ref[f] = w3_ref[...].astype(BF16)
            wb2_ref[f] = w2_ref[...].astype(BF16)

        @pl.when(f == 0)
        def _():
            xb_ref[...] = x_ref[...].astype(BF16)
            y_ref[...] = jnp.zeros_like(y_ref)

        xb = xb_ref[...]
        a = jnp.dot(xb, wb1_ref[f], preferred_element_type=F32)
        c = jnp.dot(xb, wb3_ref[f], preferred_element_type=F32)
        t = (a * _sigmoid(a) * c).astype(BF16)
        y_ref[...] += jnp.dot(t, wb2_ref[f], preferred_element_type=F32)

    @pl.when(jnp.logical_and(b >= nu_ref[0], f == pl.num_programs(1) - 1))
    def _():
        y_ref[...] = jnp.zeros_like(y_ref)


def _moe_experts(blk_e, n_used, xs, w1, w3, w2, layer):
    rows = xs.shape[0]
    n_blk = rows // MOE_BLK
    nf = D_FF_EXPERT // TF_MOE

    def row_idx(b, f, be, nu):
        return (jnp.minimum(b, nu[0] - 1), 0)

    def f_idx(b, f, be, nu):
        fetch = jnp.logical_and(b < nu[0], _first_block_of_expert(b, be))
        return jnp.where(fetch, f, nf - 1)

    grid_spec = pltpu.PrefetchScalarGridSpec(
        num_scalar_prefetch=2,
        grid=(n_blk, nf),
        in_specs=[
            pl.BlockSpec((MOE_BLK, D_MODEL), row_idx),
            pl.BlockSpec((None, None, D_MODEL, TF_MOE),
                         lambda b, f, be, nu: (layer, be[b], 0, f_idx(b, f, be, nu))),
            pl.BlockSpec((None, None, D_MODEL, TF_MOE),
                         lambda b, f, be, nu: (layer, be[b], 0, f_idx(b, f, be, nu))),
            pl.BlockSpec((None, None, TF_MOE, D_MODEL),
                         lambda b, f, be, nu: (layer, be[b], f_idx(b, f, be, nu), 0)),
        ],
        out_specs=pl.BlockSpec((MOE_BLK, D_MODEL), lambda b, f, be, nu: (b, 0)),
        scratch_shapes=[pltpu.VMEM((MOE_BLK, D_MODEL), BF16),
                        pltpu.VMEM((nf, D_MODEL, TF_MOE), BF16), pltpu.VMEM((nf, D_MODEL, TF_MOE), BF16),
                        pltpu.VMEM((nf, TF_MOE, D_MODEL), BF16)],
    )
    return pl.pallas_call(
        _moe_kernel,
        grid_spec=grid_spec,
        out_shape=jax.ShapeDtypeStruct((rows, D_MODEL), F32),
        compiler_params=_params(56, 2),
        name="moe_experts",
    )(blk_e, n_used, xs, w1, w3, w2)


def _combine_kernel(dest_ref, x_ref, meta_ref, gfin_ref, y_ref, o_ref, ybuf_ref, sem, *, final_norm):
    tm = x_ref.shape[0]

    def issue(r, carry):
        for k in range(2):
            _row_copy(y_ref, dest_ref[2 * r + k], ybuf_ref.at[k], r, sem).start(priority=k)
        return carry

    lax.fori_loop(0, tm, issue, 0, unroll=ROW_ISSUE_UNROLL)
    for k in range(2):
        pltpu.make_async_copy(y_ref.at[pl.ds(0, tm), :], ybuf_ref.at[k], sem).wait()

    g1 = meta_ref[:, 2:3]
    g2 = meta_ref[:, 3:4]
    out = x_ref[...] + (g1 * ybuf_ref[0] + g2 * ybuf_ref[1])
    if final_norm:
        out = _rms(out, gfin_ref[...])
    o_ref[...] = out


def _combine(dest_flat, x, meta, gfin, y, final_norm):
    n = x.shape[0]
    tm = TM_ROW
    kern = functools.partial(_combine_kernel, final_norm=final_norm)
    return pl.pallas_call(
        kern,
        grid=(n // tm,),
        in_specs=[
            pl.BlockSpec((2 * tm,), lambda i: (i,), memory_space=pltpu.SMEM),
            pl.BlockSpec((tm, D_MODEL), lambda i: (i, 0)),
            pl.BlockSpec((tm, LANES), lambda i: (i, 0)),
            pl.BlockSpec((1, D_MODEL), lambda i: (0, 0)),
            pl.BlockSpec(memory_space=pl.ANY),
        ],
        out_specs=pl.BlockSpec((tm, D_MODEL), lambda i: (i, 0)),
        out_shape=jax.ShapeDtypeStruct((n, D_MODEL), F32),
        scratch_shapes=[pltpu.VMEM((2, tm, D_MODEL), F32), pltpu.SemaphoreType.DMA(())],
        compiler_params=_params(32, 1),
        name="moe_combine",
    )(dest_flat, x, meta, gfin, y)


def _moe_layer(x, g, wr, lt, w1, w3, w2, layer, gfin, final_norm):
    n = x.shape[0]
    h, meta, cnt = _router(x, g, wr, lt)
    expert = meta[:, 0:2].astype(jnp.int32)
    rank = meta[:, 4:6].astype(jnp.int32)
    counts = cnt[0, :N_EXPERTS].astype(jnp.int32)
    padded = (counts + MOE_BLK - 1) // MOE_BLK * MOE_BLK
    pad_end = jnp.cumsum(padded)
    pad_start = pad_end - padded
    dest = (pad_start[expert] + rank).reshape(-1)
    n_blk = (2 * n) // MOE_BLK + N_EXPERTS
    blk_start = jnp.arange(n_blk, dtype=jnp.int32) * MOE_BLK
    blk_e = jnp.minimum(jnp.sum(blk_start[:, None] >= pad_end[None, :], axis=1), N_EXPERTS - 1)
    n_used = (pad_end[-1:] // MOE_BLK).astype(jnp.int32)
    bounds = jnp.concatenate([jnp.zeros((1,), jnp.int32), pad_end.astype(jnp.int32)])
    xs = _dispatch(bounds, dest, h, n_blk * MOE_BLK)
    y = _moe_experts(blk_e.astype(jnp.int32), n_used, xs, w1, w3, w2, layer)
    return _combine(dest, x, meta, gfin, y, final_norm)


def _prepare_mixer(w_in, b_forget, conv_w, w_spatial, b_spatial, w_conv_out, w_att_out, w_sg_out,
                   w_mix_out):
    depth, d, _ = w_in.shape

    def pad_heads(w):
        w = w.reshape(depth, d, ATT_HEADS, HEAD_DIM)
        w = jnp.pad(w, ((0, 0), (0, 0), (0, 0), (0, HEAD_PAD - HEAD_DIM)))
        return w.reshape(depth, d, ATT_HEADS * HEAD_PAD)

    prep = {}
    prep["w_main"] = jnp.concatenate(
        [pad_heads(w_in[:, :, OFF_K:OFF_V]), w_in[:, :, OFF_CONV:OFF_Q], w_in[:, :, OFF_SG:OFF_GATE],
         w_in[:, :, OFF_GATE:], pad_heads(w_in[:, :, OFF_Q:OFF_K])], axis=2).astype(BF16)
    col = jnp.arange(P_COLS)
    is_q = col >= P_Q
    spare = (col % HEAD_PAD >= HEAD_DIM) & (col % HEAD_PAD < HEAD_DIM + N_SPLIT)
    prep["col_scale"] = jnp.where(is_q, LOG2E * HEAD_DIM ** -0.5, 1.0).astype(F32)[None, :]
    prep["col_bias"] = jnp.where(is_q & spare, 1.0, 0.0).astype(F32)[None, :]
    kcol = jnp.arange(ATT_HEADS * HEAD_PAD)
    place = -((kcol[None, None, :] // HEAD_PAD == jnp.arange(LANES)[None, :, None])
              & (kcol[None, None, :] % HEAD_PAD == HEAD_DIM + jnp.arange(N_SPLIT)[:, None, None])
              ).astype(BF16)
    prep["place"] = place.reshape(N_SPLIT * LANES, ATT_HEADS * HEAD_PAD)
    prep["wf"] = jnp.pad(w_in[:, :, OFF_F:OFF_SG], ((0, 0), (0, 0), (0, LANES - ATT_HEADS))).astype(BF16)
    prep["bf"] = jnp.pad(b_forget, ((0, 0), (0, LANES - ATT_HEADS)))[:, None, :]
    wv_t = jnp.swapaxes(w_in[:, :, OFF_V:OFF_F], 1, 2).reshape(depth, ATT_HEADS // 2, 2, HEAD_DIM, d)
    zeros_h = jnp.zeros_like(wv_t[:, :, 0, :V_ROWS - HEAD_DIM])
    wvt = jnp.concatenate([wv_t[:, :, 0], zeros_h, zeros_h, wv_t[:, :, 1]], axis=2)
    prep["wvt"] = wvt.reshape(depth, ATT_HEADS * V_ROWS, d).astype(BF16)
    vrow = jnp.arange(ATT_HEADS * V_ROWS) % (2 * V_ROWS)
    prep["vbias"] = ((vrow == HEAD_DIM) | (vrow == V_ROWS)).astype(F32)[:, None]
    prep["lt_incl"] = jnp.tril(jnp.ones((LANES, LANES), F32)).astype(BF16)
    prep["conv_w"] = jnp.pad(conv_w, ((0, 0), (0, 1), (0, 0)))
    causal = jnp.tril(jnp.ones((SG_CHUNK, SG_CHUNK), bool))
    prep["w_sp"] = jnp.where(causal[None, None], w_spatial, 0).astype(BF16)
    prep["sg_bias"] = jnp.repeat(jnp.swapaxes(b_spatial, 1, 2), SG_WIDTH // SG_GROUPS, axis=2)
    for name, w in (("wc", w_conv_out), ("wa", w_att_out), ("ws", w_sg_out), ("wm", w_mix_out)):
        prep[name] = w.astype(BF16)
    return prep


def _token_mixer(xf, prep, vecs, layer, batch, seq):
    row = lambda name: vecs[name][layer][None, :]
    p, vt = _inproj(xf, row("mix_norm_g"), prep["w_main"], prep["col_scale"], prep["col_bias"],
                    prep["wf"], prep["bf"][layer], prep["lt_incl"], prep["place"],
                    prep["wvt"], prep["vbias"], seq, layer)
    hc = _conv_branch(p, prep["conv_w"][layer], row("conv_b"), row("conv_ln_g"), row("conv_ln_b"), seq)
    hs = _sg_branch(p, row("sg_ln_g"), row("sg_ln_b"), prep["w_sp"][layer], prep["sg_bias"][layer])
    ha = _attention(p, vt, batch, seq)
    return _merge(xf, hc, ha, hs, p, row("b_gate"), prep["wc"], prep["wa"], prep["ws"], prep["wm"], layer)


def kernel(x, mix_norm_g, w_in, b_forget, b_gate, conv_w, conv_b, conv_ln_g, conv_ln_b, w_conv_out,
           w_att_out, sg_ln_g, sg_ln_b, w_spatial, b_spatial, w_sg_out, w_mix_out, ffn_norm_g,
           ffn_w1, ffn_w3, ffn_w2, router_w, moe_w1, moe_w3, moe_w2, final_norm_g):
    batch, seq, d = x.shape
    depth = w_in.shape[0]
    if depth % 2 == 1:
        raise NotImplementedError("the final norm is fused into the last (expert) layer")
    xf = x.reshape(batch * seq, d)

    prep = _prepare_mixer(w_in, b_forget, conv_w, w_spatial, b_spatial, w_conv_out, w_att_out,
                          w_sg_out, w_mix_out)
    vecs = dict(mix_norm_g=mix_norm_g, conv_b=conv_b, conv_ln_g=conv_ln_g, conv_ln_b=conv_ln_b,
                sg_ln_g=sg_ln_g, sg_ln_b=sg_ln_b, b_gate=b_gate)
    f1, f3, f2 = ffn_w1, ffn_w3, ffn_w2
    m1, m3, m2 = moe_w1, moe_w3, moe_w2
    wr = jnp.pad(router_w, ((0, 0), (0, 0), (0, LANES - N_EXPERTS)))
    lt = jnp.tril(jnp.ones((TM_BRANCH, TM_BRANCH), F32), -1).astype(BF16)
    gfin = final_norm_g[None, :]

    for layer in range(depth):
        xf = _token_mixer(xf, prep, vecs, layer, batch, seq)
        g_ffn = ffn_norm_g[layer][None, :]
        i = layer // 2
        if layer % 2 == 0:
            xf = _ffn(xf, g_ffn, f1, f3, f2, i)
        else:
            xf = _moe_layer(xf, g_ffn, wr[i], lt, m1, m3, m2, i, gfin,
                            final_norm=(layer == depth - 1))
    return xf.reshape(batch, seq, d)
```
